```python
import math
import jax
import jax.numpy as jnp
from jax import lax
import numpy as np

D_MODEL = 2048
BATCH = 8
SEQ = 2048
DEPTH = 2

HEAD_DIM = 128
Q_BLOCK = 128
A_GROUPS = ((128, 1), (512, 4), (2048, 16))
A_HEADS_PER_GROUP = 2
A_HEADS = A_HEADS_PER_GROUP * len(A_GROUPS)
A_OUT = A_HEADS_PER_GROUP * HEAD_DIM
B_HEADS = 4
CMP_LEN = 32
CMP_STRIDE = 16
SLC_LEN = 64
N_SELECT = 16
WIN = 512
B_OUT = B_HEADS * HEAD_DIM
C_HEADS = 6
C_OUT = C_HEADS * HEAD_DIM
N_BUCKETS = 32
MAX_DISTANCE = 2048
BIAS_HEADS = A_HEADS + B_HEADS
D_FF = 5632
CONV_W = 3
ALPHA = (2 * DEPTH) ** 0.25
BETA = (8 * DEPTH) ** -0.25
LN_EPS = 1e-5
NEG_INF = -1e30
FORCE_SCORE = 1e9
ATTN_SCALE = HEAD_DIM ** -0.5

IN_SPLITS = (
    ('a_q', A_HEADS * HEAD_DIM), ('a_k', A_HEADS * HEAD_DIM), ('a_v', A_HEADS * HEAD_DIM),
    ('b_q', B_HEADS * HEAD_DIM),
    ('b_k_cmp', HEAD_DIM), ('b_v_cmp', HEAD_DIM),
    ('b_k_slc', HEAD_DIM), ('b_v_slc', HEAD_DIM),
    ('b_k_win', HEAD_DIM), ('b_v_win', HEAD_DIM),
    ('b_gate', 3 * B_HEADS),
    ('c_q', C_HEADS * HEAD_DIM), ('c_k', C_HEADS * HEAD_DIM), ('c_v', C_HEADS * HEAD_DIM),
    ('c_f', C_HEADS),
)
V_COLUMNS = ('a_v', 'b_v_cmp', 'b_v_slc', 'b_v_win', 'c_v')
N_IN = sum(w for _, w in IN_SPLITS)

kernel_name = 'hybrid_gated_dilated_nsa_fox_block'


def t5_bucket(dist):
    max_exact = N_BUCKETS // 2
    d = jnp.maximum(dist, 0)
    log_ratio = jnp.log(jnp.maximum(d, 1).astype(jnp.float32) / max_exact) / math.log(MAX_DISTANCE / max_exact)
    large = jnp.minimum(max_exact + (log_ratio * (N_BUCKETS - max_exact)).astype(jnp.int32), N_BUCKETS - 1)
    return jnp.where(d < max_exact, d, large)


def rel_bias_lookup(table, dist):
    return table[t5_bucket(dist)].astype(jnp.float32)


def layer_norm(x, g, b):
    xf = x.astype(jnp.float32)
    mu = jnp.mean(xf, axis=-1, keepdims=True)
    var = jnp.mean(jnp.square(xf - mu), axis=-1, keepdims=True)
    return ((xf - mu) * lax.rsqrt(var + LN_EPS) * g + b).astype(x.dtype)


def split_columns(u):
    offs = np.cumsum([w for _, w in IN_SPLITS])[:-1].tolist()
    return dict(zip([n for n, _ in IN_SPLITS], jnp.split(u, offs, axis=-1)))


def unblock(o):
    o = jnp.moveaxis(o, 0, 1)
    return o.reshape(o.shape[0], o.shape[1] * o.shape[2], *o.shape[3:])


def dilated_mixer(q, k, v, bias_tab):
    bsz, seq = q.shape[0], q.shape[1]
    n_blk = seq // Q_BLOCK
    outs, lses = [], []
    for g, (window, dil) in enumerate(A_GROUPS):
        hs = slice(g * A_HEADS_PER_GROUP, (g + 1) * A_HEADS_PER_GROUP)
        qg, kg, vg = q[:, :, hs], k[:, :, hs], v[:, :, hs]
        dist = jnp.arange(window // dil + 1) * dil
        bias = jnp.moveaxis(rel_bias_lookup(bias_tab[:, hs], dist), -1, 0)

        def block(i):
            t = i * Q_BLOCK + jnp.arange(Q_BLOCK)
            key_pos = t[:, None] - dist[None, :]
            idx = jnp.maximum(key_pos, 0)
            qb = lax.dynamic_slice_in_dim(qg, i * Q_BLOCK, Q_BLOCK, axis=1)
            kb, vb = kg[:, idx], vg[:, idx]
            s = jnp.einsum('bqhd,bqkhd->bhqk', qb, kb).astype(jnp.float32) * ATTN_SCALE + bias[None, :, None, :]
            s = jnp.where((key_pos >= 0)[None, None], s, NEG_INF)
            lse = jax.nn.logsumexp(s, axis=-1)
            p = jnp.exp(s - lse[..., None]).astype(vb.dtype)
            o = jnp.einsum('bhqk,bqkhd->bqhd', p, vb)
            return o, jnp.transpose(lse, (0, 2, 1))

        o_g, lse_g = lax.map(block, jnp.arange(n_blk))
        outs.append(unblock(o_g))
        lses.append(unblock(lse_g))
    w = jax.nn.softmax(jnp.stack(lses), axis=0)
    o = jnp.einsum('gbsh,gbshd->bshd', w.astype(v.dtype), jnp.stack(outs))
    return o.reshape(bsz, seq, A_OUT)


def nsa_mixer(q, k_cmp_src, v_cmp_src, k_slc, v_slc, k_win, v_win, gate_logits, b_gate,
              cmp_pe, cmp_w1, cmp_b1, cmp_w2, cmp_b2, bias_tab):
    bsz, seq = q.shape[0], q.shape[1]
    n_blk = seq // Q_BLOCK
    n_cmp = (seq - CMP_LEN) // CMP_STRIDE + 1
    n_slc = seq // SLC_LEN
    k_sel = min(N_SELECT, n_slc)
    pos = jnp.arange(seq)
    cidx = jnp.arange(n_cmp)[:, None] * CMP_STRIDE + jnp.arange(CMP_LEN)[None, :]
    kv = jnp.stack([k_cmp_src, v_cmp_src], axis=1)[:, :, cidx] + cmp_pe[None, :, None]
    kv = kv.reshape(bsz, 2, n_cmp, CMP_LEN * HEAD_DIM)
    hid = jax.nn.gelu(jnp.einsum('bcnf,cfe->bcne', kv, cmp_w1) + cmp_b1[None, :, None], approximate=False)
    kv_c = jnp.einsum('bcne,ced->bcnd', hid, cmp_w2) + cmp_b2[None, :, None]
    k_c, v_c = kv_c[:, 0], kv_c[:, 1]
    dist_c = pos[:, None] - (jnp.arange(n_cmp) * CMP_STRIDE + CMP_LEN - 1)[None, :]
    valid_c = (dist_c >= 0)[None, None]
    bias_c = jnp.moveaxis(rel_bias_lookup(bias_tab, dist_c), -1, 0)[None]
    s_c = jnp.einsum('bshd,bnd->bhsn', q, k_c).astype(jnp.float32) * ATTN_SCALE + bias_c
    p_c = jnp.where(valid_c, jax.nn.softmax(jnp.where(valid_c, s_c, NEG_INF), axis=-1), 0.0)
    o_cmp = jnp.einsum('bhsn,bnd->bshd', p_c.astype(v_c.dtype), v_c)
    ci = np.arange(n_cmp)[:, None]
    sj = np.arange(n_slc)[None, :]
    overlap = ((ci * CMP_STRIDE <= sj * SLC_LEN + SLC_LEN - 1)
               & (ci * CMP_STRIDE + CMP_LEN - 1 >= sj * SLC_LEN)).astype(np.float32)
    imp = jnp.einsum('bhsn,nj->bsj', p_c, jnp.asarray(overlap))
    blk = jnp.arange(n_slc)[None, :]
    cur = (pos // SLC_LEN)[:, None]
    forced = (blk == 0) | (blk == cur) | (blk == cur - 1)
    causal = blk * SLC_LEN <= pos[:, None]
    score = jnp.where(forced[None], FORCE_SCORE, jnp.where(causal[None], imp, NEG_INF))
    _, sel = lax.top_k(score, k_sel)
    k_blocks = k_slc.reshape(bsz, n_slc, SLC_LEN, HEAD_DIM)
    v_blocks = v_slc.reshape(bsz, n_slc, SLC_LEN, HEAD_DIM)
    kw_pad = jnp.pad(k_win, ((0, 0), (WIN, 0), (0, 0)))
    vw_pad = jnp.pad(v_win, ((0, 0), (WIN, 0), (0, 0)))
    q_off = jnp.arange(Q_BLOCK)
    dist_w = q_off[:, None] + WIN - jnp.arange(Q_BLOCK + WIN)[None, :]
    in_win = (dist_w >= 0) & (dist_w < WIN)
    bias_w = jnp.moveaxis(rel_bias_lookup(bias_tab, dist_w), -1, 0)[None]
    gather_blocks = jax.vmap(lambda blocks, ids: blocks[ids])

    def block(i):
        t0 = i * Q_BLOCK
        t = t0 + q_off
        qb = lax.dynamic_slice_in_dim(q, t0, Q_BLOCK, axis=1)
        sb = lax.dynamic_slice_in_dim(sel, t0, Q_BLOCK, axis=1)
        kb = gather_blocks(k_blocks, sb).reshape(bsz, Q_BLOCK, k_sel * SLC_LEN, HEAD_DIM)
        vb = gather_blocks(v_blocks, sb).reshape(bsz, Q_BLOCK, k_sel * SLC_LEN, HEAD_DIM)
        key_pos = (sb[..., None] * SLC_LEN + jnp.arange(SLC_LEN)).reshape(bsz, Q_BLOCK, k_sel * SLC_LEN)
        dist_s = t[None, :, None] - key_pos
        bias_s = jnp.moveaxis(rel_bias_lookup(bias_tab, dist_s), -1, 1)
        s_s = jnp.einsum('bqhd,bqnd->bhqn', qb, kb).astype(jnp.float32) * ATTN_SCALE + bias_s
        p_s = jax.nn.softmax(jnp.where((dist_s >= 0)[:, None], s_s, NEG_INF), axis=-1)
        o_s = jnp.einsum('bhqn,bqnd->bqhd', p_s.astype(vb.dtype), vb)
        kw = lax.dynamic_slice_in_dim(kw_pad, t0, Q_BLOCK + WIN, axis=1)
        vw = lax.dynamic_slice_in_dim(vw_pad, t0, Q_BLOCK + WIN, axis=1)
        valid_w = in_win & (t0 - WIN + jnp.arange(Q_BLOCK + WIN) >= 0)[None, :]
        s_w = jnp.einsum('bqhd,bnd->bhqn', qb, kw).astype(jnp.float32) * ATTN_SCALE + bias_w
        p_w = jax.nn.softmax(jnp.where(valid_w[None, None], s_w, NEG_INF), axis=-1)
        o_w = jnp.einsum('bhqn,bnd->bqhd', p_w.astype(vw.dtype), vw)
        return o_s, o_w

    o_slc, o_win = lax.map(block, jnp.arange(n_blk))
    o_slc, o_win = unblock(o_slc), unblock(o_win)
    g = jax.nn.sigmoid(gate_logits + b_gate).reshape(bsz, seq, 3, B_HEADS)[..., None]
    o = g[:, :, 0] * o_cmp + g[:, :, 1] * o_slc + g[:, :, 2] * o_win
    return o.reshape(bsz, seq, B_OUT)


def forgetting_mixer(q, k, v, f_logits, b_f):
    bsz, seq = q.shape[0], q.shape[1]
    n_blk = seq // Q_BLOCK
    log_f = jax.nn.log_sigmoid(f_logits.astype(jnp.float32) + b_f)
    c = jnp.moveaxis(jnp.cumsum(log_f, axis=1), -1, 1)
    key_pos = jnp.arange(seq)

    def block(i):
        t0 = i * Q_BLOCK
        t = t0 + jnp.arange(Q_BLOCK)
        qb = lax.dynamic_slice_in_dim(q, t0, Q_BLOCK, axis=1)
        c_q = lax.dynamic_slice_in_dim(c, t0, Q_BLOCK, axis=2)
        s = (jnp.einsum('bqhd,bshd->bhqs', qb, k).astype(jnp.float32) * ATTN_SCALE
             + c_q[..., None] - c[:, :, None, :])
        s = jnp.where((key_pos[None, :] <= t[:, None])[None, None], s, NEG_INF)
        p = jax.nn.softmax(s, axis=-1)
        return jnp.einsum('bhqs,bshd->bqhd', p.astype(v.dtype), v)

    o = unblock(lax.map(block, jnp.arange(n_blk)))
    return o.reshape(bsz, seq, C_OUT)


def conv_ffn(x, w_up, conv_w, conv_b, w_down):
    seq = x.shape[1]
    h = jnp.einsum('bsd,df->bsf', x, w_up)
    hp = jnp.pad(h, ((0, 0), (CONV_W - 1, 0), (0, 0)))
    h = conv_b + sum(conv_w[j] * hp[:, j:j + seq] for j in range(CONV_W))
    a, b = jnp.split(h, 2, axis=-1)
    return jnp.einsum('bsf,fd->bsd', jax.nn.gelu(a, approximate=False) * b, w_down)


def hybrid_layer(x, rel_bias, w_in, b_f, b_nsa_gate, cmp_pe, cmp_w1, cmp_b1, cmp_w2, cmp_b2,
                 w_gate, b_gate, w_pa, w_pb, w_pc, w_out, ln1_g, ln1_b,
                 w_up, conv_w, conv_b, w_down, ln2_g, ln2_b):
    bsz, seq, _ = x.shape
    u = split_columns(jnp.einsum('bsd,dn->bsn', x, w_in))

    def heads(t, h):
        return t.reshape(bsz, seq, h, HEAD_DIM)

    o_a = dilated_mixer(heads(u['a_q'], A_HEADS), heads(u['a_k'], A_HEADS), heads(u['a_v'], A_HEADS),
                        rel_bias[:, :A_HEADS])
    o_b = nsa_mixer(heads(u['b_q'], B_HEADS), u['b_k_cmp'], u['b_v_cmp'], u['b_k_slc'], u['b_v_slc'],
                    u['b_k_win'], u['b_v_win'], u['b_gate'], b_nsa_gate,
                    cmp_pe, cmp_w1, cmp_b1, cmp_w2, cmp_b2, rel_bias[:, A_HEADS:])
    o_c = forgetting_mixer(heads(u['c_q'], C_HEADS), heads(u['c_k'], C_HEADS), heads(u['c_v'], C_HEADS),
                           u['c_f'], b_f)
    g_a, g_b, g_c = jnp.split(jax.nn.sigmoid(jnp.einsum('bsd,de->bse', x, w_gate) + b_gate), 3, axis=-1)
    mixed = g_a * (o_a @ w_pa) + g_b * (o_b @ w_pb) + g_c * (o_c @ w_pc)
    x = layer_norm(ALPHA * x + mixed @ w_out, ln1_g, ln1_b)
    x = layer_norm(ALPHA * x + conv_ffn(x, w_up, conv_w, conv_b, w_down), ln2_g, ln2_b)
    return x


def setup_inputs(seed: int = 0) -> dict:
    key = jax.random.key(seed)
    ks = jax.random.split(key, 24)
    f32 = jnp.float32
    L, D, dh = DEPTH, D_MODEL, HEAD_DIM

    def nrm(k, shape, fan_in, gain=1.0):
        return jax.random.normal(k, shape, f32) * (gain * fan_in ** -0.5)

    def small(k, shape, s):
        return jax.random.normal(k, shape, f32) * s

    col_scale = np.concatenate([np.full(w, BETA if n in V_COLUMNS else 1.0, np.float32) for n, w in IN_SPLITS])
    return {
        'x': jax.random.normal(ks[0], (BATCH, SEQ, D), f32),
        'rel_bias': small(ks[1], (N_BUCKETS, BIAS_HEADS), 0.5),
        'w_in': nrm(ks[2], (L, D, N_IN), D) * jnp.asarray(col_scale),
        'b_f': jax.random.uniform(ks[3], (L, C_HEADS), f32, 1.0, 6.0),
        'b_nsa_gate': small(ks[4], (L, 3 * B_HEADS), 0.1),
        'cmp_pe': small(ks[5], (L, 2, CMP_LEN, dh), 0.1),
        'cmp_w1': nrm(ks[6], (L, 2, CMP_LEN * dh, dh), CMP_LEN * dh),
        'cmp_b1': small(ks[7], (L, 2, dh), 0.02),
        'cmp_w2': nrm(ks[8], (L, 2, dh, dh), dh),
        'cmp_b2': small(ks[9], (L, 2, dh), 0.02),
        'w_gate': nrm(ks[10], (L, D, 3 * D), D),
        'b_gate': small(ks[11], (L, 3 * D), 0.1),
        'w_pa': nrm(ks[12], (L, A_OUT, D), A_OUT, BETA),
        'w_pb': nrm(ks[13], (L, B_OUT, D), B_OUT, BETA),
        'w_pc': nrm(ks[14], (L, C_OUT, D), C_OUT, BETA),
        'w_out': nrm(ks[15], (L, D, D), D, BETA),
        'ln1_g': 1.0 + small(ks[16], (L, D), 0.05),
        'ln1_b': small(ks[17], (L, D), 0.02),
        'w_up': nrm(ks[18], (L, D, 2 * D_FF), D, BETA),
        'conv_w': nrm(ks[19], (L, CONV_W, 2 * D_FF), CONV_W),
        'conv_b': small(ks[20], (L, 2 * D_FF), 0.02),
        'w_down': nrm(ks[21], (L, D_FF, D), D_FF, BETA),
        'ln2_g': 1.0 + small(ks[22], (L, D), 0.05),
        'ln2_b': small(ks[23], (L, D), 0.02),
    }


def reference(x, rel_bias, w_in, b_f, b_nsa_gate, cmp_pe, cmp_w1, cmp_b1, cmp_w2, cmp_b2,
              w_gate, b_gate, w_pa, w_pb, w_pc, w_out, ln1_g, ln1_b,
              w_up, conv_w, conv_b, w_down, ln2_g, ln2_b):
    for l in range(DEPTH):
        x = hybrid_layer(x, rel_bias, w_in[l], b_f[l], b_nsa_gate[l], cmp_pe[l], cmp_w1[l], cmp_b1[l],
                         cmp_w2[l], cmp_b2[l], w_gate[l], b_gate[l], w_pa[l], w_pb[l], w_pc[l], w_out[l],
                         ln1_g[l], ln1_b[l], w_up[l], conv_w[l], conv_b[l], w_down[l], ln2_g[l], ln2_b[l])
    return x
```

```python
import functools
import math

import numpy as np
import jax
import jax.numpy as jnp
from jax import lax
from jax.experimental import pallas as pl
from jax.experimental.pallas import tpu as pltpu

HEAD_DIM = 128
A_GROUPS = ((128, 1), (512, 4), (2048, 16))
A_HEADS_PER_GROUP = 2
A_HEADS = A_HEADS_PER_GROUP * len(A_GROUPS)
B_HEADS = 4
CMP_LEN = 32
CMP_STRIDE = 16
SLC_LEN = 64
N_SELECT = 16
WIN = 512
C_HEADS = 6
N_BUCKETS = 32
MAX_DISTANCE = 2048
CONV_W = 3
LN_EPS = 1e-5
NEG_INF = -1e30
FORCE_SCORE = 1e9
ATTN_SCALE = HEAD_DIM ** -0.5

LANES = 128
VMEM_LIMIT_BYTES = 56 * 2 ** 20
BF16_SUBLANES = 16

MXU_DTYPE = jnp.bfloat16
F32 = jnp.float32
BLK = LANES

_COL_A_Q, _COL_A_K, _COL_A_V = 0, 6, 12
_COL_C_Q, _COL_C_K, _COL_C_V = 18, 24, 30
_COL_B_Q = 36
_COL_B_KS, _COL_B_VS, _COL_B_KW, _COL_B_VW = 40, 41, 42, 43
_N_COL_BLOCKS = 44
_SMALL_BLOCKS = 3
_GATE_LANE0 = 8


def _params(*sem):
    return pltpu.CompilerParams(dimension_semantics=sem, vmem_limit_bytes=VMEM_LIMIT_BYTES)


def _dot(a, b):
    return jnp.dot(a, b, preferred_element_type=F32)


def _dot_nt(a, b):
    return lax.dot_general(a, b, (((1,), (1,)), ((), ())), preferred_element_type=F32)


def _bias_tile_kernel(base_ref, rmul_ref, cmul_ref, dmask_ref, dmax_ref, rmax_ref, head_ref,
                      rel_ref, out_ref):
    t = pl.program_id(0)
    row = lax.broadcasted_iota(jnp.int32, (BLK, BLK), 0)
    col = lax.broadcasted_iota(jnp.int32, (BLK, BLK), 1)
    dist = base_ref[t] + rmul_ref[t] * row + cmul_ref[t] * col
    valid = ((dist >= 0) & (dist <= dmax_ref[t]) & ((dist & dmask_ref[t]) == 0)
             & (row < rmax_ref[t]))
    max_exact = N_BUCKETS // 2
    d = jnp.maximum(dist, 0)
    log_ratio = (jnp.log(jnp.maximum(d, 1).astype(F32) / max_exact)
                 / math.log(MAX_DISTANCE / max_exact))
    large = jnp.minimum(max_exact + (log_ratio * (N_BUCKETS - max_exact)).astype(jnp.int32),
                        N_BUCKETS - 1)
    bucket = jnp.where(d < max_exact, d, large)
    h = head_ref[t]
    val = jnp.zeros((BLK, BLK), F32)
    for b in range(N_BUCKETS):
        val = jnp.where(bucket == b, rel_ref[b, h], val)
    out_ref[0] = jnp.where(valid, val, NEG_INF)


def _a_tile_counts(nblk):
    return [min(window // BLK + 1, nblk) for window, _ in A_GROUPS]


def _tile_tables(nblk):
    big = 2 ** 30
    fam = {}
    fam["a"] = [(dl * BLK, 1, -1, dil - 1, window, BLK, g * A_HEADS_PER_GROUP + hh)
                for g, (window, dil) in enumerate(A_GROUPS)
                for hh in range(A_HEADS_PER_GROUP)
                for dl in range(_a_tile_counts(nblk)[g])]
    fam["slc"] = [(dl * BLK, -1, 1, 0, big, BLK, A_HEADS + h)
                  for dl in range(nblk) for h in range(B_HEADS)]
    fam["win"] = [(dl * BLK, -1, 1, 0, WIN - 1, BLK, A_HEADS + h)
                  for dl in range(WIN // BLK + 1) for h in range(B_HEADS)]
    n_cmp = (nblk * BLK - CMP_LEN) // CMP_STRIDE + 1
    fam["cmp"] = [(i * BLK - (CMP_LEN - 1), -CMP_STRIDE, 1, 0, big, n_cmp, A_HEADS + h)
                  for i in range(nblk) for h in range(B_HEADS)]
    return {k: [np.asarray(c, np.int32) for c in zip(*rows)] for k, rows in fam.items()}


def _a_tile_offsets(nblk):
    cnt = _a_tile_counts(nblk)
    return [A_HEADS_PER_GROUP * sum(cnt[:g]) for g in range(len(A_GROUPS))]


def _bias_tiles(rel_bias, nblk):
    out = {}
    for name, cols in _tile_tables(nblk).items():
        n = cols[0].shape[0]
        out[name] = pl.pallas_call(
            _bias_tile_kernel,
            grid_spec=pltpu.PrefetchScalarGridSpec(
                num_scalar_prefetch=len(cols),
                grid=(n,),
                in_specs=[pl.BlockSpec(memory_space=pltpu.SMEM)],
                out_specs=pl.BlockSpec((1, BLK, BLK), lambda t, *_: (t, 0, 0)),
            ),
            out_shape=jax.ShapeDtypeStruct((n, BLK, BLK), F32),
            compiler_params=_params("arbitrary"),
            name="bias_tiles_" + name,
        )(*[jnp.asarray(c) for c in cols], rel_bias)
    return out


def _proj_kernel(x_ref, w_ref, o_ref, xb_ref):
    @pl.when(pl.program_id(1) == 0)
    def _():
        xb_ref[...] = x_ref[...].astype(MXU_DTYPE)

    o_ref[...] = _dot(xb_ref[...], w_ref[...]).astype(o_ref.dtype)


def _proj(x2, w, out_dtype, tm, tn):
    t, d = x2.shape
    n = w.shape[1]
    return pl.pallas_call(
        _proj_kernel,
        grid=(t // tm, n // tn),
        in_specs=[pl.BlockSpec((tm, d), lambda i, j: (i, 0)),
                  pl.BlockSpec((d, tn), lambda i, j: (0, j))],
        out_specs=pl.BlockSpec((tm, tn), lambda i, j: (i, j)),
        out_shape=jax.ShapeDtypeStruct((t, n), out_dtype),
        scratch_shapes=[pltpu.VMEM((tm, d), MXU_DTYPE)],
        compiler_params=_params("arbitrary", "arbitrary"),
        name="in_proj",
    )(x2, w)


def _softmax_step(s, m, l, acc, v, axis):
    m_new = jnp.maximum(m, jnp.max(s, axis=axis, keepdims=True))
    alpha = jnp.exp(m - m_new)
    p = jnp.exp(s - m_new)
    l = alpha * l + jnp.sum(p, axis=axis, keepdims=True)
    if axis == 1:
        acc = alpha * acc + _dot(p.astype(MXU_DTYPE), v)
    else:
        acc = alpha * acc + _dot(v, p.astype(MXU_DTYPE))
    return m_new, l, acc


def _mixer_a_kernel(q_ref, k_ref, v_ref, t_ref, o_ref, *, tile_off, tile_cnt):
    i = pl.program_id(1)
    hpg = A_HEADS_PER_GROUP
    outs = [[None] * len(A_GROUPS) for _ in range(hpg)]
    lses = [[None] * len(A_GROUPS) for _ in range(hpg)]
    for g in range(len(A_GROUPS)):
        nd = tile_cnt[g]
        qs = [q_ref[0, :, (g * hpg + hh) * HEAD_DIM:(g * hpg + hh + 1) * HEAD_DIM]
              for hh in range(hpg)]

        def body(dl, carry, g=g, nd=nd, qs=qs):
            start = pl.multiple_of((i - dl) * BLK, BLK)
            new = []
            for hh in range(hpg):
                m, l, acc = carry[hh]
                c0 = (g * hpg + hh) * HEAD_DIM
                k = k_ref[0, pl.ds(start, BLK), c0:c0 + HEAD_DIM]
                v = v_ref[0, pl.ds(start, BLK), c0:c0 + HEAD_DIM]
                s = _dot_nt(qs[hh], k) * ATTN_SCALE + t_ref[tile_off[g] + hh * nd + dl]
                new.append(_softmax_step(s, m, l, acc, v, axis=1))
            return tuple(new)

        init = tuple((jnp.full((BLK, 1), NEG_INF, F32), jnp.zeros((BLK, 1), F32),
                      jnp.zeros((BLK, HEAD_DIM), F32)) for _ in range(hpg))
        res = lax.fori_loop(0, jnp.minimum(nd, i + 1), body, init)
        for hh in range(hpg):
            m, l, acc = res[hh]
            outs[hh][g] = acc / l
            lses[hh][g] = m + jnp.log(l)
    for hh in range(hpg):
        mx = functools.reduce(jnp.maximum, lses[hh])
        es = [jnp.exp(x - mx) for x in lses[hh]]
        tot = functools.reduce(lambda a, b: a + b, es)
        o = functools.reduce(lambda a, b: a + b,
                             [(e / tot) * og for e, og in zip(es, outs[hh])])
        o_ref[0, :, hh * HEAD_DIM:(hh + 1) * HEAD_DIM] = o.astype(o_ref.dtype)


def _mixer_a(u, tiles_a, nblk):
    b, s, _ = u.shape
    w = A_HEADS * HEAD_DIM
    kern = functools.partial(_mixer_a_kernel, tile_off=tuple(_a_tile_offsets(nblk)),
                             tile_cnt=tuple(_a_tile_counts(nblk)))
    n_a = tiles_a.shape[0]
    return pl.pallas_call(
        kern,
        grid=(b, nblk),
        in_specs=[pl.BlockSpec((1, BLK, w), lambda bi, i: (bi, i, _COL_A_Q // A_HEADS)),
                  pl.BlockSpec((1, s, w), lambda bi, i: (bi, 0, _COL_A_K // A_HEADS)),
                  pl.BlockSpec((1, s, w), lambda bi, i: (bi, 0, _COL_A_V // A_HEADS)),
                  pl.BlockSpec((n_a, BLK, BLK), lambda bi, i: (0, 0, 0))],
        out_specs=pl.BlockSpec((1, BLK, A_HEADS_PER_GROUP * HEAD_DIM), lambda bi, i: (bi, i, 0)),
        out_shape=jax.ShapeDtypeStruct((b, s, A_HEADS_PER_GROUP * HEAD_DIM), MXU_DTYPE),
        compiler_params=_params("arbitrary", "arbitrary"),
        name="mixer_a",
    )(u, u, u, tiles_a)


_C_TK = 256


def _split3(x):
    hi = x.astype(MXU_DTYPE)
    r1 = x - hi.astype(F32)
    mid = r1.astype(MXU_DTYPE)
    lo = (r1 - mid.astype(F32)).astype(MXU_DTYPE)
    return hi, mid, lo


def _fox_decay_kernel(f_ref, bf_ref, c_ref):
    x = f_ref[0] + bf_ref[...]
    logf = jnp.minimum(x, 0.0) - jnp.log1p(jnp.exp(-jnp.abs(x)))
    logf_t = logf.T[0:8]
    nk = c_ref.shape[1]
    r = lax.broadcasted_iota(jnp.int32, (_C_TK, _C_TK), 0)
    c = lax.broadcasted_iota(jnp.int32, (_C_TK, _C_TK), 1)
    upper = (r <= c).astype(MXU_DTYPE)
    carry = jnp.zeros((8, 1), F32)
    for j in range(nk):
        hi, mid, lo = _split3(logf_t[:, j * _C_TK:(j + 1) * _C_TK])
        cs = (_dot(hi, upper) + _dot(mid, upper)) + _dot(lo, upper) + carry
        c_ref[0, j] = cs
        carry = cs[:, _C_TK - 1:_C_TK]


def _fox_decay(u_small, b_f):
    b, s, _ = u_small.shape
    nk = s // _C_TK
    bf_pad = jnp.zeros((1, LANES), F32).at[0, :C_HEADS].set(b_f)
    return pl.pallas_call(
        _fox_decay_kernel,
        grid=(b,),
        in_specs=[pl.BlockSpec((1, s, LANES), lambda bi: (bi, 0, 2)),
                  pl.BlockSpec((1, LANES), lambda bi: (0, 0))],
        out_specs=pl.BlockSpec((1, nk, 8, _C_TK), lambda bi: (bi, 0, 0, 0)),
        out_shape=jax.ShapeDtypeStruct((b, nk, 8, _C_TK), F32),
        compiler_params=_params("arbitrary"),
        name="fox_decay",
    )(u_small, bf_pad)


def _mixer_c_kernel(q_ref, k_ref, v_ref, c_ref, o_ref):
    h = pl.program_id(1)
    i = pl.program_id(2)
    tq = q_ref.shape[1]
    q = q_ref[0]

    def scores(kb):
        start = pl.multiple_of(kb * _C_TK, _C_TK)
        k = k_ref[0, pl.ds(start, _C_TK), :]
        v = v_ref[0, pl.ds(start, _C_TK), :]
        s = _dot_nt(q, k) * ATTN_SCALE - c_ref[0, kb, pl.ds(h, 1), :]
        return s, v

    def body(kb, carry):
        s, v = scores(kb)
        return _softmax_step(s, *carry, v, axis=1)

    init = (jnp.full((tq, 1), NEG_INF, F32), jnp.zeros((tq, 1), F32),
            jnp.zeros((tq, HEAD_DIM), F32))
    carry = lax.fori_loop(0, i, body, init)
    s, v = scores(i)
    r = lax.broadcasted_iota(jnp.int32, (tq, _C_TK), 0)
    c = lax.broadcasted_iota(jnp.int32, (tq, _C_TK), 1)
    s = jnp.where(c <= r, s, NEG_INF)
    m, l, acc = _softmax_step(s, *carry, v, axis=1)
    o_ref[0] = (acc / l).astype(o_ref.dtype)


def _mixer_c(u, cdec):
    b, s, _ = u.shape
    tq = _C_TK
    nk = s // _C_TK
    return pl.pallas_call(
        _mixer_c_kernel,
        grid=(b, C_HEADS, s // tq),
        in_specs=[pl.BlockSpec((1, tq, HEAD_DIM), lambda bi, h, i: (bi, i, _COL_C_Q + h)),
                  pl.BlockSpec((1, s, HEAD_DIM), lambda bi, h, i: (bi, 0, _COL_C_K + h)),
                  pl.BlockSpec((1, s, HEAD_DIM), lambda bi, h, i: (bi, 0, _COL_C_V + h)),
                  pl.BlockSpec((1, nk, 8, _C_TK), lambda bi, h, i: (bi, 0, 0, 0))],
        out_specs=pl.BlockSpec((1, tq, HEAD_DIM), lambda bi, h, i: (bi, i, h)),
        out_shape=jax.ShapeDtypeStruct((b, s, C_HEADS * HEAD_DIM), MXU_DTYPE),
        compiler_params=_params("arbitrary", "arbitrary", "arbitrary"),
        name="mixer_c",
    )(u, u, u, cdec)


def _gelu(x):
    return 0.5 * x * (1.0 + lax.erf(x * np.float32(math.sqrt(0.5))))


def _nsa_compress_kernel(k_ref, v_ref, pe_ref, w1_ref, b1_ref, w2_ref, b2_ref, kc_ref, vct_ref):
    n_chunk = k_ref.shape[1] // CMP_STRIDE
    halves = CMP_LEN // CMP_STRIDE
    res = []
    for c, src in enumerate((k_ref, v_ref)):
        parts = []
        for half in range(halves):
            acc = jnp.zeros((n_chunk, HEAD_DIM), F32)
            for p in range(CMP_STRIDE):
                pos = half * CMP_STRIDE + p
                rows = src[0, pl.ds(p, n_chunk, stride=CMP_STRIDE), :] + pe_ref[c, pos:pos + 1, :]
                acc = acc + _dot(rows.astype(MXU_DTYPE), w1_ref[c, pos])
            parts.append(acc)
        hid = parts[0]
        for half in range(1, halves):
            hid = hid + pltpu.roll(parts[half], n_chunk - half, 0)
        hid = _gelu(hid + b1_ref[c:c + 1, :])
        res.append(_dot(hid.astype(MXU_DTYPE), w2_ref[c]) + b2_ref[c:c + 1, :])
    kc_ref[0] = res[0].astype(kc_ref.dtype)
    vct_ref[0] = res[1].T.astype(vct_ref.dtype)


def _nsa_compress(u_small, cmp_pe, cmp_w1, cmp_b1, cmp_w2, cmp_b2):
    b, s, _ = u_small.shape
    n_chunk = s // CMP_STRIDE
    w1 = cmp_w1.reshape(2, CMP_LEN, HEAD_DIM, HEAD_DIM)
    full = lambda shape: pl.BlockSpec(shape, lambda bi: (0,) * len(shape))
    return pl.pallas_call(
        _nsa_compress_kernel,
        grid=(b,),
        in_specs=[pl.BlockSpec((1, s, LANES), lambda bi: (bi, 0, 0)),
                  pl.BlockSpec((1, s, LANES), lambda bi: (bi, 0, 1)),
                  full(cmp_pe.shape), full(w1.shape), full(cmp_b1.shape),
                  full(cmp_w2.shape), full(cmp_b2.shape)],
        out_specs=[pl.BlockSpec((1, n_chunk, HEAD_DIM), lambda bi: (bi, 0, 0)),
                   pl.BlockSpec((1, HEAD_DIM, n_chunk), lambda bi: (bi, 0, 0))],
        out_shape=[jax.ShapeDtypeStruct((b, n_chunk, HEAD_DIM), MXU_DTYPE),
                   jax.ShapeDtypeStruct((b, HEAD_DIM, n_chunk), MXU_DTYPE)],
        compiler_params=_params("arbitrary"),
        name="nsa_compress",
    )(u_small, u_small, cmp_pe, w1, cmp_b1, cmp_w2, cmp_b2)


def _lane_cat(xs):
    return jnp.concatenate(xs, axis=1)


def _mixer_b_kernel(q_ref, ks_ref, vs_ref, kw_ref, vw_ref, kc_ref, vct_ref, gl_ref, bg_ref,
                    tslc_ref, twin_ref, tcmp_ref, o_ref, vst_ref, vwt_ref, add_ref):
    i = pl.program_id(1)
    nblk = vst_ref.shape[0]
    nh = B_HEADS
    n_slc = add_ref.shape[0]

    @pl.when(i == 0)
    def _():
        for kb in range(nblk):
            vst_ref[kb] = vs_ref[0, kb * BLK:(kb + 1) * BLK, :].astype(F32).T.astype(MXU_DTYPE)
            vwt_ref[kb] = vw_ref[0, kb * BLK:(kb + 1) * BLK, :].astype(F32).T.astype(MXU_DTYPE)

    q4 = jnp.concatenate([q_ref[0, :, h * HEAD_DIM:(h + 1) * HEAD_DIM] for h in range(nh)], axis=0)

    st = _dot_nt(kc_ref[0], q4) * ATTN_SCALE + _lane_cat([tcmp_ref[h] for h in range(nh)])
    m = jnp.max(st, axis=0, keepdims=True)
    e = jnp.exp(st - m)
    l = jnp.sum(e, axis=0, keepdims=True)
    pc = jnp.where(m > 0.5 * NEG_INF, e / l, 0.0).astype(MXU_DTYPE)
    o_cmp = _dot(vct_ref[0], pc)
    n_cmp_pad = kc_ref.shape[1]
    jj = lax.broadcasted_iota(jnp.int32, (n_slc, n_cmp_pad), 0)
    nn = lax.broadcasted_iota(jnp.int32, (n_slc, n_cmp_pad), 1)
    overlap = ((nn * CMP_STRIDE <= jj * SLC_LEN + SLC_LEN - 1)
               & (nn * CMP_STRIDE + CMP_LEN - 1 >= jj * SLC_LEN)).astype(MXU_DTYPE)
    imp4 = _dot(overlap, pc)
    imp = functools.reduce(lambda a, b: a + b,
                           [imp4[:, h * BLK:(h + 1) * BLK] for h in range(nh)])
    t = i * BLK + lax.broadcasted_iota(jnp.int32, (n_slc, BLK), 1)
    blk = lax.broadcasted_iota(jnp.int32, (n_slc, BLK), 0)
    cur = t // SLC_LEN
    forced = (blk == 0) | (blk == cur) | (blk == cur - 1)
    causal = blk * SLC_LEN <= t
    score = jnp.where(forced, FORCE_SCORE, jnp.where(causal, imp, NEG_INF))
    rank = jnp.zeros((n_slc, BLK), jnp.int32)
    for r in range(n_slc):
        row = score[r:r + 1, :]
        ahead = (row > score) | ((row == score) & (r < blk))
        rank = rank + ahead.astype(jnp.int32)
    add = jnp.where(rank < min(N_SELECT, n_slc), 0.0, NEG_INF).astype(F32)
    add_ref[...] = _lane_cat([add] * nh)

    init = (jnp.full((1, nh * BLK), NEG_INF, F32), jnp.zeros((1, nh * BLK), F32),
            jnp.zeros((HEAD_DIM, nh * BLK), F32))
    per_blk = BLK // SLC_LEN

    def slc_body(dl, carry):
        kb = i - dl
        start = pl.multiple_of(kb * BLK, BLK)
        k = ks_ref[0, pl.ds(start, BLK), :]
        bias = _lane_cat([tslc_ref[dl * nh + h] for h in range(nh)])
        sel = jnp.concatenate(
            [jnp.broadcast_to(add_ref[pl.ds(kb * per_blk + j, 1), :], (SLC_LEN, nh * BLK))
             for j in range(per_blk)], axis=0)
        s = _dot_nt(k, q4) * ATTN_SCALE + bias + sel
        return _softmax_step(s, *carry, vst_ref[kb], axis=0)

    ms, ls, accs = lax.fori_loop(0, i + 1, slc_body, init)

    def win_body(dl, carry):
        kb = i - dl
        start = pl.multiple_of(kb * BLK, BLK)
        k = kw_ref[0, pl.ds(start, BLK), :]
        bias = _lane_cat([twin_ref[dl * nh + h] for h in range(nh)])
        s = _dot_nt(k, q4) * ATTN_SCALE + bias
        return _softmax_step(s, *carry, vwt_ref[kb], axis=0)

    n_win = twin_ref.shape[0] // nh
    mw, lw, accw = lax.fori_loop(0, jnp.minimum(n_win, i + 1), win_body, init)

    gt = jax.nn.sigmoid(gl_ref[0] + bg_ref[...]).T

    def gate(br):
        r0 = _GATE_LANE0 + br * nh
        return _lane_cat([gt[r0 + h:r0 + h + 1, :] for h in range(nh)])

    o_t = gate(0) * o_cmp + gate(1) * (accs / ls) + gate(2) * (accw / lw)
    for h in range(nh):
        o_ref[0, :, h * HEAD_DIM:(h + 1) * HEAD_DIM] = (
            o_t[:, h * BLK:(h + 1) * BLK].T.astype(o_ref.dtype))


def _mixer_b(u, u_small, kc, vct, b_nsa_gate, tiles, nblk):
    b, s, _ = u.shape
    nh = B_HEADS
    bg_pad = jnp.zeros((1, LANES), F32).at[0, _GATE_LANE0:_GATE_LANE0 + 3 * nh].set(b_nsa_gate)
    whole = lambda a: pl.BlockSpec(a.shape, lambda bi, i: (0, 0, 0))
    kv = lambda col: pl.BlockSpec((1, s, HEAD_DIM), lambda bi, i: (bi, 0, col))
    n_chunk = kc.shape[1]
    return pl.pallas_call(
        _mixer_b_kernel,
        grid=(b, nblk),
        in_specs=[pl.BlockSpec((1, BLK, nh * HEAD_DIM), lambda bi, i: (bi, i, _COL_B_Q // nh)),
                  kv(_COL_B_KS), kv(_COL_B_VS), kv(_COL_B_KW), kv(_COL_B_VW),
                  pl.BlockSpec((1, n_chunk, HEAD_DIM), lambda bi, i: (bi, 0, 0)),
                  pl.BlockSpec((1, HEAD_DIM, n_chunk), lambda bi, i: (bi, 0, 0)),
                  pl.BlockSpec((1, BLK, LANES), lambda bi, i: (bi, i, 2)),
                  pl.BlockSpec((1, LANES), lambda bi, i: (0, 0)),
                  whole(tiles["slc"]), whole(tiles["win"]),
                  pl.BlockSpec((nh, BLK, BLK), lambda bi, i: (i, 0, 0))],
        out_specs=pl.BlockSpec((1, BLK, nh * HEAD_DIM), lambda bi, i: (bi, i, 0)),
        out_shape=jax.ShapeDtypeStruct((b, s, nh * HEAD_DIM), MXU_DTYPE),
        scratch_shapes=[pltpu.VMEM((nblk, HEAD_DIM, BLK), MXU_DTYPE),
                        pltpu.VMEM((nblk, HEAD_DIM, BLK), MXU_DTYPE),
                        pltpu.VMEM((s // SLC_LEN, nh * BLK), F32)],
        compiler_params=_params("arbitrary", "arbitrary"),
        name="mixer_b",
    )(u, u, u, u, u, kc, vct, u_small, bg_pad, tiles["slc"], tiles["win"], tiles["cmp"])


def _layer_norm(y, g, b):
    mu = jnp.mean(y, axis=-1, keepdims=True)
    yc = y - mu
    var = jnp.mean(yc * yc, axis=-1, keepdims=True)
    return yc * lax.rsqrt(var + LN_EPS) * g + b


def _merge_kernel(x_ref, oa_ref, ob_ref, oc_ref, wga_ref, wgb_ref, wgc_ref, bga_ref, bgb_ref,
                  bgc_ref, wpa_ref, wpb_ref, wpc_ref, wo_ref, g_ref, b_ref, y_ref, yb_ref,
                  xb_ref, acc_ref, *, alpha):
    j = pl.program_id(1)

    @pl.when(j == 0)
    def _():
        xb_ref[...] = x_ref[...].astype(MXU_DTYPE)
        acc_ref[...] = jnp.zeros_like(acc_ref)

    xb = xb_ref[...]
    mixed = None
    for o_ref, wg_ref, bg_ref, wp_ref in ((oa_ref, wga_ref, bga_ref, wpa_ref),
                                          (ob_ref, wgb_ref, bgb_ref, wpb_ref),
                                          (oc_ref, wgc_ref, bgc_ref, wpc_ref)):
        gate = jax.nn.sigmoid(_dot(xb, wg_ref[...]) + bg_ref[...])
        term = gate * _dot(o_ref[...], wp_ref[...])
        mixed = term if mixed is None else mixed + term
    acc_ref[...] += _dot(mixed.astype(MXU_DTYPE), wo_ref[...])

    @pl.when(j == pl.num_programs(1) - 1)
    def _():
        y = _layer_norm(alpha * x_ref[...] + acc_ref[...], g_ref[...], b_ref[...])
        y_ref[...] = y
        yb_ref[...] = y.astype(MXU_DTYPE)


def _merge_ln(x2, oa, ob, oc, w_gate, b_gate, w_pa, w_pb, w_pc, w_out, ln_g, ln_b, alpha,
              tm, tn):
    t, d = x2.shape
    nj = d // tn
    row = lambda k: pl.BlockSpec((tm, k), lambda i, j: (i, 0))
    colw = lambda k, shift: pl.BlockSpec((k, tn), lambda i, j: (0, shift * nj + j))
    vec = lambda shift: pl.BlockSpec((1, tn), lambda i, j: (0, shift * nj + j))
    fullvec = pl.BlockSpec((1, d), lambda i, j: (0, 0))
    return pl.pallas_call(
        functools.partial(_merge_kernel, alpha=alpha),
        grid=(t // tm, nj),
        in_specs=[row(d), row(oa.shape[1]), row(ob.shape[1]), row(oc.shape[1]),
                  colw(d, 0), colw(d, 1), colw(d, 2), vec(0), vec(1), vec(2),
                  colw(w_pa.shape[0], 0), colw(w_pb.shape[0], 0), colw(w_pc.shape[0], 0),
                  pl.BlockSpec((tn, d), lambda i, j: (j, 0)), fullvec, fullvec],
        out_specs=[pl.BlockSpec((tm, d), lambda i, j: (i, 0)),
                   pl.BlockSpec((tm, d), lambda i, j: (i, 0))],
        out_shape=[jax.ShapeDtypeStruct((t, d), F32), jax.ShapeDtypeStruct((t, d), MXU_DTYPE)],
        scratch_shapes=[pltpu.VMEM((tm, d), MXU_DTYPE), pltpu.VMEM((tm, d), F32)],
        compiler_params=_params("arbitrary", "arbitrary"),
        name="merge_ln",
    )(x2, oa, ob, oc, w_gate, w_gate, w_gate, b_gate, b_gate, b_gate, w_pa, w_pb, w_pc, w_out,
      ln_g, ln_b)


_HALO = BF16_SUBLANES


def _ffn_kernel(x_ref, xb_ref, halo_ref, wa_ref, wb_ref, cwa_ref, cwb_ref, cba_ref, cbb_ref,
                wd_ref, g_ref, b_ref, y_ref, xe_ref, ha_ref, hb_ref, acc_ref, *, alpha,
                tiles_per_seq):
    i = pl.program_id(0)
    j = pl.program_id(1)
    tm = x_ref.shape[0]

    @pl.when(j == 0)
    def _():
        keep = (i % tiles_per_seq != 0).astype(MXU_DTYPE)
        xe_ref[0:_HALO, :] = halo_ref[...] * keep
        xe_ref[_HALO:, :] = xb_ref[...]
        acc_ref[...] = jnp.zeros_like(acc_ref)

    xe = xe_ref[...]
    ha_ref[...] = _dot(xe, wa_ref[...])
    hb_ref[...] = _dot(xe, wb_ref[...])

    def conv(h_ref, cw_ref, cb_ref):
        out = cb_ref[...]
        for tap in range(CONV_W):
            shift = CONV_W - 1 - tap
            out = out + cw_ref[tap:tap + 1, :] * h_ref[pl.ds(_HALO - shift, tm), :]
        return out

    a = conv(ha_ref, cwa_ref, cba_ref)
    bb = conv(hb_ref, cwb_ref, cbb_ref)
    acc_ref[...] += _dot((_gelu(a) * bb).astype(MXU_DTYPE), wd_ref[...])

    @pl.when(j == pl.num_programs(1) - 1)
    def _():
        y_ref[...] = _layer_norm(alpha * x_ref[...] + acc_ref[...], g_ref[...], b_ref[...])


def _ffn_ln(x2, xb2, w_up, conv_w, conv_b, w_down, ln_g, ln_b, alpha, seq, tm, tf):
    t, d = x2.shape
    f = w_down.shape[0]
    nj = f // tf
    hb = tm // _HALO
    fullvec = pl.BlockSpec((1, d), lambda i, j: (0, 0))
    return pl.pallas_call(
        functools.partial(_ffn_kernel, alpha=alpha, tiles_per_seq=seq // tm),
        grid=(t // tm, nj),
        in_specs=[pl.BlockSpec((tm, d), lambda i, j: (i, 0)),
                  pl.BlockSpec((tm, d), lambda i, j: (i, 0)),
                  pl.BlockSpec((_HALO, d), lambda i, j: (jnp.maximum(i * hb - 1, 0), 0)),
                  pl.BlockSpec((d, tf), lambda i, j: (0, j)),
                  pl.BlockSpec((d, tf), lambda i, j: (0, nj + j)),
                  pl.BlockSpec((CONV_W, tf), lambda i, j: (0, j)),
                  pl.BlockSpec((CONV_W, tf), lambda i, j: (0, nj + j)),
                  pl.BlockSpec((1, tf), lambda i, j: (0, j)),
                  pl.BlockSpec((1, tf), lambda i, j: (0, nj + j)),
                  pl.BlockSpec((tf, d), lambda i, j: (j, 0)), fullvec, fullvec],
        out_specs=pl.BlockSpec((tm, d), lambda i, j: (i, 0)),
        out_shape=jax.ShapeDtypeStruct((t, d), F32),
        scratch_shapes=[pltpu.VMEM((tm + _HALO, d), MXU_DTYPE),
                        pltpu.VMEM((tm + _HALO, tf), F32),
                        pltpu.VMEM((tm + _HALO, tf), F32),
                        pltpu.VMEM((tm, d), F32)],
        compiler_params=_params("arbitrary", "arbitrary"),
        name="ffn_ln",
    )(x2, xb2, xb2, w_up, w_up, conv_w, conv_w, conv_b, conv_b, w_down, ln_g, ln_b)


def _in_split_offsets():
    widths = (('a_q', A_HEADS * HEAD_DIM), ('a_k', A_HEADS * HEAD_DIM), ('a_v', A_HEADS * HEAD_DIM),
              ('b_q', B_HEADS * HEAD_DIM), ('b_k_cmp', HEAD_DIM), ('b_v_cmp', HEAD_DIM),
              ('b_k_slc', HEAD_DIM), ('b_v_slc', HEAD_DIM), ('b_k_win', HEAD_DIM),
              ('b_v_win', HEAD_DIM), ('b_gate', 3 * B_HEADS),
              ('c_q', C_HEADS * HEAD_DIM), ('c_k', C_HEADS * HEAD_DIM), ('c_v', C_HEADS * HEAD_DIM),
              ('c_f', C_HEADS))
    out, o = {}, 0
    for name, w in widths:
        out[name] = (o, o + w)
        o += w
    return out


def _pack_w_in(w_in_l):
    sl = _in_split_offsets()
    take = lambda name: w_in_l[:, sl[name][0]:sl[name][1]]
    main = jnp.concatenate([take(n) for n in ('a_q', 'a_k', 'a_v', 'c_q', 'c_k', 'c_v', 'b_q',
                                              'b_k_slc', 'b_v_slc', 'b_k_win', 'b_v_win')], axis=1)
    d = w_in_l.shape[0]
    misc = jnp.zeros((d, LANES), w_in_l.dtype)
    misc = misc.at[:, :C_HEADS].set(take('c_f'))
    misc = misc.at[:, _GATE_LANE0:_GATE_LANE0 + 3 * B_HEADS].set(take('b_gate'))
    small = jnp.concatenate([take('b_k_cmp'), take('b_v_cmp'), misc], axis=1)
    return main.astype(MXU_DTYPE), small.astype(MXU_DTYPE)


def kernel(x, rel_bias, w_in, b_f, b_nsa_gate, cmp_pe, cmp_w1, cmp_b1, cmp_w2, cmp_b2, w_gate, b_gate, w_pa, w_pb, w_pc, w_out, ln1_g, ln1_b, w_up, conv_w, conv_b, w_down, ln2_g, ln2_b):
    bsz, seq, d = x.shape
    depth = w_in.shape[0]
    nblk = seq // BLK
    alpha = (2 * depth) ** 0.25
    t = bsz * seq
    tm = 512
    bf = lambda a: a.astype(MXU_DTYPE)

    tiles = _bias_tiles(rel_bias, nblk)
    x2 = x.reshape(t, d)
    for l in range(depth):
        w_main, w_small = _pack_w_in(w_in[l])
        u = _proj(x2, w_main, MXU_DTYPE, 1024, 512).reshape(bsz, seq, -1)
        u_small = _proj(x2, w_small, F32, 1024, w_small.shape[1]).reshape(bsz, seq, -1)

        o_a = _mixer_a(u, tiles["a"], nblk)
        kc, vct = _nsa_compress(u_small, cmp_pe[l], bf(cmp_w1[l]), cmp_b1[l], bf(cmp_w2[l]),
                                cmp_b2[l])
        o_b = _mixer_b(u, u_small, kc, vct, b_nsa_gate[l], tiles, nblk)
        cdec = _fox_decay(u_small, b_f[l])
        o_c = _mixer_c(u, cdec)

        x1, x1b = _merge_ln(x2, o_a.reshape(t, -1), o_b.reshape(t, -1), o_c.reshape(t, -1),
                            bf(w_gate[l]), b_gate[l][None], bf(w_pa[l]), bf(w_pb[l]), bf(w_pc[l]),
                            bf(w_out[l]), ln1_g[l][None], ln1_b[l][None], alpha, tm, 256)
        x2 = _ffn_ln(x1, x1b, bf(w_up[l]), conv_w[l], conv_b[l][None], bf(w_down[l]),
                     ln2_g[l][None], ln2_b[l][None], alpha, seq, tm, 512)
    return x2.reshape(bsz, seq, d)
```

```python
import functools
import math

import numpy as np
import jax
import jax.numpy as jnp
from jax import lax
from jax.experimental import pallas as pl
from jax.experimental.pallas import tpu as pltpu

HEAD_DIM = 128
A_GROUPS = ((128, 1), (512, 4), (2048, 16))
A_HEADS_PER_GROUP = 2
A_HEADS = A_HEADS_PER_GROUP * len(A_GROUPS)
B_HEADS = 4
CMP_LEN = 32
CMP_STRIDE = 16
SLC_LEN = 64
N_SELECT = 16
WIN = 512
C_HEADS = 6
N_BUCKETS = 32
MAX_DISTANCE = 2048
CONV_W = 3
LN_EPS = 1e-5
NEG_INF = -1e30
FORCE_SCORE = 1e9
ATTN_SCALE = HEAD_DIM ** -0.5

LANES = 128
VMEM_LIMIT_BYTES = 56 * 2 ** 20
BF16_SUBLANES = 16

MXU_DTYPE = jnp.bfloat16
F32 = jnp.float32
BLK = LANES

_COL_A_Q, _COL_A_K, _COL_A_V = 0, 6, 12
_COL_C_Q, _COL_C_K, _COL_C_V = 18, 24, 30
_COL_B_Q = 36
_COL_B_KS, _COL_B_VS, _COL_B_KW, _COL_B_VW = 40, 41, 42, 43
_N_COL_BLOCKS = 44


def _in_split_offsets():
    widths = (('a_q', A_HEADS * HEAD_DIM), ('a_k', A_HEADS * HEAD_DIM), ('a_v', A_HEADS * HEAD_DIM),
              ('b_q', B_HEADS * HEAD_DIM), ('b_k_cmp', HEAD_DIM), ('b_v_cmp', HEAD_DIM),
              ('b_k_slc', HEAD_DIM), ('b_v_slc', HEAD_DIM), ('b_k_win', HEAD_DIM),
              ('b_v_win', HEAD_DIM), ('b_gate', 3 * B_HEADS),
              ('c_q', C_HEADS * HEAD_DIM), ('c_k', C_HEADS * HEAD_DIM), ('c_v', C_HEADS * HEAD_DIM),
              ('c_f', C_HEADS))
    out, o = {}, 0
    for name, w in widths:
        out[name] = (o, o + w)
        o += w
    return out, o


_SPLIT, N_IN = _in_split_offsets()
_SMALL_KC, _SMALL_VC, _SMALL_GATE, _SMALL_FORGET = 0, 1, 2, 3
_SMALL_BLOCKS = 4
_GATE_COL0 = _SPLIT['b_gate'][0] // LANES * LANES
_GATE_LANE0 = _SPLIT['b_gate'][0] - _GATE_COL0
_FORGET_COL0 = _SPLIT['c_f'][0] // LANES * LANES
_FORGET_LANE0 = _SPLIT['c_f'][0] - _FORGET_COL0
assert _SPLIT['b_gate'][1] - _GATE_COL0 <= LANES and N_IN - _FORGET_COL0 <= LANES
_FORGET_ROW0 = _FORGET_LANE0 // 8 * 8
_FORGET_ROWS = -(-(_FORGET_LANE0 + C_HEADS - _FORGET_ROW0) // 8) * 8


def _params(*sem):
    return pltpu.CompilerParams(dimension_semantics=sem, vmem_limit_bytes=VMEM_LIMIT_BYTES)


def _dot(a, b):
    return jnp.dot(a, b, preferred_element_type=F32)


def _dot_nt(a, b):
    return lax.dot_general(a, b, (((1,), (1,)), ((), ())), preferred_element_type=F32)


def _bias_tile_kernel(base_ref, rmul_ref, cmul_ref, dmask_ref, dmax_ref, rmax_ref, head_ref,
                      rel_ref, out_ref):
    t = pl.program_id(0)
    row = lax.broadcasted_iota(jnp.int32, (BLK, BLK), 0)
    col = lax.broadcasted_iota(jnp.int32, (BLK, BLK), 1)
    dist = base_ref[t] + rmul_ref[t] * row + cmul_ref[t] * col
    valid = ((dist >= 0) & (dist <= dmax_ref[t]) & ((dist & dmask_ref[t]) == 0)
             & (row < rmax_ref[t]))
    max_exact = N_BUCKETS // 2
    d = jnp.maximum(dist, 0)
    log_ratio = (jnp.log(jnp.maximum(d, 1).astype(F32) / max_exact)
                 / math.log(MAX_DISTANCE / max_exact))
    large = jnp.minimum(max_exact + (log_ratio * (N_BUCKETS - max_exact)).astype(jnp.int32),
                        N_BUCKETS - 1)
    bucket = jnp.where(d < max_exact, d, large)
    h = head_ref[t]
    val = jnp.zeros((BLK, BLK), F32)
    for b in range(N_BUCKETS):
        val = jnp.where(bucket == b, rel_ref[b, h], val)
    out_ref[0] = jnp.where(valid, val, NEG_INF)


def _a_tile_counts(nblk):
    return [min(window // BLK + 1, nblk) for window, _ in A_GROUPS]


def _tile_tables(nblk):
    big = 2 ** 30
    fam = {}
    fam["a"] = [(dl * BLK, 1, -1, dil - 1, window if dl < cnt else -1, BLK,
                 g * A_HEADS_PER_GROUP + hh)
                for g, ((window, dil), cnt) in enumerate(zip(A_GROUPS, _a_tile_counts(nblk)))
                for hh in range(A_HEADS_PER_GROUP)
                for dl in range(cnt + 1)]
    fam["slc"] = [(dl * BLK, -1, 1, 0, big if dl < nblk else -1, BLK, A_HEADS + h)
                  for dl in range(nblk + 1) for h in range(B_HEADS)]
    n_win = WIN // BLK + 1
    fam["win"] = [(dl * BLK, -1, 1, 0, WIN - 1 if dl < n_win else -1, BLK, A_HEADS + h)
                  for dl in range(n_win + 1) for h in range(B_HEADS)]
    n_cmp = (nblk * BLK - CMP_LEN) // CMP_STRIDE + 1
    fam["cmp"] = [(i * BLK - (CMP_LEN - 1), -CMP_STRIDE, 1, 0, big, n_cmp, A_HEADS + h)
                  for i in range(nblk) for h in range(B_HEADS)]
    return {k: [np.asarray(c, np.int32) for c in zip(*rows)] for k, rows in fam.items()}


def _a_tile_offsets(nblk):
    cnt = _a_tile_counts(nblk)
    return [A_HEADS_PER_GROUP * sum(c + 1 for c in cnt[:g]) for g in range(len(A_GROUPS))]


def _bias_tiles(rel_bias, nblk):
    out = {}
    for name, cols in _tile_tables(nblk).items():
        n = cols[0].shape[0]
        out[name] = pl.pallas_call(
            _bias_tile_kernel,
            grid_spec=pltpu.PrefetchScalarGridSpec(
                num_scalar_prefetch=len(cols),
                grid=(n,),
                in_specs=[pl.BlockSpec(memory_space=pltpu.SMEM)],
                out_specs=pl.BlockSpec((1, BLK, BLK), lambda t, *_: (t, 0, 0)),
            ),
            out_shape=jax.ShapeDtypeStruct((n, BLK, BLK), F32),
            compiler_params=_params("arbitrary"),
            name="bias_tiles_" + name,
        )(*[jnp.asarray(c) for c in cols], rel_bias)
    return out


def _proj_kernel(x_ref, w_ref, o_ref, xb_ref):
    @pl.when(pl.program_id(1) == 0)
    def _():
        xb_ref[...] = x_ref[...].astype(MXU_DTYPE)

    o_ref[...] = _dot(xb_ref[...], w_ref[...]).astype(o_ref.dtype)


def _proj(x2, w, out_dtype, tm, tn):
    t, d = x2.shape
    n = w.shape[1]
    return pl.pallas_call(
        _proj_kernel,
        grid=(t // tm, n // tn),
        in_specs=[pl.BlockSpec((tm, d), lambda i, j: (i, 0)),
                  pl.BlockSpec((d, tn), lambda i, j: (0, j))],
        out_specs=pl.BlockSpec((tm, tn), lambda i, j: (i, j)),
        out_shape=jax.ShapeDtypeStruct((t, n), out_dtype),
        scratch_shapes=[pltpu.VMEM((tm, d), MXU_DTYPE)],
        compiler_params=_params("arbitrary", "arbitrary"),
        name="in_proj",
    )(x2, w)


def _lane_cat(xs):
    return jnp.concatenate(xs, axis=1)


def _softmax_pv(s, v, axis):
    m = jnp.max(s, axis=axis, keepdims=True)
    p = jnp.exp(s - m)
    l = jnp.sum(p, axis=axis, keepdims=True)
    if axis == 1:
        o = _dot(p.astype(MXU_DTYPE), v)
    else:
        o = _dot(v, p.astype(MXU_DTYPE))
    return o / l, m, l


def _mixer_a_kernel(q_ref, k_ref, v_ref, t_ref, o_ref, *, tile_off, tile_cnt):
    i = pl.program_id(1)
    hpg = A_HEADS_PER_GROUP
    outs = [[None] * len(A_GROUPS) for _ in range(hpg)]
    lses = [[None] * len(A_GROUPS) for _ in range(hpg)]
    for g in range(len(A_GROUPS)):
        nd = tile_cnt[g]
        kb0 = jnp.maximum(i - (nd - 1), 0)
        start = pl.multiple_of(kb0 * BLK, BLK)
        for hh in range(hpg):
            c0 = (g * hpg + hh) * HEAD_DIM
            q = q_ref[0, :, c0:c0 + HEAD_DIM]
            k = k_ref[0, pl.ds(start, nd * BLK), c0:c0 + HEAD_DIM]
            v = v_ref[0, pl.ds(start, nd * BLK), c0:c0 + HEAD_DIM]
            tbase = tile_off[g] + hh * (nd + 1)
            deltas = [i - kb0 - j for j in range(nd)]
            bias = _lane_cat([t_ref[tbase + jnp.where(dl >= 0, dl, nd)] for dl in deltas])
            s = _dot_nt(q, k) * ATTN_SCALE + bias
            o, m, l = _softmax_pv(s, v, axis=1)
            outs[hh][g] = o
            lses[hh][g] = m + jnp.log(l)
    for hh in range(hpg):
        mx = functools.reduce(jnp.maximum, lses[hh])
        es = [jnp.exp(x - mx) for x in lses[hh]]
        tot = functools.reduce(lambda a, b: a + b, es)
        o = functools.reduce(lambda a, b: a + b,
                             [(e / tot) * og for e, og in zip(es, outs[hh])])
        o_ref[0, :, hh * HEAD_DIM:(hh + 1) * HEAD_DIM] = o.astype(o_ref.dtype)


def _mixer_a(u, tiles_a, nblk):
    b, s, _ = u.shape
    w = A_HEADS * HEAD_DIM
    kern = functools.partial(_mixer_a_kernel, tile_off=tuple(_a_tile_offsets(nblk)),
                             tile_cnt=tuple(_a_tile_counts(nblk)))
    n_a = tiles_a.shape[0]
    return pl.pallas_call(
        kern,
        grid=(b, nblk),
        in_specs=[pl.BlockSpec((1, BLK, w), lambda bi, i: (bi, i, _COL_A_Q // A_HEADS)),
                  pl.BlockSpec((1, s, w), lambda bi, i: (bi, 0, _COL_A_K // A_HEADS)),
                  pl.BlockSpec((1, s, w), lambda bi, i: (bi, 0, _COL_A_V // A_HEADS)),
                  pl.BlockSpec((n_a, BLK, BLK), lambda bi, i: (0, 0, 0))],
        out_specs=pl.BlockSpec((1, BLK, A_HEADS_PER_GROUP * HEAD_DIM), lambda bi, i: (bi, i, 0)),
        out_shape=jax.ShapeDtypeStruct((b, s, A_HEADS_PER_GROUP * HEAD_DIM), MXU_DTYPE),
        compiler_params=_params("arbitrary", "arbitrary"),
        name="mixer_a",
    )(u, u, u, tiles_a)


_C_TK = 256


def _split3(x):
    hi = x.astype(MXU_DTYPE)
    r1 = x - hi.astype(F32)
    mid = r1.astype(MXU_DTYPE)
    lo = (r1 - mid.astype(F32)).astype(MXU_DTYPE)
    return hi, mid, lo


def _fox_decay_kernel(f_ref, bf_ref, c_ref):
    x = f_ref[0] + bf_ref[...]
    logf = jnp.minimum(x, 0.0) - jnp.log1p(jnp.exp(-jnp.abs(x)))
    logf_t = logf.T[_FORGET_ROW0:_FORGET_ROW0 + _FORGET_ROWS]
    nk = c_ref.shape[1]
    r = lax.broadcasted_iota(jnp.int32, (_C_TK, _C_TK), 0)
    c = lax.broadcasted_iota(jnp.int32, (_C_TK, _C_TK), 1)
    upper = (r <= c).astype(MXU_DTYPE)
    carry = jnp.zeros((_FORGET_ROWS, 1), F32)
    for j in range(nk):
        hi, mid, lo = _split3(logf_t[:, j * _C_TK:(j + 1) * _C_TK])
        cs = (_dot(hi, upper) + _dot(mid, upper)) + _dot(lo, upper) + carry
        c_ref[0, j] = cs
        carry = cs[:, _C_TK - 1:_C_TK]


def _fox_decay(u_small, b_f):
    b, s, _ = u_small.shape
    nk = s // _C_TK
    bf_pad = jnp.zeros((1, LANES), F32).at[0, _FORGET_LANE0:_FORGET_LANE0 + C_HEADS].set(b_f)
    return pl.pallas_call(
        _fox_decay_kernel,
        grid=(b,),
        in_specs=[pl.BlockSpec((1, s, LANES), lambda bi: (bi, 0, _SMALL_FORGET)),
                  pl.BlockSpec((1, LANES), lambda bi: (0, 0))],
        out_specs=pl.BlockSpec((1, nk, _FORGET_ROWS, _C_TK), lambda bi: (bi, 0, 0, 0)),
        out_shape=jax.ShapeDtypeStruct((b, nk, _FORGET_ROWS, _C_TK), F32),
        compiler_params=_params("arbitrary"),
        name="fox_decay",
    )(u_small, bf_pad)


_C_TQ = 512


def _mixer_c_kernel(q_ref, k_ref, v_ref, c_ref, o_ref):
    row = _FORGET_LANE0 - _FORGET_ROW0 + pl.program_id(1)
    tq = _C_TQ
    r = lax.broadcasted_iota(jnp.int32, (tq, tq), 0)
    c = lax.broadcasted_iota(jnp.int32, (tq, tq), 1)
    for i in range(q_ref.shape[1] // tq):
        nk = (i + 1) * tq
        q = q_ref[0, i * tq:(i + 1) * tq, :]
        decay = _lane_cat([c_ref[0, j, pl.ds(row, 1), :] for j in range(nk // _C_TK)])
        s = _dot_nt(q, k_ref[0, :nk, :]) * ATTN_SCALE - decay
        diag = jnp.where(c <= r, s[:, nk - tq:], NEG_INF)
        s = diag if i == 0 else _lane_cat([s[:, :nk - tq], diag])
        o, _, _ = _softmax_pv(s, v_ref[0, :nk, :], axis=1)
        o_ref[0, i * tq:(i + 1) * tq, :] = o.astype(o_ref.dtype)


def _mixer_c(u, cdec):
    b, s, _ = u.shape
    nk = s // _C_TK
    head = lambda col: pl.BlockSpec((1, s, HEAD_DIM), lambda bi, h: (bi, 0, col + h))
    return pl.pallas_call(
        _mixer_c_kernel,
        grid=(b, C_HEADS),
        in_specs=[head(_COL_C_Q), head(_COL_C_K), head(_COL_C_V),
                  pl.BlockSpec((1, nk, _FORGET_ROWS, _C_TK), lambda bi, h: (bi, 0, 0, 0))],
        out_specs=head(0),
        out_shape=jax.ShapeDtypeStruct((b, s, C_HEADS * HEAD_DIM), MXU_DTYPE),
        compiler_params=_params("arbitrary", "arbitrary"),
        name="mixer_c",
    )(u, u, u, cdec)


def _gelu(x):
    return 0.5 * x * (1.0 + lax.erf(x * np.float32(math.sqrt(0.5))))


def _nsa_compress_kernel(k_ref, v_ref, pe_ref, w1_ref, b1_ref, w2_ref, b2_ref, kc_ref, vct_ref):
    n_chunk = k_ref.shape[1] // CMP_STRIDE
    halves = CMP_LEN // CMP_STRIDE
    res = []
    for c, src in enumerate((k_ref, v_ref)):
        parts = []
        for half in range(halves):
            acc = jnp.zeros((n_chunk, HEAD_DIM), F32)
            for p in range(CMP_STRIDE):
                pos = half * CMP_STRIDE + p
                rows = src[0, pl.ds(p, n_chunk, stride=CMP_STRIDE), :] + pe_ref[c, pos:pos + 1, :]
                acc = acc + _dot(rows.astype(MXU_DTYPE), w1_ref[c, pos])
            parts.append(acc)
        hid = parts[0]
        for half in range(1, halves):
            hid = hid + pltpu.roll(parts[half], n_chunk - half, 0)
        hid = _gelu(hid + b1_ref[c:c + 1, :])
        res.append(_dot(hid.astype(MXU_DTYPE), w2_ref[c]) + b2_ref[c:c + 1, :])
    kc_ref[0] = res[0].astype(kc_ref.dtype)
    vct_ref[0] = res[1].T.astype(vct_ref.dtype)


def _nsa_compress(u_small, cmp_pe, cmp_w1, cmp_b1, cmp_w2, cmp_b2):
    b, s, _ = u_small.shape
    n_chunk = s // CMP_STRIDE
    w1 = cmp_w1.reshape(2, CMP_LEN, HEAD_DIM, HEAD_DIM)
    full = lambda shape: pl.BlockSpec(shape, lambda bi: (0,) * len(shape))
    return pl.pallas_call(
        _nsa_compress_kernel,
        grid=(b,),
        in_specs=[pl.BlockSpec((1, s, LANES), lambda bi: (bi, 0, _SMALL_KC)),
                  pl.BlockSpec((1, s, LANES), lambda bi: (bi, 0, _SMALL_VC)),
                  full(cmp_pe.shape), full(w1.shape), full(cmp_b1.shape),
                  full(cmp_w2.shape), full(cmp_b2.shape)],
        out_specs=[pl.BlockSpec((1, n_chunk, HEAD_DIM), lambda bi: (bi, 0, 0)),
                   pl.BlockSpec((1, HEAD_DIM, n_chunk), lambda bi: (bi, 0, 0))],
        out_shape=[jax.ShapeDtypeStruct((b, n_chunk, HEAD_DIM), MXU_DTYPE),
                   jax.ShapeDtypeStruct((b, HEAD_DIM, n_chunk), MXU_DTYPE)],
        compiler_params=_params("arbitrary"),
        name="nsa_compress",
    )(u_small, u_small, cmp_pe, w1, cmp_b1, cmp_w2, cmp_b2)


def _mixer_b_kernel(q_ref, ks_ref, vs_ref, kw_ref, vw_ref, kc_ref, vct_ref, gl_ref, bg_ref,
                    tslc_ref, twin_ref, tcmp_ref, o_ref, vst_ref, vwt_ref):
    i = pl.program_id(1)
    nblk = vwt_ref.shape[0]
    nh = B_HEADS
    n_slc = ks_ref.shape[1] // SLC_LEN

    @pl.when(i == 0)
    def _():
        for kb in range(nblk):
            rows = slice(kb * BLK, (kb + 1) * BLK)
            vst_ref[:, rows] = vs_ref[0, rows, :].astype(F32).T.astype(MXU_DTYPE)
            vwt_ref[kb] = vw_ref[0, rows, :].astype(F32).T.astype(MXU_DTYPE)

    q4 = jnp.concatenate([q_ref[0, :, h * HEAD_DIM:(h + 1) * HEAD_DIM] for h in range(nh)], axis=0)

    st = _dot_nt(kc_ref[0], q4) * ATTN_SCALE + _lane_cat([tcmp_ref[h] for h in range(nh)])
    m = jnp.max(st, axis=0, keepdims=True)
    e = jnp.exp(st - m)
    l = jnp.sum(e, axis=0, keepdims=True)
    pc = jnp.where(m > 0.5 * NEG_INF, e / l, 0.0).astype(MXU_DTYPE)
    o_cmp = _dot(vct_ref[0], pc)
    n_cmp_pad = kc_ref.shape[1]
    jj = lax.broadcasted_iota(jnp.int32, (n_slc, n_cmp_pad), 0)
    nn = lax.broadcasted_iota(jnp.int32, (n_slc, n_cmp_pad), 1)
    overlap = ((nn * CMP_STRIDE <= jj * SLC_LEN + SLC_LEN - 1)
               & (nn * CMP_STRIDE + CMP_LEN - 1 >= jj * SLC_LEN)).astype(MXU_DTYPE)
    imp4 = _dot(overlap, pc)
    imp = functools.reduce(lambda a, b: a + b,
                           [imp4[:, h * BLK:(h + 1) * BLK] for h in range(nh)])
    t = i * BLK + lax.broadcasted_iota(jnp.int32, (n_slc, BLK), 1)
    blk = lax.broadcasted_iota(jnp.int32, (n_slc, BLK), 0)
    cur = t // SLC_LEN
    forced = (blk == 0) | (blk == cur) | (blk == cur - 1)
    causal = blk * SLC_LEN <= t
    score = jnp.where(forced, FORCE_SCORE, jnp.where(causal, imp, NEG_INF))
    rank = jnp.zeros((n_slc, BLK), jnp.int32)
    for r in range(n_slc):
        row = score[r:r + 1, :]
        ahead = (row > score) | ((row == score) & (r < blk))
        rank = rank + ahead.astype(jnp.int32)
    add = jnp.where(rank < min(N_SELECT, n_slc), 0.0, NEG_INF).astype(F32)
    add4 = _lane_cat([add] * nh)

    def bias_rows(t_ref, dl, n_real):
        idx = jnp.where(dl >= 0, dl, n_real)
        return _lane_cat([t_ref[idx * nh + h] for h in range(nh)])

    per_blk = BLK // SLC_LEN
    rows = []
    for kb in range(nblk):
        sel = jnp.concatenate(
            [jnp.broadcast_to(add4[kb * per_blk + j:kb * per_blk + j + 1, :], (SLC_LEN, nh * BLK))
             for j in range(per_blk)], axis=0)
        rows.append(bias_rows(tslc_ref, i - kb, nblk) + sel)
    s = _dot_nt(ks_ref[0], q4) * ATTN_SCALE + jnp.concatenate(rows, axis=0)
    o_slc, _, _ = _softmax_pv(s, vst_ref[...], axis=0)

    n_win = twin_ref.shape[0] // nh - 1
    kb0 = jnp.maximum(i - (n_win - 1), 0)
    start = pl.multiple_of(kb0 * BLK, BLK)
    bias = jnp.concatenate([bias_rows(twin_ref, i - kb0 - j, n_win) for j in range(n_win)],
                           axis=0)
    s = _dot_nt(kw_ref[0, pl.ds(start, n_win * BLK), :], q4) * ATTN_SCALE + bias
    o_win, _, _ = _softmax_pv(s, _lane_cat([vwt_ref[kb0 + j] for j in range(n_win)]), axis=0)

    gt = jax.nn.sigmoid(gl_ref[0] + bg_ref[...]).T

    def gate(br):
        r0 = _GATE_LANE0 + br * nh
        return _lane_cat([gt[r0 + h:r0 + h + 1, :] for h in range(nh)])

    o_t = gate(0) * o_cmp + gate(1) * o_slc + gate(2) * o_win
    for h in range(nh):
        o_ref[0, :, h * HEAD_DIM:(h + 1) * HEAD_DIM] = (
            o_t[:, h * BLK:(h + 1) * BLK].T.astype(o_ref.dtype))


def _mixer_b(u, u_small, kc, vct, b_nsa_gate, tiles, nblk):
    b, s, _ = u.shape
    nh = B_HEADS
    bg_pad = jnp.zeros((1, LANES), F32).at[0, _GATE_LANE0:_GATE_LANE0 + 3 * nh].set(b_nsa_gate)
    whole = lambda a: pl.BlockSpec(a.shape, lambda bi, i: (0, 0, 0))
    kv = lambda col: pl.BlockSpec((1, s, HEAD_DIM), lambda bi, i: (bi, 0, col))
    n_chunk = kc.shape[1]
    return pl.pallas_call(
        _mixer_b_kernel,
        grid=(b, nblk),
        in_specs=[pl.BlockSpec((1, BLK, nh * HEAD_DIM), lambda bi, i: (bi, i, _COL_B_Q // nh)),
                  kv(_COL_B_KS), kv(_COL_B_VS), kv(_COL_B_KW), kv(_COL_B_VW),
                  pl.BlockSpec((1, n_chunk, HEAD_DIM), lambda bi, i: (bi, 0, 0)),
                  pl.BlockSpec((1, HEAD_DIM, n_chunk), lambda bi, i: (bi, 0, 0)),
                  pl.BlockSpec((1, BLK, LANES), lambda bi, i: (bi, i, _SMALL_GATE)),
                  pl.BlockSpec((1, LANES), lambda bi, i: (0, 0)),
                  whole(tiles["slc"]), whole(tiles["win"]),
                  pl.BlockSpec((nh, BLK, BLK), lambda bi, i: (i, 0, 0))],
        out_specs=pl.BlockSpec((1, BLK, nh * HEAD_DIM), lambda bi, i: (bi, i, 0)),
        out_shape=jax.ShapeDtypeStruct((b, s, nh * HEAD_DIM), MXU_DTYPE),
        scratch_shapes=[pltpu.VMEM((HEAD_DIM, s), MXU_DTYPE),
                        pltpu.VMEM((nblk, HEAD_DIM, BLK), MXU_DTYPE)],
        compiler_params=_params("arbitrary", "arbitrary"),
        name="mixer_b",
    )(u, u, u, u, u, kc, vct, u_small, bg_pad, tiles["slc"], tiles["win"], tiles["cmp"])


def _layer_norm(y, g, b):
    mu = jnp.mean(y, axis=-1, keepdims=True)
    yc = y - mu
    var = jnp.mean(yc * yc, axis=-1, keepdims=True)
    return yc * lax.rsqrt(var + LN_EPS) * g + b


def _merge_kernel(x_ref, oa_ref, ob_ref, oc_ref, wga_ref, wgb_ref, wgc_ref, bga_ref, bgb_ref,
                  bgc_ref, wpa_ref, wpb_ref, wpc_ref, wo_ref, g_ref, b_ref, y_ref, yb_ref,
                  xb_ref, acc_ref, *, alpha):
    j = pl.program_id(1)

    @pl.when(j == 0)
    def _():
        xb_ref[...] = x_ref[...].astype(MXU_DTYPE)
        acc_ref[...] = jnp.zeros_like(acc_ref)

    xb = xb_ref[...]
    mixed = None
    for o_ref, wg_ref, bg_ref, wp_ref in ((oa_ref, wga_ref, bga_ref, wpa_ref),
                                          (ob_ref, wgb_ref, bgb_ref, wpb_ref),
                                          (oc_ref, wgc_ref, bgc_ref, wpc_ref)):
        gate = jax.nn.sigmoid(_dot(xb, wg_ref[...]) + bg_ref[...])
        term = gate * _dot(o_ref[...], wp_ref[...])
        mixed = term if mixed is None else mixed + term
    acc_ref[...] += _dot(mixed.astype(MXU_DTYPE), wo_ref[...])

    @pl.when(j == pl.num_programs(1) - 1)
    def _():
        y = _layer_norm(alpha * x_ref[...] + acc_ref[...], g_ref[...], b_ref[...])
        y_ref[...] = y
        yb_ref[...] = y.astype(MXU_DTYPE)


def _merge_ln(x2, oa, ob, oc, w_gate, b_gate, w_pa, w_pb, w_pc, w_out, ln_g, ln_b, alpha,
              tm, tn):
    t, d = x2.shape
    nj = d // tn
    row = lambda k: pl.BlockSpec((tm, k), lambda i, j: (i, 0))
    colw = lambda k, shift: pl.BlockSpec((k, tn), lambda i, j: (0, shift * nj + j))
    vec = lambda shift: pl.BlockSpec((1, tn), lambda i, j: (0, shift * nj + j))
    fullvec = pl.BlockSpec((1, d), lambda i, j: (0, 0))
    return pl.pallas_call(
        functools.partial(_merge_kernel, alpha=alpha),
        grid=(t // tm, nj),
        in_specs=[row(d), row(oa.shape[1]), row(ob.shape[1]), row(oc.shape[1]),
                  colw(d, 0), colw(d, 1), colw(d, 2), vec(0), vec(1), vec(2),
                  colw(w_pa.shape[0], 0), colw(w_pb.shape[0], 0), colw(w_pc.shape[0], 0),
                  pl.BlockSpec((tn, d), lambda i, j: (j, 0)), fullvec, fullvec],
        out_specs=[pl.BlockSpec((tm, d), lambda i, j: (i, 0)),
                   pl.BlockSpec((tm, d), lambda i, j: (i, 0))],
        out_shape=[jax.ShapeDtypeStruct((t, d), F32), jax.ShapeDtypeStruct((t, d), MXU_DTYPE)],
        scratch_shapes=[pltpu.VMEM((tm, d), MXU_DTYPE), pltpu.VMEM((tm, d), F32)],
        compiler_params=_params("arbitrary", "arbitrary"),
        name="merge_ln",
    )(x2, oa, ob, oc, w_gate, w_gate, w_gate, b_gate, b_gate, b_gate, w_pa, w_pb, w_pc, w_out,
      ln_g, ln_b)


_HALO = BF16_SUBLANES


def _ffn_kernel(x_ref, xb_ref, halo_ref, wa_ref, wb_ref, cwa_ref, cwb_ref, cba_ref, cbb_ref,
                wd_ref, g_ref, b_ref, y_ref, xe_ref, ha_ref, hb_ref, acc_ref, *, alpha,
                tiles_per_seq):
    i = pl.program_id(0)
    j = pl.program_id(1)
    tm = x_ref.shape[0]

    @pl.when(j == 0)
    def _():
        keep = (i % tiles_per_seq != 0).astype(MXU_DTYPE)
        xe_ref[0:_HALO, :] = halo_ref[...] * keep
        xe_ref[_HALO:, :] = xb_ref[...]
        acc_ref[...] = jnp.zeros_like(acc_ref)

    xe = xe_ref[...]
    ha_ref[...] = _dot(xe, wa_ref[...])
    hb_ref[...] = _dot(xe, wb_ref[...])

    def conv(h_ref, cw_ref, cb_ref):
        out = cb_ref[...]
        for tap in range(CONV_W):
            shift = CONV_W - 1 - tap
            out = out + cw_ref[tap:tap + 1, :] * h_ref[pl.ds(_HALO - shift, tm), :]
        return out

    a = conv(ha_ref, cwa_ref, cba_ref)
    bb = conv(hb_ref, cwb_ref, cbb_ref)
    acc_ref[...] += _dot((_gelu(a) * bb).astype(MXU_DTYPE), wd_ref[...])

    @pl.when(j == pl.num_programs(1) - 1)
    def _():
        y_ref[...] = _layer_norm(alpha * x_ref[...] + acc_ref[...], g_ref[...], b_ref[...])


def _ffn_ln(x2, xb2, w_up, conv_w, conv_b, w_down, ln_g, ln_b, alpha, seq, tm, tf):
    t, d = x2.shape
    f = w_down.shape[0]
    nj = f // tf
    hb = tm // _HALO
    fullvec = pl.BlockSpec((1, d), lambda i, j: (0, 0))
    return pl.pallas_call(
        functools.partial(_ffn_kernel, alpha=alpha, tiles_per_seq=seq // tm),
        grid=(t // tm, nj),
        in_specs=[pl.BlockSpec((tm, d), lambda i, j: (i, 0)),
                  pl.BlockSpec((tm, d), lambda i, j: (i, 0)),
                  pl.BlockSpec((_HALO, d), lambda i, j: (jnp.maximum(i * hb - 1, 0), 0)),
                  pl.BlockSpec((d, tf), lambda i, j: (0, j)),
                  pl.BlockSpec((d, tf), lambda i, j: (0, nj + j)),
                  pl.BlockSpec((CONV_W, tf), lambda i, j: (0, j)),
                  pl.BlockSpec((CONV_W, tf), lambda i, j: (0, nj + j)),
                  pl.BlockSpec((1, tf), lambda i, j: (0, j)),
                  pl.BlockSpec((1, tf), lambda i, j: (0, nj + j)),
                  pl.BlockSpec((tf, d), lambda i, j: (j, 0)), fullvec, fullvec],
        out_specs=pl.BlockSpec((tm, d), lambda i, j: (i, 0)),
        out_shape=jax.ShapeDtypeStruct((t, d), F32),
        scratch_shapes=[pltpu.VMEM((tm + _HALO, d), MXU_DTYPE),
                        pltpu.VMEM((tm + _HALO, tf), F32),
                        pltpu.VMEM((tm + _HALO, tf), F32),
                        pltpu.VMEM((tm, d), F32)],
        compiler_params=_params("arbitrary", "arbitrary"),
        name="ffn_ln",
    )(x2, xb2, xb2, w_up, w_up, conv_w, conv_w, conv_b, conv_b, w_down, ln_g, ln_b)


def _pack_moves():
    main_dst = {'a_q': _COL_A_Q, 'a_k': _COL_A_K, 'a_v': _COL_A_V,
                'c_q': _COL_C_Q, 'c_k': _COL_C_K, 'c_v': _COL_C_V, 'b_q': _COL_B_Q,
                'b_k_slc': _COL_B_KS, 'b_v_slc': _COL_B_VS, 'b_k_win': _COL_B_KW,
                'b_v_win': _COL_B_VW}
    moves = [(0, blk * HEAD_DIM, _SPLIT[n][0], _SPLIT[n][1] - _SPLIT[n][0])
             for n, blk in main_dst.items()]
    moves += [(1, _SMALL_KC * LANES, _SPLIT['b_k_cmp'][0], LANES),
              (1, _SMALL_VC * LANES, _SPLIT['b_v_cmp'][0], LANES),
              (1, _SMALL_GATE * LANES, _GATE_COL0, LANES),
              (1, _SMALL_FORGET * LANES, _FORGET_COL0, N_IN - _FORGET_COL0)]
    return moves


def _pack_kernel(w_ref, main_ref, small_ref):
    dst = (main_ref, small_ref)
    small_ref[0, :, _SMALL_FORGET * LANES:] = jnp.zeros(
        (small_ref.shape[1], small_ref.shape[2] - _SMALL_FORGET * LANES), small_ref.dtype)
    for which, d0, s0, width in _pack_moves():
        dst[which][0, :, d0:d0 + width] = w_ref[0, :, s0:s0 + width].astype(dst[which].dtype)


def _pack_w_in(w_in, tk):
    depth, d, n_in = w_in.shape
    assert n_in == N_IN
    n_main, n_small = _N_COL_BLOCKS * HEAD_DIM, _SMALL_BLOCKS * LANES
    return pl.pallas_call(
        _pack_kernel,
        grid=(depth, d // tk),
        in_specs=[pl.BlockSpec((1, tk, n_in), lambda l, i: (l, i, 0))],
        out_specs=[pl.BlockSpec((1, tk, n_main), lambda l, i: (l, i, 0)),
                   pl.BlockSpec((1, tk, n_small), lambda l, i: (l, i, 0))],
        out_shape=[jax.ShapeDtypeStruct((depth, d, n_main), MXU_DTYPE),
                   jax.ShapeDtypeStruct((depth, d, n_small), MXU_DTYPE)],
        compiler_params=_params("arbitrary", "arbitrary"),
        name="pack_w_in",
    )(w_in)


def kernel(x, rel_bias, w_in, b_f, b_nsa_gate, cmp_pe, cmp_w1, cmp_b1, cmp_w2, cmp_b2, w_gate, b_gate, w_pa, w_pb, w_pc, w_out, ln1_g, ln1_b, w_up, conv_w, conv_b, w_down, ln2_g, ln2_b):
    bsz, seq, d = x.shape
    depth = w_in.shape[0]
    nblk = seq // BLK
    alpha = (2 * depth) ** 0.25
    t = bsz * seq
    tm = 512
    bf = lambda a: a.astype(MXU_DTYPE)

    tiles = _bias_tiles(rel_bias, nblk)
    w_main, w_small = _pack_w_in(w_in, 256)
    x2 = x.reshape(t, d)
    for l in range(depth):
        u = _proj(x2, w_main[l], MXU_DTYPE, 1024, 512).reshape(bsz, seq, -1)
        u_small = _proj(x2, w_small[l], F32, 1024, w_small.shape[2]).reshape(bsz, seq, -1)

        o_a = _mixer_a(u, tiles["a"], nblk)
        kc, vct = _nsa_compress(u_small, cmp_pe[l], bf(cmp_w1[l]), cmp_b1[l], bf(cmp_w2[l]),
                                cmp_b2[l])
        o_b = _mixer_b(u, u_small, kc, vct, b_nsa_gate[l], tiles, nblk)
        cdec = _fox_decay(u_small, b_f[l])
        o_c = _mixer_c(u, cdec)

        x1, x1b = _merge_ln(x2, o_a.reshape(t, -1), o_b.reshape(t, -1), o_c.reshape(t, -1),
                            bf(w_gate[l]), b_gate[l][None], bf(w_pa[l]), bf(w_pb[l]), bf(w_pc[l]),
                            bf(w_out[l]), ln1_g[l][None], ln1_b[l][None], alpha, tm, 256)
        x2 = _ffn_ln(x1, x1b, bf(w_up[l]), conv_w[l], conv_b[l][None], bf(w_down[l]),
                     ln2_g[l][None], ln2_b[l][None], alpha, seq, tm, 512)
    return x2.reshape(bsz, seq, d)
```

```python
import functools
import math

import numpy as np
import jax
import jax.numpy as jnp
from jax import lax
from jax.experimental import pallas as pl
from jax.experimental.pallas import tpu as pltpu

HEAD_DIM = 128
A_GROUPS = ((128, 1), (512, 4), (2048, 16))
A_HEADS_PER_GROUP = 2
A_HEADS = A_HEADS_PER_GROUP * len(A_GROUPS)
B_HEADS = 4
CMP_LEN = 32
CMP_STRIDE = 16
SLC_LEN = 64
N_SELECT = 16
WIN = 512
C_HEADS = 6
N_BUCKETS = 32
MAX_DISTANCE = 2048
CONV_W = 3
LN_EPS = 1e-5
NEG_INF = -1e30
FORCE_SCORE = 1e9
ATTN_SCALE = HEAD_DIM ** -0.5

LANES = 128
VMEM_LIMIT_BYTES = 56 * 2 ** 20

MXU_DTYPE = jnp.bfloat16
F32 = jnp.float32
BLK = LANES

_COL_A_Q, _COL_A_K, _COL_A_V = 0, 6, 12
_COL_C_Q, _COL_C_K, _COL_C_V = 18, 24, 30
_COL_B_Q = 36
_COL_B_KS, _COL_B_VS, _COL_B_KW, _COL_B_VW = 40, 41, 42, 43
_N_COL_BLOCKS = 44


def _in_split_offsets():
    widths = (('a_q', A_HEADS * HEAD_DIM), ('a_k', A_HEADS * HEAD_DIM), ('a_v', A_HEADS * HEAD_DIM),
              ('b_q', B_HEADS * HEAD_DIM), ('b_k_cmp', HEAD_DIM), ('b_v_cmp', HEAD_DIM),
              ('b_k_slc', HEAD_DIM), ('b_v_slc', HEAD_DIM), ('b_k_win', HEAD_DIM),
              ('b_v_win', HEAD_DIM), ('b_gate', 3 * B_HEADS),
              ('c_q', C_HEADS * HEAD_DIM), ('c_k', C_HEADS * HEAD_DIM), ('c_v', C_HEADS * HEAD_DIM),
              ('c_f', C_HEADS))
    out, o = {}, 0
    for name, w in widths:
        out[name] = (o, o + w)
        o += w
    return out, o


_SPLIT, N_IN = _in_split_offsets()
_SMALL_KC, _SMALL_VC, _SMALL_GATE, _SMALL_FORGET = 0, 1, 2, 3
_SMALL_BLOCKS = 4
_GATE_COL0 = _SPLIT['b_gate'][0] // LANES * LANES
_GATE_LANE0 = _SPLIT['b_gate'][0] - _GATE_COL0
_FORGET_COL0 = _SPLIT['c_f'][0] // LANES * LANES
_FORGET_LANE0 = _SPLIT['c_f'][0] - _FORGET_COL0
assert _SPLIT['b_gate'][1] - _GATE_COL0 <= LANES and N_IN - _FORGET_COL0 <= LANES
_FORGET_ROW0 = _FORGET_LANE0 // 8 * 8
_FORGET_ROWS = -(-(_FORGET_LANE0 + C_HEADS - _FORGET_ROW0) // 8) * 8


def _params(*sem):
    return pltpu.CompilerParams(dimension_semantics=sem, vmem_limit_bytes=VMEM_LIMIT_BYTES)


def _dot(a, b):
    return jnp.dot(a, b, preferred_element_type=F32)


def _dot_nt(a, b):
    return lax.dot_general(a, b, (((1,), (1,)), ((), ())), preferred_element_type=F32)


def _bias_tile_kernel(base_ref, rmul_ref, cmul_ref, dmask_ref, dmax_ref, rmax_ref, head_ref,
                      rel_ref, out_ref):
    t = pl.program_id(0)
    row = lax.broadcasted_iota(jnp.int32, (BLK, BLK), 0)
    col = lax.broadcasted_iota(jnp.int32, (BLK, BLK), 1)
    dist = base_ref[t] + rmul_ref[t] * row + cmul_ref[t] * col
    valid = ((dist >= 0) & (dist <= dmax_ref[t]) & ((dist & dmask_ref[t]) == 0)
             & (row < rmax_ref[t]))
    max_exact = N_BUCKETS // 2
    d = jnp.maximum(dist, 0)
    log_ratio = (jnp.log(jnp.maximum(d, 1).astype(F32) / max_exact)
                 / math.log(MAX_DISTANCE / max_exact))
    large = jnp.minimum(max_exact + (log_ratio * (N_BUCKETS - max_exact)).astype(jnp.int32),
                        N_BUCKETS - 1)
    bucket = jnp.where(d < max_exact, d, large)
    h = head_ref[t]
    val = jnp.zeros((BLK, BLK), F32)
    for b in range(N_BUCKETS):
        val = jnp.where(bucket == b, rel_ref[b, h], val)
    out_ref[0] = jnp.where(valid, val, NEG_INF)


def _a_tile_counts(nblk):
    return [min(window // BLK + 1, nblk) for window, _ in A_GROUPS]


def _tile_tables(nblk):
    big = 2 ** 30
    fam = {}
    fam["a"] = [(dl * BLK, 1, -1, dil - 1, window if dl < cnt else -1, BLK,
                 g * A_HEADS_PER_GROUP + hh)
                for g, ((window, dil), cnt) in enumerate(zip(A_GROUPS, _a_tile_counts(nblk)))
                for hh in range(A_HEADS_PER_GROUP)
                for dl in range(cnt + 1)]
    fam["slc"] = [(dl * BLK, -1, 1, 0, big if dl < nblk else -1, BLK, A_HEADS + h)
                  for dl in range(nblk + 1) for h in range(B_HEADS)]
    n_win = WIN // BLK + 1
    fam["win"] = [(dl * BLK, -1, 1, 0, WIN - 1 if dl < n_win else -1, BLK, A_HEADS + h)
                  for dl in range(n_win + 1) for h in range(B_HEADS)]
    n_cmp = (nblk * BLK - CMP_LEN) // CMP_STRIDE + 1
    fam["cmp"] = [(i * BLK - (CMP_LEN - 1), -CMP_STRIDE, 1, 0, big, n_cmp, A_HEADS + h)
                  for i in range(nblk) for h in range(B_HEADS)]
    return {k: [np.asarray(c, np.int32) for c in zip(*rows)] for k, rows in fam.items()}


def _a_tile_offsets(nblk):
    cnt = _a_tile_counts(nblk)
    return [A_HEADS_PER_GROUP * sum(c + 1 for c in cnt[:g]) for g in range(len(A_GROUPS))]


def _bias_tiles(rel_bias, nblk):
    out = {}
    for name, cols in _tile_tables(nblk).items():
        n = cols[0].shape[0]
        out[name] = pl.pallas_call(
            _bias_tile_kernel,
            grid_spec=pltpu.PrefetchScalarGridSpec(
                num_scalar_prefetch=len(cols),
                grid=(n,),
                in_specs=[pl.BlockSpec(memory_space=pltpu.SMEM)],
                out_specs=pl.BlockSpec((1, BLK, BLK), lambda t, *_: (t, 0, 0)),
            ),
            out_shape=jax.ShapeDtypeStruct((n, BLK, BLK), F32),
            compiler_params=_params("arbitrary"),
            name="bias_tiles_" + name,
        )(*[jnp.asarray(c) for c in cols], rel_bias)
    return out


def _proj_kernel(x_ref, w_ref, o_ref, xb_ref):
    @pl.when(pl.program_id(1) == 0)
    def _():
        xb_ref[...] = x_ref[...].astype(MXU_DTYPE)

    o_ref[...] = _dot(xb_ref[...], w_ref[...]).astype(o_ref.dtype)


def _proj(x2, w, out_dtype, tm, tn):
    t, d = x2.shape
    n = w.shape[1]
    return pl.pallas_call(
        _proj_kernel,
        grid=(t // tm, n // tn),
        in_specs=[pl.BlockSpec((tm, d), lambda i, j: (i, 0)),
                  pl.BlockSpec((d, tn), lambda i, j: (0, j))],
        out_specs=pl.BlockSpec((tm, tn), lambda i, j: (i, j)),
        out_shape=jax.ShapeDtypeStruct((t, n), out_dtype),
        scratch_shapes=[pltpu.VMEM((tm, d), MXU_DTYPE)],
        compiler_params=_params("arbitrary", "arbitrary"),
        name="in_proj",
    )(x2, w)


def _lane_cat(xs):
    return jnp.concatenate(xs, axis=1)


def _softmax_pv(s, v, axis):
    m = jnp.max(s, axis=axis, keepdims=True)
    p = jnp.exp(s - m)
    l = jnp.sum(p, axis=axis, keepdims=True)
    if axis == 1:
        o = _dot(p.astype(MXU_DTYPE), v)
    else:
        o = _dot(v, p.astype(MXU_DTYPE))
    return o / l, m, l


def _mixer_a_kernel(q_ref, k_ref, v_ref, t_ref, o_ref, *, tile_off, tile_cnt):
    i = pl.program_id(1)
    hpg = A_HEADS_PER_GROUP
    outs = [[None] * len(A_GROUPS) for _ in range(hpg)]
    lses = [[None] * len(A_GROUPS) for _ in range(hpg)]
    for g in range(len(A_GROUPS)):
        nd = tile_cnt[g]
        kb0 = jnp.maximum(i - (nd - 1), 0)
        start = pl.multiple_of(kb0 * BLK, BLK)
        for hh in range(hpg):
            c0 = (g * hpg + hh) * HEAD_DIM
            q = q_ref[0, :, c0:c0 + HEAD_DIM]
            k = k_ref[0, pl.ds(start, nd * BLK), c0:c0 + HEAD_DIM]
            v = v_ref[0, pl.ds(start, nd * BLK), c0:c0 + HEAD_DIM]
            tbase = tile_off[g] + hh * (nd + 1)
            deltas = [i - kb0 - j for j in range(nd)]
            bias = _lane_cat([t_ref[tbase + jnp.where(dl >= 0, dl, nd)] for dl in deltas])
            s = _dot_nt(q, k) * ATTN_SCALE + bias
            o, m, l = _softmax_pv(s, v, axis=1)
            outs[hh][g] = o
            lses[hh][g] = m + jnp.log(l)
    for hh in range(hpg):
        mx = functools.reduce(jnp.maximum, lses[hh])
        es = [jnp.exp(x - mx) for x in lses[hh]]
        tot = functools.reduce(lambda a, b: a + b, es)
        o = functools.reduce(lambda a, b: a + b,
                             [(e / tot) * og for e, og in zip(es, outs[hh])])
        o_ref[0, :, hh * HEAD_DIM:(hh + 1) * HEAD_DIM] = o.astype(o_ref.dtype)


def _mixer_a(u, tiles_a, nblk):
    b, s, _ = u.shape
    w = A_HEADS * HEAD_DIM
    kern = functools.partial(_mixer_a_kernel, tile_off=tuple(_a_tile_offsets(nblk)),
                             tile_cnt=tuple(_a_tile_counts(nblk)))
    n_a = tiles_a.shape[0]
    return pl.pallas_call(
        kern,
        grid=(b, nblk),
        in_specs=[pl.BlockSpec((1, BLK, w), lambda bi, i: (bi, i, _COL_A_Q // A_HEADS)),
                  pl.BlockSpec((1, s, w), lambda bi, i: (bi, 0, _COL_A_K // A_HEADS)),
                  pl.BlockSpec((1, s, w), lambda bi, i: (bi, 0, _COL_A_V // A_HEADS)),
                  pl.BlockSpec((n_a, BLK, BLK), lambda bi, i: (0, 0, 0))],
        out_specs=pl.BlockSpec((1, BLK, A_HEADS_PER_GROUP * HEAD_DIM), lambda bi, i: (bi, i, 0)),
        out_shape=jax.ShapeDtypeStruct((b, s, A_HEADS_PER_GROUP * HEAD_DIM), MXU_DTYPE),
        compiler_params=_params("arbitrary", "arbitrary"),
        name="mixer_a",
    )(u, u, u, tiles_a)


_C_TK = 256


def _split3(x):
    hi = x.astype(MXU_DTYPE)
    r1 = x - hi.astype(F32)
    mid = r1.astype(MXU_DTYPE)
    lo = (r1 - mid.astype(F32)).astype(MXU_DTYPE)
    return hi, mid, lo


def _fox_decay_kernel(f_ref, bf_ref, c_ref):
    x = f_ref[0] + bf_ref[...]
    logf = jnp.minimum(x, 0.0) - jnp.log1p(jnp.exp(-jnp.abs(x)))
    logf_t = logf.T[_FORGET_ROW0:_FORGET_ROW0 + _FORGET_ROWS]
    nk = c_ref.shape[1]
    r = lax.broadcasted_iota(jnp.int32, (_C_TK, _C_TK), 0)
    c = lax.broadcasted_iota(jnp.int32, (_C_TK, _C_TK), 1)
    upper = (r <= c).astype(MXU_DTYPE)
    carry = jnp.zeros((_FORGET_ROWS, 1), F32)
    for j in range(nk):
        hi, mid, lo = _split3(logf_t[:, j * _C_TK:(j + 1) * _C_TK])
        cs = (_dot(hi, upper) + _dot(mid, upper)) + _dot(lo, upper) + carry
        c_ref[0, j] = cs
        carry = cs[:, _C_TK - 1:_C_TK]


def _fox_decay(u_small, b_f):
    b, s, _ = u_small.shape
    nk = s // _C_TK
    bf_pad = jnp.zeros((1, LANES), F32).at[0, _FORGET_LANE0:_FORGET_LANE0 + C_HEADS].set(b_f)
    return pl.pallas_call(
        _fox_decay_kernel,
        grid=(b,),
        in_specs=[pl.BlockSpec((1, s, LANES), lambda bi: (bi, 0, _SMALL_FORGET)),
                  pl.BlockSpec((1, LANES), lambda bi: (0, 0))],
        out_specs=pl.BlockSpec((1, nk, _FORGET_ROWS, _C_TK), lambda bi: (bi, 0, 0, 0)),
        out_shape=jax.ShapeDtypeStruct((b, nk, _FORGET_ROWS, _C_TK), F32),
        compiler_params=_params("arbitrary"),
        name="fox_decay",
    )(u_small, bf_pad)


_C_TQ = 512


def _mixer_c_kernel(q_ref, k_ref, v_ref, c_ref, o_ref):
    row = _FORGET_LANE0 - _FORGET_ROW0 + pl.program_id(1)
    tq = _C_TQ
    r = lax.broadcasted_iota(jnp.int32, (tq, tq), 0)
    c = lax.broadcasted_iota(jnp.int32, (tq, tq), 1)
    for i in range(q_ref.shape[1] // tq):
        nk = (i + 1) * tq
        q = q_ref[0, i * tq:(i + 1) * tq, :]
        decay = _lane_cat([c_ref[0, j, pl.ds(row, 1), :] for j in range(nk // _C_TK)])
        s = _dot_nt(q, k_ref[0, :nk, :]) * ATTN_SCALE - decay
        diag = jnp.where(c <= r, s[:, nk - tq:], NEG_INF)
        s = diag if i == 0 else _lane_cat([s[:, :nk - tq], diag])
        o, _, _ = _softmax_pv(s, v_ref[0, :nk, :], axis=1)
        o_ref[0, i * tq:(i + 1) * tq, :] = o.astype(o_ref.dtype)


def _mixer_c(u, cdec):
    b, s, _ = u.shape
    nk = s // _C_TK
    head = lambda col: pl.BlockSpec((1, s, HEAD_DIM), lambda bi, h: (bi, 0, col + h))
    return pl.pallas_call(
        _mixer_c_kernel,
        grid=(b, C_HEADS),
        in_specs=[head(_COL_C_Q), head(_COL_C_K), head(_COL_C_V),
                  pl.BlockSpec((1, nk, _FORGET_ROWS, _C_TK), lambda bi, h: (bi, 0, 0, 0))],
        out_specs=head(0),
        out_shape=jax.ShapeDtypeStruct((b, s, C_HEADS * HEAD_DIM), MXU_DTYPE),
        compiler_params=_params("arbitrary", "arbitrary"),
        name="mixer_c",
    )(u, u, u, cdec)


def _gelu(x):
    return 0.5 * x * (1.0 + lax.erf(x * np.float32(math.sqrt(0.5))))


def _nsa_compress_kernel(k_ref, v_ref, pe_ref, w1_ref, b1_ref, w2_ref, b2_ref, kc_ref, vct_ref):
    n_chunk = k_ref.shape[1] // CMP_STRIDE
    halves = CMP_LEN // CMP_STRIDE
    res = []
    for c, src in enumerate((k_ref, v_ref)):
        parts = []
        for half in range(halves):
            acc = jnp.zeros((n_chunk, HEAD_DIM), F32)
            for p in range(CMP_STRIDE):
                pos = half * CMP_STRIDE + p
                rows = src[0, pl.ds(p, n_chunk, stride=CMP_STRIDE), :] + pe_ref[c, pos:pos + 1, :]
                acc = acc + _dot(rows.astype(MXU_DTYPE), w1_ref[c, pos])
            parts.append(acc)
        hid = parts[0]
        for half in range(1, halves):
            hid = hid + pltpu.roll(parts[half], n_chunk - half, 0)
        hid = _gelu(hid + b1_ref[c:c + 1, :])
        res.append(_dot(hid.astype(MXU_DTYPE), w2_ref[c]) + b2_ref[c:c + 1, :])
    kc_ref[0] = res[0].astype(kc_ref.dtype)
    vct_ref[0] = res[1].T.astype(vct_ref.dtype)


def _nsa_compress(u_small, cmp_pe, cmp_w1, cmp_b1, cmp_w2, cmp_b2):
    b, s, _ = u_small.shape
    n_chunk = s // CMP_STRIDE
    w1 = cmp_w1.reshape(2, CMP_LEN, HEAD_DIM, HEAD_DIM)
    full = lambda shape: pl.BlockSpec(shape, lambda bi: (0,) * len(shape))
    return pl.pallas_call(
        _nsa_compress_kernel,
        grid=(b,),
        in_specs=[pl.BlockSpec((1, s, LANES), lambda bi: (bi, 0, _SMALL_KC)),
                  pl.BlockSpec((1, s, LANES), lambda bi: (bi, 0, _SMALL_VC)),
                  full(cmp_pe.shape), full(w1.shape), full(cmp_b1.shape),
                  full(cmp_w2.shape), full(cmp_b2.shape)],
        out_specs=[pl.BlockSpec((1, n_chunk, HEAD_DIM), lambda bi: (bi, 0, 0)),
                   pl.BlockSpec((1, HEAD_DIM, n_chunk), lambda bi: (bi, 0, 0))],
        out_shape=[jax.ShapeDtypeStruct((b, n_chunk, HEAD_DIM), MXU_DTYPE),
                   jax.ShapeDtypeStruct((b, HEAD_DIM, n_chunk), MXU_DTYPE)],
        compiler_params=_params("arbitrary"),
        name="nsa_compress",
    )(u_small, u_small, cmp_pe, w1, cmp_b1, cmp_w2, cmp_b2)


def _mixer_b_kernel(q_ref, ks_ref, vs_ref, kw_ref, vw_ref, kc_ref, vct_ref, gl_ref, bg_ref,
                    tslc_ref, twin_ref, tcmp_ref, o_ref, vst_ref, vwt_ref):
    i = pl.program_id(1)
    nblk = vwt_ref.shape[0]
    nh = B_HEADS
    n_slc = ks_ref.shape[1] // SLC_LEN

    @pl.when(i == 0)
    def _():
        for kb in range(nblk):
            rows = slice(kb * BLK, (kb + 1) * BLK)
            vst_ref[:, rows] = vs_ref[0, rows, :].astype(F32).T.astype(MXU_DTYPE)
            vwt_ref[kb] = vw_ref[0, rows, :].astype(F32).T.astype(MXU_DTYPE)

    q4 = jnp.concatenate([q_ref[0, :, h * HEAD_DIM:(h + 1) * HEAD_DIM] for h in range(nh)], axis=0)

    st = _dot_nt(kc_ref[0], q4) * ATTN_SCALE + _lane_cat([tcmp_ref[h] for h in range(nh)])
    m = jnp.max(st, axis=0, keepdims=True)
    e = jnp.exp(st - m)
    l = jnp.sum(e, axis=0, keepdims=True)
    pc = jnp.where(m > 0.5 * NEG_INF, e / l, 0.0).astype(MXU_DTYPE)
    o_cmp = _dot(vct_ref[0], pc)
    n_cmp_pad = kc_ref.shape[1]
    jj = lax.broadcasted_iota(jnp.int32, (n_slc, n_cmp_pad), 0)
    nn = lax.broadcasted_iota(jnp.int32, (n_slc, n_cmp_pad), 1)
    overlap = ((nn * CMP_STRIDE <= jj * SLC_LEN + SLC_LEN - 1)
               & (nn * CMP_STRIDE + CMP_LEN - 1 >= jj * SLC_LEN)).astype(MXU_DTYPE)
    imp4 = _dot(overlap, pc)
    imp = functools.reduce(lambda a, b: a + b,
                           [imp4[:, h * BLK:(h + 1) * BLK] for h in range(nh)])
    t = i * BLK + lax.broadcasted_iota(jnp.int32, (n_slc, BLK), 1)
    blk = lax.broadcasted_iota(jnp.int32, (n_slc, BLK), 0)
    cur = t // SLC_LEN
    forced = (blk == 0) | (blk == cur) | (blk == cur - 1)
    causal = blk * SLC_LEN <= t
    score = jnp.where(forced, FORCE_SCORE, jnp.where(causal, imp, NEG_INF))
    rank = jnp.zeros((n_slc, BLK), jnp.int32)
    for r in range(n_slc):
        row = score[r:r + 1, :]
        ahead = (row > score) | ((row == score) & (r < blk))
        rank = rank + ahead.astype(jnp.int32)
    add = jnp.where(rank < min(N_SELECT, n_slc), 0.0, NEG_INF).astype(F32)
    add4 = _lane_cat([add] * nh)

    def bias_rows(t_ref, dl, n_real):
        idx = jnp.where(dl >= 0, dl, n_real)
        return _lane_cat([t_ref[idx * nh + h] for h in range(nh)])

    per_blk = BLK // SLC_LEN
    rows = []
    for kb in range(nblk):
        sel = jnp.concatenate(
            [jnp.broadcast_to(add4[kb * per_blk + j:kb * per_blk + j + 1, :], (SLC_LEN, nh * BLK))
             for j in range(per_blk)], axis=0)
        rows.append(bias_rows(tslc_ref, i - kb, nblk) + sel)
    s = _dot_nt(ks_ref[0], q4) * ATTN_SCALE + jnp.concatenate(rows, axis=0)
    o_slc, _, _ = _softmax_pv(s, vst_ref[...], axis=0)

    n_win = twin_ref.shape[0] // nh - 1
    kb0 = jnp.maximum(i - (n_win - 1), 0)
    start = pl.multiple_of(kb0 * BLK, BLK)
    bias = jnp.concatenate([bias_rows(twin_ref, i - kb0 - j, n_win) for j in range(n_win)],
                           axis=0)
    s = _dot_nt(kw_ref[0, pl.ds(start, n_win * BLK), :], q4) * ATTN_SCALE + bias
    o_win, _, _ = _softmax_pv(s, _lane_cat([vwt_ref[kb0 + j] for j in range(n_win)]), axis=0)

    gt = jax.nn.sigmoid(gl_ref[0] + bg_ref[...]).T

    def gate(br):
        r0 = _GATE_LANE0 + br * nh
        return _lane_cat([gt[r0 + h:r0 + h + 1, :] for h in range(nh)])

    o_t = gate(0) * o_cmp + gate(1) * o_slc + gate(2) * o_win
    for h in range(nh):
        o_ref[0, :, h * HEAD_DIM:(h + 1) * HEAD_DIM] = (
            o_t[:, h * BLK:(h + 1) * BLK].T.astype(o_ref.dtype))


def _mixer_b(u, u_small, kc, vct, b_nsa_gate, tiles, nblk):
    b, s, _ = u.shape
    nh = B_HEADS
    bg_pad = jnp.zeros((1, LANES), F32).at[0, _GATE_LANE0:_GATE_LANE0 + 3 * nh].set(b_nsa_gate)
    whole = lambda a: pl.BlockSpec(a.shape, lambda bi, i: (0, 0, 0))
    kv = lambda col: pl.BlockSpec((1, s, HEAD_DIM), lambda bi, i: (bi, 0, col))
    n_chunk = kc.shape[1]
    return pl.pallas_call(
        _mixer_b_kernel,
        grid=(b, nblk),
        in_specs=[pl.BlockSpec((1, BLK, nh * HEAD_DIM), lambda bi, i: (bi, i, _COL_B_Q // nh)),
                  kv(_COL_B_KS), kv(_COL_B_VS), kv(_COL_B_KW), kv(_COL_B_VW),
                  pl.BlockSpec((1, n_chunk, HEAD_DIM), lambda bi, i: (bi, 0, 0)),
                  pl.BlockSpec((1, HEAD_DIM, n_chunk), lambda bi, i: (bi, 0, 0)),
                  pl.BlockSpec((1, BLK, LANES), lambda bi, i: (bi, i, _SMALL_GATE)),
                  pl.BlockSpec((1, LANES), lambda bi, i: (0, 0)),
                  whole(tiles["slc"]), whole(tiles["win"]),
                  pl.BlockSpec((nh, BLK, BLK), lambda bi, i: (i, 0, 0))],
        out_specs=pl.BlockSpec((1, BLK, nh * HEAD_DIM), lambda bi, i: (bi, i, 0)),
        out_shape=jax.ShapeDtypeStruct((b, s, nh * HEAD_DIM), MXU_DTYPE),
        scratch_shapes=[pltpu.VMEM((HEAD_DIM, s), MXU_DTYPE),
                        pltpu.VMEM((nblk, HEAD_DIM, BLK), MXU_DTYPE)],
        compiler_params=_params("arbitrary", "arbitrary"),
        name="mixer_b",
    )(u, u, u, u, u, kc, vct, u_small, bg_pad, tiles["slc"], tiles["win"], tiles["cmp"])


def _layer_norm(y, g, b):
    mu = jnp.mean(y, axis=-1, keepdims=True)
    yc = y - mu
    var = jnp.mean(yc * yc, axis=-1, keepdims=True)
    return yc * lax.rsqrt(var + LN_EPS) * g + b


def _merge_kernel(x_ref, oa_ref, ob_ref, oc_ref, wga_ref, wgb_ref, wgc_ref, bga_ref, bgb_ref,
                  bgc_ref, wpa_ref, wpb_ref, wpc_ref, wo_ref, g_ref, b_ref, y_ref, xb_ref, *,
                  alpha):
    j = pl.program_id(1)

    @pl.when(j == 0)
    def _():
        xb_ref[...] = x_ref[...].astype(MXU_DTYPE)
        y_ref[...] = jnp.zeros_like(y_ref)

    xb = xb_ref[...]
    mixed = None
    for o_ref, wg_ref, bg_ref, wp_ref in ((oa_ref, wga_ref, bga_ref, wpa_ref),
                                          (ob_ref, wgb_ref, bgb_ref, wpb_ref),
                                          (oc_ref, wgc_ref, bgc_ref, wpc_ref)):
        gate = jax.nn.sigmoid(_dot(xb, wg_ref[...]) + bg_ref[...])
        term = gate * _dot(o_ref[...], wp_ref[...])
        mixed = term if mixed is None else mixed + term
    y_ref[...] += _dot(mixed.astype(MXU_DTYPE), wo_ref[...])

    @pl.when(j == pl.num_programs(1) - 1)
    def _():
        y_ref[...] = _layer_norm(alpha * x_ref[...] + y_ref[...], g_ref[...], b_ref[...])


def _merge_ln(x2, oa, ob, oc, w_gate, b_gate, w_pa, w_pb, w_pc, w_out, ln_g, ln_b, alpha,
              tm, tn):
    t, d = x2.shape
    nj = d // tn
    row = lambda k, **kw: pl.BlockSpec((tm, k), lambda i, j: (i, 0), **kw)
    colw = lambda k, shift: pl.BlockSpec((k, tn), lambda i, j: (0, shift * nj + j))
    vec = lambda shift: pl.BlockSpec((1, tn), lambda i, j: (0, shift * nj + j))
    fullvec = pl.BlockSpec((1, d), lambda i, j: (0, 0))
    return pl.pallas_call(
        functools.partial(_merge_kernel, alpha=alpha),
        grid=(t // tm, nj),
        in_specs=[row(d, pipeline_mode=pl.Buffered(1)),
                  row(oa.shape[1]), row(ob.shape[1]), row(oc.shape[1]),
                  colw(d, 0), colw(d, 1), colw(d, 2), vec(0), vec(1), vec(2),
                  colw(w_pa.shape[0], 0), colw(w_pb.shape[0], 0), colw(w_pc.shape[0], 0),
                  pl.BlockSpec((tn, d), lambda i, j: (j, 0)), fullvec, fullvec],
        out_specs=pl.BlockSpec((tm, d), lambda i, j: (i, 0), pipeline_mode=pl.Buffered(1)),
        out_shape=jax.ShapeDtypeStruct((t, d), F32),
        scratch_shapes=[pltpu.VMEM((tm, d), MXU_DTYPE)],
        compiler_params=_params("arbitrary", "arbitrary"),
        name="merge_ln",
    )(x2, oa, ob, oc, w_gate, w_gate, w_gate, b_gate, b_gate, b_gate, w_pa, w_pb, w_pc, w_out,
      ln_g, ln_b)


_TAIL = 8


def _ffn_kernel(x_ref, wa_ref, wb_ref, cwa_ref, cwb_ref, cba_ref, cbb_ref, wd_ref, g_ref, b_ref,
                y_ref, xb_ref, ha_ref, hb_ref, tail_ref, *, alpha, tiles_per_seq):
    i = pl.program_id(0)
    j = pl.program_id(1)
    tm = x_ref.shape[0]
    tf = wa_ref.shape[1]

    @pl.when(j == 0)
    def _():
        xb_ref[...] = x_ref[...].astype(MXU_DTYPE)
        y_ref[...] = jnp.zeros_like(y_ref)

    @pl.when(i % tiles_per_seq == 0)
    def _():
        tail_ref[j] = jnp.zeros(tail_ref.shape[1:], F32)

    xb = xb_ref[...]
    for h_ref, w_ref, c0 in ((ha_ref, wa_ref, 0), (hb_ref, wb_ref, tf)):
        h_ref[0:_TAIL, :] = tail_ref[j, :, c0:c0 + tf]
        h_ref[_TAIL:, :] = _dot(xb, w_ref[...])
        tail_ref[j, :, c0:c0 + tf] = h_ref[tm:tm + _TAIL, :]

    def conv(h_ref, cw_ref, cb_ref):
        out = cb_ref[...]
        for tap in range(CONV_W):
            shift = CONV_W - 1 - tap
            out = out + cw_ref[tap:tap + 1, :] * h_ref[pl.ds(_TAIL - shift, tm), :]
        return out

    a = conv(ha_ref, cwa_ref, cba_ref)
    bb = conv(hb_ref, cwb_ref, cbb_ref)
    y_ref[...] += _dot((_gelu(a) * bb).astype(MXU_DTYPE), wd_ref[...])

    @pl.when(j == pl.num_programs(1) - 1)
    def _():
        y_ref[...] = _layer_norm(alpha * x_ref[...] + y_ref[...], g_ref[...], b_ref[...])


def _ffn_ln(x2, w_up, conv_w, conv_b, w_down, ln_g, ln_b, alpha, seq, tm, tf):
    t, d = x2.shape
    f = w_down.shape[0]
    nj = f // tf
    assert seq % tm == 0 and CONV_W - 1 <= _TAIL
    fullvec = pl.BlockSpec((1, d), lambda i, j: (0, 0))
    return pl.pallas_call(
        functools.partial(_ffn_kernel, alpha=alpha, tiles_per_seq=seq // tm),
        grid=(t // tm, nj),
        in_specs=[pl.BlockSpec((tm, d), lambda i, j: (i, 0), pipeline_mode=pl.Buffered(1)),
                  pl.BlockSpec((d, tf), lambda i, j: (0, j)),
                  pl.BlockSpec((d, tf), lambda i, j: (0, nj + j)),
                  pl.BlockSpec((CONV_W, tf), lambda i, j: (0, j)),
                  pl.BlockSpec((CONV_W, tf), lambda i, j: (0, nj + j)),
                  pl.BlockSpec((1, tf), lambda i, j: (0, j)),
                  pl.BlockSpec((1, tf), lambda i, j: (0, nj + j)),
                  pl.BlockSpec((tf, d), lambda i, j: (j, 0)), fullvec, fullvec],
        out_specs=pl.BlockSpec((tm, d), lambda i, j: (i, 0)),
        out_shape=jax.ShapeDtypeStruct((t, d), F32),
        scratch_shapes=[pltpu.VMEM((tm, d), MXU_DTYPE),
                        pltpu.VMEM((tm + _TAIL, tf), F32),
                        pltpu.VMEM((tm + _TAIL, tf), F32),
                        pltpu.VMEM((nj, _TAIL, 2 * tf), F32)],
        compiler_params=_params("arbitrary", "arbitrary"),
        name="ffn_ln",
    )(x2, w_up, w_up, conv_w, conv_w, conv_b, conv_b, w_down, ln_g, ln_b)


def _pack_moves():
    main_dst = {'a_q': _COL_A_Q, 'a_k': _COL_A_K, 'a_v': _COL_A_V,
                'c_q': _COL_C_Q, 'c_k': _COL_C_K, 'c_v': _COL_C_V, 'b_q': _COL_B_Q,
                'b_k_slc': _COL_B_KS, 'b_v_slc': _COL_B_VS, 'b_k_win': _COL_B_KW,
                'b_v_win': _COL_B_VW}
    moves = [(0, blk * HEAD_DIM, _SPLIT[n][0], _SPLIT[n][1] - _SPLIT[n][0])
             for n, blk in main_dst.items()]
    moves += [(1, _SMALL_KC * LANES, _SPLIT['b_k_cmp'][0], LANES),
              (1, _SMALL_VC * LANES, _SPLIT['b_v_cmp'][0], LANES),
              (1, _SMALL_GATE * LANES, _GATE_COL0, LANES),
              (1, _SMALL_FORGET * LANES, _FORGET_COL0, N_IN - _FORGET_COL0)]
    return moves


def _pack_kernel(w_ref, main_ref, small_ref):
    dst = (main_ref, small_ref)
    small_ref[0, :, _SMALL_FORGET * LANES:] = jnp.zeros(
        (small_ref.shape[1], small_ref.shape[2] - _SMALL_FORGET * LANES), small_ref.dtype)
    for which, d0, s0, width in _pack_moves():
        dst[which][0, :, d0:d0 + width] = w_ref[0, :, s0:s0 + width].astype(dst[which].dtype)


def _pack_w_in(w_in, tk):
    depth, d, n_in = w_in.shape
    assert n_in == N_IN
    n_main, n_small = _N_COL_BLOCKS * HEAD_DIM, _SMALL_BLOCKS * LANES
    return pl.pallas_call(
        _pack_kernel,
        grid=(depth, d // tk),
        in_specs=[pl.BlockSpec((1, tk, n_in), lambda l, i: (l, i, 0))],
        out_specs=[pl.BlockSpec((1, tk, n_main), lambda l, i: (l, i, 0)),
                   pl.BlockSpec((1, tk, n_small), lambda l, i: (l, i, 0))],
        out_shape=[jax.ShapeDtypeStruct((depth, d, n_main), MXU_DTYPE),
                   jax.ShapeDtypeStruct((depth, d, n_small), MXU_DTYPE)],
        compiler_params=_params("arbitrary", "arbitrary"),
        name="pack_w_in",
    )(w_in)


def kernel(x, rel_bias, w_in, b_f, b_nsa_gate, cmp_pe, cmp_w1, cmp_b1, cmp_w2, cmp_b2, w_gate, b_gate, w_pa, w_pb, w_pc, w_out, ln1_g, ln1_b, w_up, conv_w, conv_b, w_down, ln2_g, ln2_b):
    bsz, seq, d = x.shape
    depth = w_in.shape[0]
    nblk = seq // BLK
    alpha = (2 * depth) ** 0.25
    t = bsz * seq
    tm = 1024
    bf = lambda a: a.astype(MXU_DTYPE)

    tiles = _bias_tiles(rel_bias, nblk)
    w_main, w_small = _pack_w_in(w_in, 256)
    x2 = x.reshape(t, d)
    for l in range(depth):
        u = _proj(x2, w_main[l], MXU_DTYPE, 1024, 512).reshape(bsz, seq, -1)
        u_small = _proj(x2, w_small[l], F32, 1024, w_small.shape[2]).reshape(bsz, seq, -1)

        o_a = _mixer_a(u, tiles["a"], nblk)
        kc, vct = _nsa_compress(u_small, cmp_pe[l], bf(cmp_w1[l]), cmp_b1[l], bf(cmp_w2[l]),
                                cmp_b2[l])
        o_b = _mixer_b(u, u_small, kc, vct, b_nsa_gate[l], tiles, nblk)
        cdec = _fox_decay(u_small, b_f[l])
        o_c = _mixer_c(u, cdec)

        x1 = _merge_ln(x2, o_a.reshape(t, -1), o_b.reshape(t, -1), o_c.reshape(t, -1),
                       bf(w_gate[l]), b_gate[l][None], bf(w_pa[l]), bf(w_pb[l]), bf(w_pc[l]),
                       bf(w_out[l]), ln1_g[l][None], ln1_b[l][None], alpha, tm, 256)
        x2 = _ffn_ln(x1, bf(w_up[l]), conv_w[l], conv_b[l][None], bf(w_down[l]),
                     ln2_g[l][None], ln2_b[l][None], alpha, seq, tm, 256)
    return x2.reshape(bsz, seq, d)
```

```python
import functools
import math

import numpy as np
import jax
import jax.numpy as jnp
from jax import lax
from jax.experimental import pallas as pl
from jax.experimental.pallas import tpu as pltpu

HEAD_DIM = 128
A_GROUPS = ((128, 1), (512, 4), (2048, 16))
A_HEADS_PER_GROUP = 2
A_HEADS = A_HEADS_PER_GROUP * len(A_GROUPS)
B_HEADS = 4
CMP_LEN = 32
CMP_STRIDE = 16
SLC_LEN = 64
N_SELECT = 16
WIN = 512
C_HEADS = 6
N_BUCKETS = 32
MAX_DISTANCE = 2048
CONV_W = 3
LN_EPS = 1e-5
NEG_INF = -1e30
FORCE_SCORE = 1e9
ATTN_SCALE = HEAD_DIM ** -0.5

LANES = 128
VMEM_LIMIT_BYTES = 56 * 2 ** 20

MXU_DTYPE = jnp.bfloat16
F32 = jnp.float32
BLK = LANES

_COL_A_Q, _COL_A_K, _COL_A_V = 0, 6, 12
_COL_C_Q, _COL_C_K, _COL_C_V = 18, 24, 30
_COL_B_Q = 36
_COL_B_KS, _COL_B_VS, _COL_B_KW, _COL_B_VW = 40, 41, 42, 43
_N_COL_BLOCKS = 44


def _in_split_offsets():
    widths = (('a_q', A_HEADS * HEAD_DIM), ('a_k', A_HEADS * HEAD_DIM), ('a_v', A_HEADS * HEAD_DIM),
              ('b_q', B_HEADS * HEAD_DIM), ('b_k_cmp', HEAD_DIM), ('b_v_cmp', HEAD_DIM),
              ('b_k_slc', HEAD_DIM), ('b_v_slc', HEAD_DIM), ('b_k_win', HEAD_DIM),
              ('b_v_win', HEAD_DIM), ('b_gate', 3 * B_HEADS),
              ('c_q', C_HEADS * HEAD_DIM), ('c_k', C_HEADS * HEAD_DIM), ('c_v', C_HEADS * HEAD_DIM),
              ('c_f', C_HEADS))
    out, o = {}, 0
    for name, w in widths:
        out[name] = (o, o + w)
        o += w
    return out, o


_SPLIT, N_IN = _in_split_offsets()
_SMALL_KC, _SMALL_VC, _SMALL_GATE, _SMALL_FORGET = 0, 1, 2, 3
_N_MAIN = _N_COL_BLOCKS * HEAD_DIM
_N_SMALL = 4 * LANES
_GATE_COL0 = _SPLIT['b_gate'][0] // LANES * LANES
_GATE_LANE0 = _SPLIT['b_gate'][0] - _GATE_COL0
_FORGET_COL0 = _SPLIT['c_f'][0] // LANES * LANES
_FORGET_LANE0 = _SPLIT['c_f'][0] - _FORGET_COL0
assert _SPLIT['b_gate'][1] - _GATE_COL0 <= LANES and N_IN - _FORGET_COL0 <= LANES
_FORGET_ROW0 = _FORGET_LANE0 // 8 * 8
_FORGET_ROWS = -(-(_FORGET_LANE0 + C_HEADS - _FORGET_ROW0) // 8) * 8


def _params(*sem):
    return pltpu.CompilerParams(dimension_semantics=sem, vmem_limit_bytes=VMEM_LIMIT_BYTES)


def _dot(a, b):
    return jnp.dot(a, b, preferred_element_type=F32)


def _dot_nt(a, b):
    return lax.dot_general(a, b, (((1,), (1,)), ((), ())), preferred_element_type=F32)


def _bias_tile_kernel(base_ref, rmul_ref, cmul_ref, dmask_ref, dmax_ref, rmax_ref, head_ref,
                      rel_ref, out_ref):
    row = lax.broadcasted_iota(jnp.int32, (BLK, BLK), 0)
    col = lax.broadcasted_iota(jnp.int32, (BLK, BLK), 1)
    max_exact = N_BUCKETS // 2
    per_step = out_ref.shape[0]
    for k in range(per_step):
        t = pl.program_id(0) * per_step + k
        dist = base_ref[t] + rmul_ref[t] * row + cmul_ref[t] * col
        valid = ((dist >= 0) & (dist <= dmax_ref[t]) & ((dist & dmask_ref[t]) == 0)
                 & (row < rmax_ref[t]))
        d = jnp.maximum(dist, 0)
        log_ratio = (jnp.log(jnp.maximum(d, 1).astype(F32) / max_exact)
                     / math.log(MAX_DISTANCE / max_exact))
        large = jnp.minimum(max_exact + (log_ratio * (N_BUCKETS - max_exact)).astype(jnp.int32),
                            N_BUCKETS - 1)
        bucket = jnp.where(d < max_exact, d, large)
        h = head_ref[t]
        val = jnp.zeros((BLK, BLK), F32)
        for b in range(N_BUCKETS):
            val = jnp.where(bucket == b, rel_ref[b, h], val)
        out_ref[k] = jnp.where(valid, val, NEG_INF)


def _a_tile_counts(nblk):
    return [min(window // BLK + 1, nblk) for window, _ in A_GROUPS]


def _tile_tables(nblk):
    big = 2 ** 30
    fam = {}
    fam["a"] = [(dl * BLK, 1, -1, dil - 1, window if dl < cnt else -1, BLK,
                 g * A_HEADS_PER_GROUP + hh)
                for g, ((window, dil), cnt) in enumerate(zip(A_GROUPS, _a_tile_counts(nblk)))
                for hh in range(A_HEADS_PER_GROUP)
                for dl in range(cnt + 1)]
    fam["slc"] = [(dl * BLK, -1, 1, 0, big if dl < nblk else -1, BLK, A_HEADS + h)
                  for dl in range(nblk + 1) for h in range(B_HEADS)]
    n_win = WIN // BLK + 1
    fam["win"] = [(dl * BLK, -1, 1, 0, WIN - 1 if dl < n_win else -1, BLK, A_HEADS + h)
                  for dl in range(n_win + 1) for h in range(B_HEADS)]
    n_cmp = (nblk * BLK - CMP_LEN) // CMP_STRIDE + 1
    fam["cmp"] = [(i * BLK - (CMP_LEN - 1), -CMP_STRIDE, 1, 0, big, n_cmp, A_HEADS + h)
                  for i in range(nblk) for h in range(B_HEADS)]
    return {k: [np.asarray(c, np.int32) for c in zip(*rows)] for k, rows in fam.items()}


def _a_tile_offsets(nblk):
    cnt = _a_tile_counts(nblk)
    return [A_HEADS_PER_GROUP * sum(c + 1 for c in cnt[:g]) for g in range(len(A_GROUPS))]


_MAX_TILES_PER_STEP = 17


def _bias_tiles(rel_bias, nblk):
    out = {}
    for name, cols in _tile_tables(nblk).items():
        n = cols[0].shape[0]
        per_step = max(k for k in range(1, _MAX_TILES_PER_STEP + 1) if n % k == 0)
        out[name] = pl.pallas_call(
            _bias_tile_kernel,
            grid_spec=pltpu.PrefetchScalarGridSpec(
                num_scalar_prefetch=len(cols),
                grid=(n // per_step,),
                in_specs=[pl.BlockSpec(memory_space=pltpu.SMEM)],
                out_specs=pl.BlockSpec((per_step, BLK, BLK), lambda t, *_: (t, 0, 0)),
            ),
            out_shape=jax.ShapeDtypeStruct((n, BLK, BLK), F32),
            compiler_params=_params("arbitrary"),
            name="bias_tiles_" + name,
        )(*[jnp.asarray(c) for c in cols], rel_bias)
    return out


def _proj_kernel(x_ref, w_ref, o_ref, os_ref, xb_ref):
    j = pl.program_id(1)
    n_main = pl.num_programs(1) - 1

    @pl.when(j == 0)
    def _():
        xb_ref[...] = x_ref[...].astype(MXU_DTYPE)

    @pl.when(j < n_main)
    def _():
        o_ref[...] = _dot(xb_ref[...], w_ref[...]).astype(o_ref.dtype)

    @pl.when(j == n_main)
    def _():
        os_ref[...] = _dot(xb_ref[...], w_ref[...])


def _proj(x2, w_packed, layer, tm, tn):
    t, d = x2.shape
    assert tn == _N_SMALL and _N_MAIN % tn == 0
    nj = _N_MAIN // tn
    return pl.pallas_call(
        _proj_kernel,
        grid=(t // tm, nj + 1),
        in_specs=[pl.BlockSpec((tm, d), lambda i, j: (i, 0)),
                  pl.BlockSpec((None, d, tn), lambda i, j: (layer, 0, j))],
        out_specs=[pl.BlockSpec((tm, tn), lambda i, j: (i, jnp.minimum(j, nj - 1))),
                   pl.BlockSpec((tm, tn), lambda i, j: (i, 0))],
        out_shape=[jax.ShapeDtypeStruct((t, _N_MAIN), MXU_DTYPE),
                   jax.ShapeDtypeStruct((t, _N_SMALL), F32)],
        scratch_shapes=[pltpu.VMEM((tm, d), MXU_DTYPE)],
        compiler_params=_params("arbitrary", "arbitrary"),
        name="in_proj",
    )(x2, w_packed)


def _lane_cat(xs):
    return jnp.concatenate(xs, axis=1)


def _softmax_pv(s, v, axis):
    m = jnp.max(s, axis=axis, keepdims=True)
    p = jnp.exp(s - m)
    l = jnp.sum(p, axis=axis, keepdims=True)
    if axis == 1:
        o = _dot(p.astype(MXU_DTYPE), v)
    else:
        o = _dot(v, p.astype(MXU_DTYPE))
    return o / l, m, l


_CAUSAL_STEP = 4


def _mixer_a_kernel(q_ref, k_ref, v_ref, t_ref, o_ref, *, tile_off, tile_cnt):
    i = pl.program_id(1)
    hpg = A_HEADS_PER_GROUP
    outs = [[None] * len(A_GROUPS) for _ in range(hpg)]
    lses = [[None] * len(A_GROUPS) for _ in range(hpg)]
    nblk = k_ref.shape[1] // BLK

    def group(g, n_keys_blk):
        nd = tile_cnt[g]
        kb0 = jnp.maximum(i - (n_keys_blk - 1), 0) if n_keys_blk < nblk else 0
        start = kb0 * BLK if isinstance(kb0, int) else pl.multiple_of(kb0 * BLK, BLK)
        res = []
        for hh in range(hpg):
            c0 = (g * hpg + hh) * HEAD_DIM
            q = q_ref[0, :, c0:c0 + HEAD_DIM]
            k = k_ref[0, pl.ds(start, n_keys_blk * BLK), c0:c0 + HEAD_DIM]
            v = v_ref[0, pl.ds(start, n_keys_blk * BLK), c0:c0 + HEAD_DIM]
            tbase = tile_off[g] + hh * (nd + 1)
            deltas = [i - kb0 - j for j in range(n_keys_blk)]
            bias = _lane_cat([t_ref[tbase + jnp.where(dl >= 0, dl, nd)] for dl in deltas])
            s = _dot_nt(q, k) * ATTN_SCALE + bias
            o, m, l = _softmax_pv(s, v, axis=1)
            res += [o, m + jnp.log(l)]
        return tuple(res)

    for g in range(len(A_GROUPS)):
        nd = tile_cnt[g]
        if nd == nblk and nblk % _CAUSAL_STEP == 0:
            res = lax.switch(i // _CAUSAL_STEP,
                             [functools.partial(group, g, (v + 1) * _CAUSAL_STEP)
                              for v in range(nblk // _CAUSAL_STEP)])
        else:
            res = group(g, nd)
        for hh in range(hpg):
            outs[hh][g], lses[hh][g] = res[2 * hh], res[2 * hh + 1]
    for hh in range(hpg):
        mx = functools.reduce(jnp.maximum, lses[hh])
        es = [jnp.exp(x - mx) for x in lses[hh]]
        tot = functools.reduce(lambda a, b: a + b, es)
        o = functools.reduce(lambda a, b: a + b,
                             [(e / tot) * og for e, og in zip(es, outs[hh])])
        o_ref[0, :, hh * HEAD_DIM:(hh + 1) * HEAD_DIM] = o.astype(o_ref.dtype)


def _mixer_a(u, tiles_a, nblk):
    b, s, _ = u.shape
    w = A_HEADS * HEAD_DIM
    kern = functools.partial(_mixer_a_kernel, tile_off=tuple(_a_tile_offsets(nblk)),
                             tile_cnt=tuple(_a_tile_counts(nblk)))
    n_a = tiles_a.shape[0]
    return pl.pallas_call(
        kern,
        grid=(b, nblk),
        in_specs=[pl.BlockSpec((1, BLK, w), lambda bi, i: (bi, i, _COL_A_Q // A_HEADS)),
                  pl.BlockSpec((1, s, w), lambda bi, i: (bi, 0, _COL_A_K // A_HEADS)),
                  pl.BlockSpec((1, s, w), lambda bi, i: (bi, 0, _COL_A_V // A_HEADS)),
                  pl.BlockSpec((n_a, BLK, BLK), lambda bi, i: (0, 0, 0))],
        out_specs=pl.BlockSpec((1, BLK, A_HEADS_PER_GROUP * HEAD_DIM), lambda bi, i: (bi, i, 0)),
        out_shape=jax.ShapeDtypeStruct((b, s, A_HEADS_PER_GROUP * HEAD_DIM), MXU_DTYPE),
        compiler_params=_params("arbitrary", "arbitrary"),
        name="mixer_a",
    )(u, u, u, tiles_a)


_C_TK = 256


def _split3(x):
    hi = x.astype(MXU_DTYPE)
    r1 = x - hi.astype(F32)
    mid = r1.astype(MXU_DTYPE)
    lo = (r1 - mid.astype(F32)).astype(MXU_DTYPE)
    return hi, mid, lo


def _fox_decay_kernel(f_ref, bf_ref, c_ref):
    x = f_ref[0] + bf_ref[...]
    logf = jnp.minimum(x, 0.0) - jnp.log1p(jnp.exp(-jnp.abs(x)))
    logf_t = logf.T[_FORGET_ROW0:_FORGET_ROW0 + _FORGET_ROWS]
    nk = c_ref.shape[1]
    r = lax.broadcasted_iota(jnp.int32, (_C_TK, _C_TK), 0)
    c = lax.broadcasted_iota(jnp.int32, (_C_TK, _C_TK), 1)
    upper = (r <= c).astype(MXU_DTYPE)
    carry = jnp.zeros((_FORGET_ROWS, 1), F32)
    for j in range(nk):
        hi, mid, lo = _split3(logf_t[:, j * _C_TK:(j + 1) * _C_TK])
        cs = (_dot(hi, upper) + _dot(mid, upper)) + _dot(lo, upper) + carry
        c_ref[0, j] = cs
        carry = cs[:, _C_TK - 1:_C_TK]


def _fox_decay(u_small, b_f):
    b, s, _ = u_small.shape
    nk = s // _C_TK
    bf_pad = jnp.zeros((1, LANES), F32).at[0, _FORGET_LANE0:_FORGET_LANE0 + C_HEADS].set(b_f)
    return pl.pallas_call(
        _fox_decay_kernel,
        grid=(b,),
        in_specs=[pl.BlockSpec((1, s, LANES), lambda bi: (bi, 0, _SMALL_FORGET)),
                  pl.BlockSpec((1, LANES), lambda bi: (0, 0))],
        out_specs=pl.BlockSpec((1, nk, _FORGET_ROWS, _C_TK), lambda bi: (bi, 0, 0, 0)),
        out_shape=jax.ShapeDtypeStruct((b, nk, _FORGET_ROWS, _C_TK), F32),
        compiler_params=_params("arbitrary"),
        name="fox_decay",
    )(u_small, bf_pad)


_C_TQ = 512


def _mixer_c_kernel(q_ref, k_ref, v_ref, c_ref, o_ref):
    row = _FORGET_LANE0 - _FORGET_ROW0 + pl.program_id(1)
    tq = _C_TQ
    r = lax.broadcasted_iota(jnp.int32, (tq, tq), 0)
    c = lax.broadcasted_iota(jnp.int32, (tq, tq), 1)
    for i in range(q_ref.shape[1] // tq):
        nk = (i + 1) * tq
        q = q_ref[0, i * tq:(i + 1) * tq, :]
        decay = _lane_cat([c_ref[0, j, pl.ds(row, 1), :] for j in range(nk // _C_TK)])
        s = _dot_nt(q, k_ref[0, :nk, :]) * ATTN_SCALE - decay
        diag = jnp.where(c <= r, s[:, nk - tq:], NEG_INF)
        s = diag if i == 0 else _lane_cat([s[:, :nk - tq], diag])
        o, _, _ = _softmax_pv(s, v_ref[0, :nk, :], axis=1)
        o_ref[0, i * tq:(i + 1) * tq, :] = o.astype(o_ref.dtype)


def _mixer_c(u, cdec):
    b, s, _ = u.shape
    nk = s // _C_TK
    head = lambda col: pl.BlockSpec((1, s, HEAD_DIM), lambda bi, h: (bi, 0, col + h))
    return pl.pallas_call(
        _mixer_c_kernel,
        grid=(b, C_HEADS),
        in_specs=[head(_COL_C_Q), head(_COL_C_K), head(_COL_C_V),
                  pl.BlockSpec((1, nk, _FORGET_ROWS, _C_TK), lambda bi, h: (bi, 0, 0, 0))],
        out_specs=head(0),
        out_shape=jax.ShapeDtypeStruct((b, s, C_HEADS * HEAD_DIM), MXU_DTYPE),
        compiler_params=_params("arbitrary", "arbitrary"),
        name="mixer_c",
    )(u, u, u, cdec)


def _gelu(x):
    return 0.5 * x * (1.0 + lax.erf(x * np.float32(math.sqrt(0.5))))


def _nsa_compress_kernel(k_ref, v_ref, pe_ref, w1_ref, b1_ref, w2_ref, b2_ref, kc_ref, vct_ref):
    n_chunk = k_ref.shape[1] // CMP_STRIDE
    halves = CMP_LEN // CMP_STRIDE
    res = []
    for c, src in enumerate((k_ref, v_ref)):
        parts = []
        for half in range(halves):
            acc = jnp.zeros((n_chunk, HEAD_DIM), F32)
            for p in range(CMP_STRIDE):
                pos = half * CMP_STRIDE + p
                rows = src[0, pl.ds(p, n_chunk, stride=CMP_STRIDE), :] + pe_ref[c, pos:pos + 1, :]
                acc = acc + _dot(rows.astype(MXU_DTYPE), w1_ref[c, pos])
            parts.append(acc)
        hid = parts[0]
        for half in range(1, halves):
            hid = hid + pltpu.roll(parts[half], n_chunk - half, 0)
        hid = _gelu(hid + b1_ref[c:c + 1, :])
        res.append(_dot(hid.astype(MXU_DTYPE), w2_ref[c]) + b2_ref[c:c + 1, :])
    kc_ref[0] = res[0].astype(kc_ref.dtype)
    vct_ref[0] = res[1].T.astype(vct_ref.dtype)


def _nsa_compress(u_small, cmp_pe, cmp_w1, cmp_b1, cmp_w2, cmp_b2, layer):
    b, s, _ = u_small.shape
    n_chunk = s // CMP_STRIDE
    of_layer = lambda a: pl.BlockSpec((None,) + a.shape[1:],
                                      lambda bi: (layer,) + (0,) * (a.ndim - 1))
    return pl.pallas_call(
        _nsa_compress_kernel,
        grid=(b,),
        in_specs=[pl.BlockSpec((1, s, LANES), lambda bi: (bi, 0, _SMALL_KC)),
                  pl.BlockSpec((1, s, LANES), lambda bi: (bi, 0, _SMALL_VC)),
                  of_layer(cmp_pe), of_layer(cmp_w1), of_layer(cmp_b1),
                  of_layer(cmp_w2), of_layer(cmp_b2)],
        out_specs=[pl.BlockSpec((1, n_chunk, HEAD_DIM), lambda bi: (bi, 0, 0)),
                   pl.BlockSpec((1, HEAD_DIM, n_chunk), lambda bi: (bi, 0, 0))],
        out_shape=[jax.ShapeDtypeStruct((b, n_chunk, HEAD_DIM), MXU_DTYPE),
                   jax.ShapeDtypeStruct((b, HEAD_DIM, n_chunk), MXU_DTYPE)],
        compiler_params=_params("arbitrary"),
        name="nsa_compress",
    )(u_small, u_small, cmp_pe, cmp_w1, cmp_b1, cmp_w2, cmp_b2)


def _mixer_b_kernel(q_ref, ks_ref, vs_ref, kw_ref, vw_ref, kc_ref, vct_ref, gl_ref, bg_ref,
                    tslc_ref, twin_ref, tcmp_ref, o_ref, vst_ref, vwt_ref):
    i = pl.program_id(1)
    nblk = vwt_ref.shape[0]
    nh = B_HEADS
    n_slc = ks_ref.shape[1] // SLC_LEN

    @pl.when(i == 0)
    def _():
        for kb in range(nblk):
            rows = slice(kb * BLK, (kb + 1) * BLK)
            vst_ref[:, rows] = vs_ref[0, rows, :].astype(F32).T.astype(MXU_DTYPE)
            vwt_ref[kb] = vw_ref[0, rows, :].astype(F32).T.astype(MXU_DTYPE)

    q4 = jnp.concatenate([q_ref[0, :, h * HEAD_DIM:(h + 1) * HEAD_DIM] for h in range(nh)], axis=0)

    st = _dot_nt(kc_ref[0], q4) * ATTN_SCALE + _lane_cat([tcmp_ref[h] for h in range(nh)])
    m = jnp.max(st, axis=0, keepdims=True)
    e = jnp.exp(st - m)
    l = jnp.sum(e, axis=0, keepdims=True)
    pc = jnp.where(m > 0.5 * NEG_INF, e / l, 0.0).astype(MXU_DTYPE)
    o_cmp = _dot(vct_ref[0], pc)
    n_cmp_pad = kc_ref.shape[1]
    jj = lax.broadcasted_iota(jnp.int32, (n_slc, n_cmp_pad), 0)
    nn = lax.broadcasted_iota(jnp.int32, (n_slc, n_cmp_pad), 1)
    overlap = ((nn * CMP_STRIDE <= jj * SLC_LEN + SLC_LEN - 1)
               & (nn * CMP_STRIDE + CMP_LEN - 1 >= jj * SLC_LEN)).astype(MXU_DTYPE)
    imp4 = _dot(overlap, pc)
    imp = functools.reduce(lambda a, b: a + b,
                           [imp4[:, h * BLK:(h + 1) * BLK] for h in range(nh)])
    t = i * BLK + lax.broadcasted_iota(jnp.int32, (n_slc, BLK), 1)
    blk = lax.broadcasted_iota(jnp.int32, (n_slc, BLK), 0)
    cur = t // SLC_LEN
    forced = (blk == 0) | (blk == cur) | (blk == cur - 1)
    causal = blk * SLC_LEN <= t
    score = jnp.where(forced, FORCE_SCORE, jnp.where(causal, imp, NEG_INF))
    rank = jnp.zeros((n_slc, BLK), jnp.int32)
    for r in range(n_slc):
        row = score[r:r + 1, :]
        ahead = (row > score) | ((row == score) & (r < blk))
        rank = rank + ahead.astype(jnp.int32)
    add = jnp.where(rank < min(N_SELECT, n_slc), 0.0, NEG_INF).astype(F32)
    add4 = _lane_cat([add] * nh)

    def bias_rows(t_ref, dl, n_real):
        idx = jnp.where(dl >= 0, dl, n_real)
        return _lane_cat([t_ref[idx * nh + h] for h in range(nh)])

    per_blk = BLK // SLC_LEN

    def selected(n_keys_blk):
        rows = []
        for kb in range(n_keys_blk):
            sel = jnp.concatenate(
                [jnp.broadcast_to(add4[kb * per_blk + j:kb * per_blk + j + 1, :],
                                  (SLC_LEN, nh * BLK)) for j in range(per_blk)], axis=0)
            rows.append(bias_rows(tslc_ref, i - kb, nblk) + sel)
        n_keys = n_keys_blk * BLK
        s = _dot_nt(ks_ref[0, :n_keys, :], q4) * ATTN_SCALE + jnp.concatenate(rows, axis=0)
        return _softmax_pv(s, vst_ref[:, :n_keys], axis=0)[0]

    if nblk % _CAUSAL_STEP == 0:
        o_slc = lax.switch(i // _CAUSAL_STEP,
                           [functools.partial(selected, (v + 1) * _CAUSAL_STEP)
                            for v in range(nblk // _CAUSAL_STEP)])
    else:
        o_slc = selected(nblk)

    n_win = twin_ref.shape[0] // nh - 1
    kb0 = jnp.maximum(i - (n_win - 1), 0)
    start = pl.multiple_of(kb0 * BLK, BLK)
    bias = jnp.concatenate([bias_rows(twin_ref, i - kb0 - j, n_win) for j in range(n_win)],
                           axis=0)
    s = _dot_nt(kw_ref[0, pl.ds(start, n_win * BLK), :], q4) * ATTN_SCALE + bias
    o_win, _, _ = _softmax_pv(s, _lane_cat([vwt_ref[kb0 + j] for j in range(n_win)]), axis=0)

    gt = jax.nn.sigmoid(gl_ref[0] + bg_ref[...]).T

    def gate(br):
        r0 = _GATE_LANE0 + br * nh
        return _lane_cat([gt[r0 + h:r0 + h + 1, :] for h in range(nh)])

    o_t = gate(0) * o_cmp + gate(1) * o_slc + gate(2) * o_win
    for h in range(nh):
        o_ref[0, :, h * HEAD_DIM:(h + 1) * HEAD_DIM] = (
            o_t[:, h * BLK:(h + 1) * BLK].T.astype(o_ref.dtype))


def _mixer_b(u, u_small, kc, vct, b_nsa_gate, tiles, nblk):
    b, s, _ = u.shape
    nh = B_HEADS
    bg_pad = jnp.zeros((1, LANES), F32).at[0, _GATE_LANE0:_GATE_LANE0 + 3 * nh].set(b_nsa_gate)
    whole = lambda a: pl.BlockSpec(a.shape, lambda bi, i: (0, 0, 0))
    kv = lambda col: pl.BlockSpec((1, s, HEAD_DIM), lambda bi, i: (bi, 0, col))
    n_chunk = kc.shape[1]
    return pl.pallas_call(
        _mixer_b_kernel,
        grid=(b, nblk),
        in_specs=[pl.BlockSpec((1, BLK, nh * HEAD_DIM), lambda bi, i: (bi, i, _COL_B_Q // nh)),
                  kv(_COL_B_KS), kv(_COL_B_VS), kv(_COL_B_KW), kv(_COL_B_VW),
                  pl.BlockSpec((1, n_chunk, HEAD_DIM), lambda bi, i: (bi, 0, 0)),
                  pl.BlockSpec((1, HEAD_DIM, n_chunk), lambda bi, i: (bi, 0, 0)),
                  pl.BlockSpec((1, BLK, LANES), lambda bi, i: (bi, i, _SMALL_GATE)),
                  pl.BlockSpec((1, LANES), lambda bi, i: (0, 0)),
                  whole(tiles["slc"]), whole(tiles["win"]),
                  pl.BlockSpec((nh, BLK, BLK), lambda bi, i: (i, 0, 0))],
        out_specs=pl.BlockSpec((1, BLK, nh * HEAD_DIM), lambda bi, i: (bi, i, 0)),
        out_shape=jax.ShapeDtypeStruct((b, s, nh * HEAD_DIM), MXU_DTYPE),
        scratch_shapes=[pltpu.VMEM((HEAD_DIM, s), MXU_DTYPE),
                        pltpu.VMEM((nblk, HEAD_DIM, BLK), MXU_DTYPE)],
        compiler_params=_params("arbitrary", "arbitrary"),
        name="mixer_b",
    )(u, u, u, u, u, kc, vct, u_small, bg_pad, tiles["slc"], tiles["win"], tiles["cmp"])


def _layer_norm(y, g, b):
    mu = jnp.mean(y, axis=-1, keepdims=True)
    yc = y - mu
    var = jnp.mean(yc * yc, axis=-1, keepdims=True)
    return yc * lax.rsqrt(var + LN_EPS) * g + b


def _merge_kernel(x_ref, oa_ref, ob_ref, oc_ref, wga_ref, wgb_ref, wgc_ref, bga_ref, bgb_ref,
                  bgc_ref, wpa_ref, wpb_ref, wpc_ref, wo_ref, g_ref, b_ref, y_ref, xb_ref, *,
                  alpha):
    j = pl.program_id(1)

    @pl.when(j == 0)
    def _():
        xb_ref[...] = x_ref[...].astype(MXU_DTYPE)
        y_ref[...] = jnp.zeros_like(y_ref)

    xb = xb_ref[...]
    mixed = None
    for o_ref, wg_ref, bg_ref, wp_ref in ((oa_ref, wga_ref, bga_ref, wpa_ref),
                                          (ob_ref, wgb_ref, bgb_ref, wpb_ref),
                                          (oc_ref, wgc_ref, bgc_ref, wpc_ref)):
        gate = jax.nn.sigmoid(_dot(xb, wg_ref[...]) + bg_ref[...])
        term = gate * _dot(o_ref[...], wp_ref[...])
        mixed = term if mixed is None else mixed + term
    y_ref[...] += _dot(mixed.astype(MXU_DTYPE), wo_ref[...])

    @pl.when(j == pl.num_programs(1) - 1)
    def _():
        y_ref[...] = _layer_norm(alpha * x_ref[...] + y_ref[...], g_ref[...], b_ref[...])


def _merge_ln(x2, oa, ob, oc, w_gate, b_gate, w_pa, w_pb, w_pc, w_out, ln_g, ln_b, layer, alpha,
              tm, tn):
    t, d = x2.shape
    nj = d // tn
    row = lambda k: pl.BlockSpec((tm, k), lambda i, j: (i, 0))
    colw = lambda k, shift: pl.BlockSpec((None, k, tn), lambda i, j: (layer, 0, shift * nj + j))
    vec = lambda shift: pl.BlockSpec((None, 1, tn), lambda i, j: (layer, 0, shift * nj + j))
    fullvec = pl.BlockSpec((None, 1, d), lambda i, j: (layer, 0, 0))
    return pl.pallas_call(
        functools.partial(_merge_kernel, alpha=alpha),
        grid=(t // tm, nj),
        in_specs=[row(d), row(oa.shape[1]), row(ob.shape[1]), row(oc.shape[1]),
                  colw(d, 0), colw(d, 1), colw(d, 2), vec(0), vec(1), vec(2),
                  colw(w_pa.shape[1], 0), colw(w_pb.shape[1], 0), colw(w_pc.shape[1], 0),
                  pl.BlockSpec((None, tn, d), lambda i, j: (layer, j, 0)), fullvec, fullvec],
        out_specs=pl.BlockSpec((tm, d), lambda i, j: (i, 0)),
        out_shape=jax.ShapeDtypeStruct((t, d), F32),
        scratch_shapes=[pltpu.VMEM((tm, d), MXU_DTYPE)],
        compiler_params=_params("arbitrary", "arbitrary"),
        name="merge_ln",
    )(x2, oa, ob, oc, w_gate, w_gate, w_gate, b_gate, b_gate, b_gate, w_pa, w_pb, w_pc, w_out,
      ln_g, ln_b)


_TAIL = 8


def _ffn_kernel(x_ref, wa_ref, wb_ref, cwa_ref, cwb_ref, cba_ref, cbb_ref, wd_ref, g_ref, b_ref,
                y_ref, xb_ref, ha_ref, hb_ref, tail_ref, *, alpha, tiles_per_seq):
    i = pl.program_id(0)
    j = pl.program_id(1)
    tm = x_ref.shape[0]
    tf = wa_ref.shape[1]

    @pl.when(j == 0)
    def _():
        xb_ref[...] = x_ref[...].astype(MXU_DTYPE)
        y_ref[...] = jnp.zeros_like(y_ref)

    @pl.when(i % tiles_per_seq == 0)
    def _():
        tail_ref[j] = jnp.zeros(tail_ref.shape[1:], F32)

    xb = xb_ref[...]
    for h_ref, w_ref, c0 in ((ha_ref, wa_ref, 0), (hb_ref, wb_ref, tf)):
        h_ref[0:_TAIL, :] = tail_ref[j, :, c0:c0 + tf]
        h_ref[_TAIL:, :] = _dot(xb, w_ref[...])
        tail_ref[j, :, c0:c0 + tf] = h_ref[tm:tm + _TAIL, :]

    def conv(h_ref, cw_ref, cb_ref):
        out = cb_ref[...]
        for tap in range(CONV_W):
            shift = CONV_W - 1 - tap
            out = out + cw_ref[tap:tap + 1, :] * h_ref[pl.ds(_TAIL - shift, tm), :]
        return out

    a = conv(ha_ref, cwa_ref, cba_ref)
    bb = conv(hb_ref, cwb_ref, cbb_ref)
    y_ref[...] += _dot((_gelu(a) * bb).astype(MXU_DTYPE), wd_ref[...])

    @pl.when(j == pl.num_programs(1) - 1)
    def _():
        y_ref[...] = _layer_norm(alpha * x_ref[...] + y_ref[...], g_ref[...], b_ref[...])


def _ffn_ln(x2, w_up, conv_w, conv_b, w_down, ln_g, ln_b, layer, alpha, seq, tm, tf):
    t, d = x2.shape
    f = w_down.shape[1]
    nj = f // tf
    assert seq % tm == 0 and CONV_W - 1 <= _TAIL
    half = lambda k, shift: pl.BlockSpec((None, k, tf), lambda i, j: (layer, 0, shift * nj + j))
    fullvec = pl.BlockSpec((None, 1, d), lambda i, j: (layer, 0, 0))
    return pl.pallas_call(
        functools.partial(_ffn_kernel, alpha=alpha, tiles_per_seq=seq // tm),
        grid=(t // tm, nj),
        in_specs=[pl.BlockSpec((tm, d), lambda i, j: (i, 0)),
                  half(d, 0), half(d, 1), half(CONV_W, 0), half(CONV_W, 1), half(1, 0), half(1, 1),
                  pl.BlockSpec((None, tf, d), lambda i, j: (layer, j, 0)), fullvec, fullvec],
        out_specs=pl.BlockSpec((tm, d), lambda i, j: (i, 0)),
        out_shape=jax.ShapeDtypeStruct((t, d), F32),
        scratch_shapes=[pltpu.VMEM((tm, d), MXU_DTYPE),
                        pltpu.VMEM((tm + _TAIL, tf), F32),
                        pltpu.VMEM((tm + _TAIL, tf), F32),
                        pltpu.VMEM((nj, _TAIL, 2 * tf), F32)],
        compiler_params=_params("arbitrary", "arbitrary"),
        name="ffn_ln",
    )(x2, w_up, w_up, conv_w, conv_w, conv_b, conv_b, w_down, ln_g, ln_b)


def _pack_moves():
    main_dst = {'a_q': _COL_A_Q, 'a_k': _COL_A_K, 'a_v': _COL_A_V,
                'c_q': _COL_C_Q, 'c_k': _COL_C_K, 'c_v': _COL_C_V, 'b_q': _COL_B_Q,
                'b_k_slc': _COL_B_KS, 'b_v_slc': _COL_B_VS, 'b_k_win': _COL_B_KW,
                'b_v_win': _COL_B_VW}
    moves = [(blk * HEAD_DIM, _SPLIT[n][0], _SPLIT[n][1] - _SPLIT[n][0])
             for n, blk in main_dst.items()]
    moves += [(_N_MAIN + _SMALL_KC * LANES, _SPLIT['b_k_cmp'][0], LANES),
              (_N_MAIN + _SMALL_VC * LANES, _SPLIT['b_v_cmp'][0], LANES),
              (_N_MAIN + _SMALL_GATE * LANES, _GATE_COL0, LANES),
              (_N_MAIN + _SMALL_FORGET * LANES, _FORGET_COL0, N_IN - _FORGET_COL0)]
    return moves


def _pack_kernel(w_ref, o_ref):
    tail0 = _N_MAIN + _SMALL_FORGET * LANES
    o_ref[0, :, tail0:] = jnp.zeros((o_ref.shape[1], o_ref.shape[2] - tail0), o_ref.dtype)
    for d0, s0, width in _pack_moves():
        o_ref[0, :, d0:d0 + width] = w_ref[0, :, s0:s0 + width].astype(o_ref.dtype)


def _pack_w_in(w_in, tk):
    depth, d, n_in = w_in.shape
    assert n_in == N_IN
    n_out = _N_MAIN + _N_SMALL
    return pl.pallas_call(
        _pack_kernel,
        grid=(depth, d // tk),
        in_specs=[pl.BlockSpec((1, tk, n_in), lambda l, i: (l, i, 0))],
        out_specs=pl.BlockSpec((1, tk, n_out), lambda l, i: (l, i, 0)),
        out_shape=jax.ShapeDtypeStruct((depth, d, n_out), MXU_DTYPE),
        compiler_params=_params("arbitrary", "arbitrary"),
        name="pack_w_in",
    )(w_in)


_TM_PROJ, _TN_PROJ = 1024, 512
_TM_MERGE, _TN_MERGE = 512, 512
_TM_FFN, _TF_FFN = 512, 512


def kernel(x, rel_bias, w_in, b_f, b_nsa_gate, cmp_pe, cmp_w1, cmp_b1, cmp_w2, cmp_b2, w_gate, b_gate, w_pa, w_pb, w_pc, w_out, ln1_g, ln1_b, w_up, conv_w, conv_b, w_down, ln2_g, ln2_b):
    bsz, seq, d = x.shape
    depth = w_in.shape[0]
    nblk = seq // BLK
    alpha = (2 * depth) ** 0.25
    t = bsz * seq
    bf = lambda a: a.astype(MXU_DTYPE)
    row = lambda a: a[:, None, :]

    tiles = _bias_tiles(rel_bias, nblk)
    w_in_p = _pack_w_in(w_in, 256)
    cmp_w1_b = bf(cmp_w1).reshape(depth, 2, CMP_LEN, HEAD_DIM, HEAD_DIM)
    cmp_w2_b = bf(cmp_w2)
    w_gate_b, w_pa_b, w_pb_b, w_pc_b, w_out_b = (bf(w_gate), bf(w_pa), bf(w_pb), bf(w_pc),
                                                 bf(w_out))
    w_up_b, w_down_b = bf(w_up), bf(w_down)
    x2 = x.reshape(t, d)
    for l in range(depth):
        u, u_small = _proj(x2, w_in_p, l, _TM_PROJ, _TN_PROJ)
        u = u.reshape(bsz, seq, -1)
        u_small = u_small.reshape(bsz, seq, -1)

        o_a = _mixer_a(u, tiles["a"], nblk)
        kc, vct = _nsa_compress(u_small, cmp_pe, cmp_w1_b, cmp_b1, cmp_w2_b, cmp_b2, l)
        o_b = _mixer_b(u, u_small, kc, vct, b_nsa_gate[l], tiles, nblk)
        cdec = _fox_decay(u_small, b_f[l])
        o_c = _mixer_c(u, cdec)

        x1 = _merge_ln(x2, o_a.reshape(t, -1), o_b.reshape(t, -1), o_c.reshape(t, -1),
                       w_gate_b, row(b_gate), w_pa_b, w_pb_b, w_pc_b, w_out_b,
                       row(ln1_g), row(ln1_b), l, alpha, _TM_MERGE, _TN_MERGE)
        x2 = _ffn_ln(x1, w_up_b, conv_w, row(conv_b), w_down_b, row(ln2_g), row(ln2_b),
                     l, alpha, seq, _TM_FFN, _TF_FFN)
    return x2.reshape(bsz, seq, d)
```

```python
import functools
import math

import numpy as np
import jax
import jax.numpy as jnp
from jax import lax
from jax.experimental import pallas as pl
from jax.experimental.pallas import tpu as pltpu

HEAD_DIM = 128
A_GROUPS = ((128, 1), (512, 4), (2048, 16))
A_HEADS_PER_GROUP = 2
A_HEADS = A_HEADS_PER_GROUP * len(A_GROUPS)
B_HEADS = 4
CMP_LEN = 32
CMP_STRIDE = 16
SLC_LEN = 64
N_SELECT = 16
WIN = 512
C_HEADS = 6
N_BUCKETS = 32
MAX_DISTANCE = 2048
CONV_W = 3
LN_EPS = 1e-5
NEG_INF = -1e30
FORCE_SCORE = 1e9
ATTN_SCALE = HEAD_DIM ** -0.5
_INV_SCALE = np.float32(1.0 / ATTN_SCALE)
_EXP2_SCALE = np.float32(ATTN_SCALE * math.log2(math.e))

LANES = 128
VMEM_LIMIT_BYTES = 56 * 2 ** 20

MXU_DTYPE = jnp.bfloat16
F32 = jnp.float32
BLK = LANES

_COL_A_Q, _COL_A_K, _COL_A_V = 0, 6, 12
_COL_C_Q, _COL_C_K, _COL_C_V = 18, 24, 30
_COL_B_Q = 36
_COL_B_KS, _COL_B_VS, _COL_B_KW, _COL_B_VW = 40, 41, 42, 43
_N_COL_BLOCKS = 44


def _in_split_offsets():
    widths = (('a_q', A_HEADS * HEAD_DIM), ('a_k', A_HEADS * HEAD_DIM), ('a_v', A_HEADS * HEAD_DIM),
              ('b_q', B_HEADS * HEAD_DIM), ('b_k_cmp', HEAD_DIM), ('b_v_cmp', HEAD_DIM),
              ('b_k_slc', HEAD_DIM), ('b_v_slc', HEAD_DIM), ('b_k_win', HEAD_DIM),
              ('b_v_win', HEAD_DIM), ('b_gate', 3 * B_HEADS),
              ('c_q', C_HEADS * HEAD_DIM), ('c_k', C_HEADS * HEAD_DIM), ('c_v', C_HEADS * HEAD_DIM),
              ('c_f', C_HEADS))
    out, o = {}, 0
    for name, w in widths:
        out[name] = (o, o + w)
        o += w
    return out, o


_SPLIT, N_IN = _in_split_offsets()
_SMALL_KC, _SMALL_VC, _SMALL_GATE, _SMALL_FORGET = 0, 1, 2, 3
_N_MAIN = _N_COL_BLOCKS * HEAD_DIM
_N_SMALL = 4 * LANES
_GATE_COL0 = _SPLIT['b_gate'][0] // LANES * LANES
_GATE_LANE0 = _SPLIT['b_gate'][0] - _GATE_COL0
_FORGET_COL0 = _SPLIT['c_f'][0] // LANES * LANES
_FORGET_LANE0 = _SPLIT['c_f'][0] - _FORGET_COL0
assert _SPLIT['b_gate'][1] - _GATE_COL0 <= LANES and N_IN - _FORGET_COL0 <= LANES
_FORGET_ROW0 = _FORGET_LANE0 // 8 * 8
_FORGET_ROWS = -(-(_FORGET_LANE0 + C_HEADS - _FORGET_ROW0) // 8) * 8


def _params(*sem):
    return pltpu.CompilerParams(dimension_semantics=sem, vmem_limit_bytes=VMEM_LIMIT_BYTES)


def _dot(a, b):
    return jnp.dot(a, b, preferred_element_type=F32)


def _dot_nt(a, b):
    return lax.dot_general(a, b, (((1,), (1,)), ((), ())), preferred_element_type=F32)


def _bias_tile_kernel(base_ref, rmul_ref, cmul_ref, dmask_ref, dmax_ref, rmax_ref, head_ref,
                      rel_ref, out_ref):
    row = lax.broadcasted_iota(jnp.int32, (BLK, BLK), 0)
    col = lax.broadcasted_iota(jnp.int32, (BLK, BLK), 1)
    max_exact = N_BUCKETS // 2
    per_step = out_ref.shape[0]
    for k in range(per_step):
        t = pl.program_id(0) * per_step + k
        dist = base_ref[t] + rmul_ref[t] * row + cmul_ref[t] * col
        valid = ((dist >= 0) & (dist <= dmax_ref[t]) & ((dist & dmask_ref[t]) == 0)
                 & (row < rmax_ref[t]))
        d = jnp.maximum(dist, 0)
        log_ratio = (jnp.log(jnp.maximum(d, 1).astype(F32) / max_exact)
                     / math.log(MAX_DISTANCE / max_exact))
        large = jnp.minimum(max_exact + (log_ratio * (N_BUCKETS - max_exact)).astype(jnp.int32),
                            N_BUCKETS - 1)
        bucket = jnp.where(d < max_exact, d, large)
        h = head_ref[t]
        val = jnp.zeros((BLK, BLK), F32)
        for b in range(N_BUCKETS):
            val = jnp.where(bucket == b, rel_ref[b, h], val)
        out_ref[k] = jnp.where(valid, val * _INV_SCALE, NEG_INF)


def _a_tile_counts(nblk):
    return [min(window // BLK + 1, nblk) for window, _ in A_GROUPS]


def _tile_tables(nblk):
    big = 2 ** 30
    fam = {}
    fam["a"] = [(dl * BLK, 1, -1, dil - 1, window if dl < cnt else -1, BLK,
                 g * A_HEADS_PER_GROUP + hh)
                for g, ((window, dil), cnt) in enumerate(zip(A_GROUPS, _a_tile_counts(nblk)))
                for hh in range(A_HEADS_PER_GROUP)
                for dl in range(cnt + 1)]
    fam["slc"] = [(dl * BLK, -1, 1, 0, big if dl < nblk else -1, BLK, A_HEADS + h)
                  for dl in range(nblk + 1) for h in range(B_HEADS)]
    n_win = WIN // BLK + 1
    fam["win"] = [(dl * BLK, -1, 1, 0, WIN - 1 if dl < n_win else -1, BLK, A_HEADS + h)
                  for dl in range(n_win + 1) for h in range(B_HEADS)]
    n_cmp = (nblk * BLK - CMP_LEN) // CMP_STRIDE + 1
    fam["cmp"] = [(i * BLK - (CMP_LEN - 1), -CMP_STRIDE, 1, 0, big, n_cmp, A_HEADS + h)
                  for i in range(nblk) for h in range(B_HEADS)]
    return {k: [np.asarray(c, np.int32) for c in zip(*rows)] for k, rows in fam.items()}


def _a_tile_offsets(nblk):
    cnt = _a_tile_counts(nblk)
    return [A_HEADS_PER_GROUP * sum(c + 1 for c in cnt[:g]) for g in range(len(A_GROUPS))]


_MAX_TILES_PER_STEP = 17


def _bias_tiles(rel_bias, nblk):
    out = {}
    for name, cols in _tile_tables(nblk).items():
        n = cols[0].shape[0]
        per_step = max(k for k in range(1, _MAX_TILES_PER_STEP + 1) if n % k == 0)
        out[name] = pl.pallas_call(
            _bias_tile_kernel,
            grid_spec=pltpu.PrefetchScalarGridSpec(
                num_scalar_prefetch=len(cols),
                grid=(n // per_step,),
                in_specs=[pl.BlockSpec(memory_space=pltpu.SMEM)],
                out_specs=pl.BlockSpec((per_step, BLK, BLK), lambda t, *_: (t, 0, 0)),
            ),
            out_shape=jax.ShapeDtypeStruct((n, BLK, BLK), F32),
            compiler_params=_params("arbitrary"),
            name="bias_tiles_" + name,
        )(*[jnp.asarray(c) for c in cols], rel_bias)
    return out


def _proj_kernel(x_ref, w_ref, o_ref, os_ref, xb_ref):
    j = pl.program_id(1)
    tn = w_ref.shape[1]
    n_side = os_ref.shape[1]

    @pl.when(j == 0)
    def _():
        xb_ref[...] = x_ref[...].astype(MXU_DTYPE)

    acc = _dot(xb_ref[...], w_ref[...])
    o_ref[...] = acc.astype(o_ref.dtype)

    @pl.when(j == pl.num_programs(1) - 1)
    def _():
        os_ref[...] = acc[:, tn - n_side:]


def _proj(x2, w_packed, layer, tm, tn):
    t, d = x2.shape
    n_all = _N_MAIN + _N_SMALL
    assert n_all % tn == 0 and tn >= _N_SMALL
    nj = n_all // tn
    return pl.pallas_call(
        _proj_kernel,
        grid=(t // tm, nj),
        in_specs=[pl.BlockSpec((tm, d), lambda i, j: (i, 0)),
                  pl.BlockSpec((None, d, tn), lambda i, j: (layer, 0, j))],
        out_specs=[pl.BlockSpec((tm, tn), lambda i, j: (i, j)),
                   pl.BlockSpec((tm, _N_SMALL), lambda i, j: (i, 0))],
        out_shape=[jax.ShapeDtypeStruct((t, _N_MAIN), MXU_DTYPE),
                   jax.ShapeDtypeStruct((t, _N_SMALL), F32)],
        scratch_shapes=[pltpu.VMEM((tm, d), MXU_DTYPE)],
        compiler_params=_params("arbitrary", "arbitrary"),
        name="in_proj",
    )(x2, w_packed)


def _lane_cat(xs):
    return jnp.concatenate(xs, axis=1)


def _softmax_pv(z, v, axis):
    m = jnp.max(z, axis=axis, keepdims=True)
    p = jnp.exp2((z - m) * _EXP2_SCALE)
    l = jnp.sum(p, axis=axis, keepdims=True)
    if axis == 1:
        o = _dot(p.astype(MXU_DTYPE), v)
    else:
        o = _dot(v, p.astype(MXU_DTYPE))
    return o / l, m, l


_CAUSAL_STEP = 4
_Q_PER_STEP = 2


def _mixer_a_kernel(q_ref, k_ref, v_ref, t_ref, o_ref, *, tile_off, tile_cnt):
    hpg = A_HEADS_PER_GROUP
    nblk = k_ref.shape[1] // BLK

    def group(i, rows, g, n_keys_blk):
        nd = tile_cnt[g]
        kb0 = jnp.maximum(i - (n_keys_blk - 1), 0) if n_keys_blk < nblk else 0
        start = kb0 * BLK if isinstance(kb0, int) else pl.multiple_of(kb0 * BLK, BLK)
        res = []
        for hh in range(hpg):
            c0 = (g * hpg + hh) * HEAD_DIM
            q = q_ref[0, rows, c0:c0 + HEAD_DIM]
            k = k_ref[0, pl.ds(start, n_keys_blk * BLK), c0:c0 + HEAD_DIM]
            v = v_ref[0, pl.ds(start, n_keys_blk * BLK), c0:c0 + HEAD_DIM]
            tbase = tile_off[g] + hh * (nd + 1)
            deltas = [i - kb0 - j for j in range(n_keys_blk)]
            bias = _lane_cat([t_ref[tbase + jnp.where(dl >= 0, dl, nd)] for dl in deltas])
            o, m, l = _softmax_pv(_dot_nt(q, k) + bias, v, axis=1)
            res += [o, m * ATTN_SCALE + jnp.log(l)]
        return tuple(res)

    for sub in range(_Q_PER_STEP):
        i = pl.program_id(1) * _Q_PER_STEP + sub
        rows = slice(sub * BLK, (sub + 1) * BLK)
        outs = [[None] * len(A_GROUPS) for _ in range(hpg)]
        lses = [[None] * len(A_GROUPS) for _ in range(hpg)]
        for g in range(len(A_GROUPS)):
            nd = tile_cnt[g]
            if nd == nblk and nblk % _CAUSAL_STEP == 0:
                res = lax.switch(i // _CAUSAL_STEP,
                                 [functools.partial(group, i, rows, g, (v + 1) * _CAUSAL_STEP)
                                  for v in range(nblk // _CAUSAL_STEP)])
            else:
                res = group(i, rows, g, nd)
            for hh in range(hpg):
                outs[hh][g], lses[hh][g] = res[2 * hh], res[2 * hh + 1]
        for hh in range(hpg):
            mx = functools.reduce(jnp.maximum, lses[hh])
            es = [jnp.exp(x - mx) for x in lses[hh]]
            tot = functools.reduce(lambda a, b: a + b, es)
            o = functools.reduce(lambda a, b: a + b,
                                 [(e / tot) * og for e, og in zip(es, outs[hh])])
            o_ref[0, rows, hh * HEAD_DIM:(hh + 1) * HEAD_DIM] = o.astype(o_ref.dtype)


def _mixer_a(u, tiles_a, nblk):
    b, s, _ = u.shape
    w = A_HEADS * HEAD_DIM
    kern = functools.partial(_mixer_a_kernel, tile_off=tuple(_a_tile_offsets(nblk)),
                             tile_cnt=tuple(_a_tile_counts(nblk)))
    n_a = tiles_a.shape[0]
    tq = _Q_PER_STEP * BLK
    return pl.pallas_call(
        kern,
        grid=(b, nblk // _Q_PER_STEP),
        in_specs=[pl.BlockSpec((1, tq, w), lambda bi, i: (bi, i, _COL_A_Q // A_HEADS)),
                  pl.BlockSpec((1, s, w), lambda bi, i: (bi, 0, _COL_A_K // A_HEADS)),
                  pl.BlockSpec((1, s, w), lambda bi, i: (bi, 0, _COL_A_V // A_HEADS)),
                  pl.BlockSpec((n_a, BLK, BLK), lambda bi, i: (0, 0, 0))],
        out_specs=pl.BlockSpec((1, tq, A_HEADS_PER_GROUP * HEAD_DIM), lambda bi, i: (bi, i, 0)),
        out_shape=jax.ShapeDtypeStruct((b, s, A_HEADS_PER_GROUP * HEAD_DIM), MXU_DTYPE),
        compiler_params=_params("arbitrary", "arbitrary"),
        name="mixer_a",
    )(u, u, u, tiles_a)


_C_TK = 256


def _split3(x):
    hi = x.astype(MXU_DTYPE)
    r1 = x - hi.astype(F32)
    mid = r1.astype(MXU_DTYPE)
    lo = (r1 - mid.astype(F32)).astype(MXU_DTYPE)
    return hi, mid, lo


def _fox_decay_kernel(f_ref, bf_ref, c_ref):
    x = f_ref[0] + bf_ref[...]
    logf = jnp.minimum(x, 0.0) - jnp.log1p(jnp.exp(-jnp.abs(x)))
    logf_t = logf.T[_FORGET_ROW0:_FORGET_ROW0 + _FORGET_ROWS]
    nk = c_ref.shape[1]
    r = lax.broadcasted_iota(jnp.int32, (_C_TK, _C_TK), 0)
    c = lax.broadcasted_iota(jnp.int32, (_C_TK, _C_TK), 1)
    upper = (r <= c).astype(MXU_DTYPE)
    carry = jnp.zeros((_FORGET_ROWS, 1), F32)
    for j in range(nk):
        hi, mid, lo = _split3(logf_t[:, j * _C_TK:(j + 1) * _C_TK])
        cs = (_dot(hi, upper) + _dot(mid, upper)) + _dot(lo, upper) + carry
        c_ref[0, j] = cs
        carry = cs[:, _C_TK - 1:_C_TK]


def _fox_decay(u_small, b_f):
    b, s, _ = u_small.shape
    nk = s // _C_TK
    bf_pad = jnp.zeros((1, LANES), F32).at[0, _FORGET_LANE0:_FORGET_LANE0 + C_HEADS].set(b_f)
    return pl.pallas_call(
        _fox_decay_kernel,
        grid=(b,),
        in_specs=[pl.BlockSpec((1, s, LANES), lambda bi: (bi, 0, _SMALL_FORGET)),
                  pl.BlockSpec((1, LANES), lambda bi: (0, 0))],
        out_specs=pl.BlockSpec((1, nk, _FORGET_ROWS, _C_TK), lambda bi: (bi, 0, 0, 0)),
        out_shape=jax.ShapeDtypeStruct((b, nk, _FORGET_ROWS, _C_TK), F32),
        compiler_params=_params("arbitrary"),
        name="fox_decay",
    )(u_small, bf_pad)


_C_TQ = 256


def _mixer_c_kernel(q_ref, k_ref, v_ref, c_ref, o_ref):
    row = _FORGET_LANE0 - _FORGET_ROW0 + pl.program_id(1)
    tq = _C_TQ
    r = lax.broadcasted_iota(jnp.int32, (tq, tq), 0)
    c = lax.broadcasted_iota(jnp.int32, (tq, tq), 1)
    for i in range(q_ref.shape[1] // tq):
        nk = (i + 1) * tq
        q = q_ref[0, i * tq:(i + 1) * tq, :]
        decay = _lane_cat([c_ref[0, j, pl.ds(row, 1), :] for j in range(nk // _C_TK)])
        s = _dot_nt(q, k_ref[0, :nk, :]) - decay * _INV_SCALE
        diag = jnp.where(c <= r, s[:, nk - tq:], NEG_INF)
        s = diag if i == 0 else _lane_cat([s[:, :nk - tq], diag])
        o, _, _ = _softmax_pv(s, v_ref[0, :nk, :], axis=1)
        o_ref[0, i * tq:(i + 1) * tq, :] = o.astype(o_ref.dtype)


def _mixer_c(u, cdec):
    b, s, _ = u.shape
    nk = s // _C_TK
    head = lambda col: pl.BlockSpec((1, s, HEAD_DIM), lambda bi, h: (bi, 0, col + h))
    return pl.pallas_call(
        _mixer_c_kernel,
        grid=(b, C_HEADS),
        in_specs=[head(_COL_C_Q), head(_COL_C_K), head(_COL_C_V),
                  pl.BlockSpec((1, nk, _FORGET_ROWS, _C_TK), lambda bi, h: (bi, 0, 0, 0))],
        out_specs=head(0),
        out_shape=jax.ShapeDtypeStruct((b, s, C_HEADS * HEAD_DIM), MXU_DTYPE),
        compiler_params=_params("arbitrary", "arbitrary"),
        name="mixer_c",
    )(u, u, u, cdec)


def _gelu(x):
    return 0.5 * x * (1.0 + lax.erf(x * np.float32(math.sqrt(0.5))))


def _nsa_compress_kernel(k_ref, v_ref, pe_ref, w1_ref, b1_ref, w2_ref, b2_ref, kc_ref, vct_ref):
    n_chunk = k_ref.shape[1] // CMP_STRIDE
    halves = CMP_LEN // CMP_STRIDE
    res = []
    for c, src in enumerate((k_ref, v_ref)):
        parts = []
        for half in range(halves):
            acc = jnp.zeros((n_chunk, HEAD_DIM), F32)
            for p in range(CMP_STRIDE):
                pos = half * CMP_STRIDE + p
                rows = src[0, pl.ds(p, n_chunk, stride=CMP_STRIDE), :] + pe_ref[c, pos:pos + 1, :]
                acc = acc + _dot(rows.astype(MXU_DTYPE), w1_ref[c, pos])
            parts.append(acc)
        hid = parts[0]
        for half in range(1, halves):
            hid = hid + pltpu.roll(parts[half], n_chunk - half, 0)
        hid = _gelu(hid + b1_ref[c:c + 1, :])
        res.append(_dot(hid.astype(MXU_DTYPE), w2_ref[c]) + b2_ref[c:c + 1, :])
    kc_ref[0] = res[0].astype(kc_ref.dtype)
    vct_ref[0] = res[1].T.astype(vct_ref.dtype)


def _nsa_compress(u_small, cmp_pe, cmp_w1, cmp_b1, cmp_w2, cmp_b2, layer):
    b, s, _ = u_small.shape
    n_chunk = s // CMP_STRIDE
    of_layer = lambda a: pl.BlockSpec((None,) + a.shape[1:],
                                      lambda bi: (layer,) + (0,) * (a.ndim - 1))
    return pl.pallas_call(
        _nsa_compress_kernel,
        grid=(b,),
        in_specs=[pl.BlockSpec((1, s, LANES), lambda bi: (bi, 0, _SMALL_KC)),
                  pl.BlockSpec((1, s, LANES), lambda bi: (bi, 0, _SMALL_VC)),
                  of_layer(cmp_pe), of_layer(cmp_w1), of_layer(cmp_b1),
                  of_layer(cmp_w2), of_layer(cmp_b2)],
        out_specs=[pl.BlockSpec((1, n_chunk, HEAD_DIM), lambda bi: (bi, 0, 0)),
                   pl.BlockSpec((1, HEAD_DIM, n_chunk), lambda bi: (bi, 0, 0))],
        out_shape=[jax.ShapeDtypeStruct((b, n_chunk, HEAD_DIM), MXU_DTYPE),
                   jax.ShapeDtypeStruct((b, HEAD_DIM, n_chunk), MXU_DTYPE)],
        compiler_params=_params("arbitrary"),
        name="nsa_compress",
    )(u_small, u_small, cmp_pe, cmp_w1, cmp_b1, cmp_w2, cmp_b2)


def _mixer_b_kernel(q_ref, ks_ref, vs_ref, kw_ref, vw_ref, kc_ref, vct_ref, gl_ref, bg_ref,
                    tslc_ref, twin_ref, tcmp_ref, o_ref, vst_ref, vwt_ref):
    step = pl.program_id(1)
    nblk = vwt_ref.shape[0]
    nh = B_HEADS
    n_slc = ks_ref.shape[1] // SLC_LEN

    @pl.when(step == 0)
    def _():
        for kb in range(nblk):
            rows = slice(kb * BLK, (kb + 1) * BLK)
            vst_ref[:, rows] = vs_ref[0, rows, :].astype(F32).T.astype(MXU_DTYPE)
            vwt_ref[kb] = vw_ref[0, rows, :].astype(F32).T.astype(MXU_DTYPE)

    for sub in range(_Q_PER_STEP):
        i = step * _Q_PER_STEP + sub
        rows = slice(sub * BLK, (sub + 1) * BLK)
        q4 = jnp.concatenate([q_ref[0, rows, h * HEAD_DIM:(h + 1) * HEAD_DIM] for h in range(nh)],
                             axis=0)

        st = _dot_nt(kc_ref[0], q4) + _lane_cat([tcmp_ref[sub * nh + h] for h in range(nh)])
        m = jnp.max(st, axis=0, keepdims=True)
        e = jnp.exp2((st - m) * _EXP2_SCALE)
        l = jnp.sum(e, axis=0, keepdims=True)
        pc = jnp.where(m > 0.5 * NEG_INF, e / l, 0.0).astype(MXU_DTYPE)
        o_cmp = _dot(vct_ref[0], pc)
        n_cmp_pad = kc_ref.shape[1]
        jj = lax.broadcasted_iota(jnp.int32, (n_slc, n_cmp_pad), 0)
        nn = lax.broadcasted_iota(jnp.int32, (n_slc, n_cmp_pad), 1)
        overlap = ((nn * CMP_STRIDE <= jj * SLC_LEN + SLC_LEN - 1)
                   & (nn * CMP_STRIDE + CMP_LEN - 1 >= jj * SLC_LEN)).astype(MXU_DTYPE)
        imp4 = _dot(overlap, pc)
        imp = functools.reduce(lambda a, b: a + b,
                               [imp4[:, h * BLK:(h + 1) * BLK] for h in range(nh)])
        t = i * BLK + lax.broadcasted_iota(jnp.int32, (n_slc, BLK), 1)
        blk = lax.broadcasted_iota(jnp.int32, (n_slc, BLK), 0)
        cur = t // SLC_LEN
        forced = (blk == 0) | (blk == cur) | (blk == cur - 1)
        causal = blk * SLC_LEN <= t
        score = jnp.where(forced, FORCE_SCORE, jnp.where(causal, imp, NEG_INF))
        rank = jnp.zeros((n_slc, BLK), jnp.int32)
        for r in range(n_slc):
            row = score[r:r + 1, :]
            ahead = (row > score) | ((row == score) & (r < blk))
            rank = rank + ahead.astype(jnp.int32)
        add = jnp.where(rank < min(N_SELECT, n_slc), 0.0, NEG_INF).astype(F32)
        add4 = _lane_cat([add] * nh)

        def bias_rows(t_ref, dl, n_real):
            idx = jnp.where(dl >= 0, dl, n_real)
            return _lane_cat([t_ref[idx * nh + h] for h in range(nh)])

        per_blk = BLK // SLC_LEN

        def selected(n_keys_blk):
            mask_rows = []
            for kb in range(n_keys_blk):
                sel = jnp.concatenate(
                    [jnp.broadcast_to(add4[kb * per_blk + j:kb * per_blk + j + 1, :],
                                      (SLC_LEN, nh * BLK)) for j in range(per_blk)], axis=0)
                mask_rows.append(bias_rows(tslc_ref, i - kb, nblk) + sel)
            n_keys = n_keys_blk * BLK
            s = _dot_nt(ks_ref[0, :n_keys, :], q4) + jnp.concatenate(mask_rows, axis=0)
            return _softmax_pv(s, vst_ref[:, :n_keys], axis=0)[0]

        if nblk % _CAUSAL_STEP == 0:
            o_slc = lax.switch(i // _CAUSAL_STEP,
                               [functools.partial(selected, (v + 1) * _CAUSAL_STEP)
                                for v in range(nblk // _CAUSAL_STEP)])
        else:
            o_slc = selected(nblk)

        n_win = twin_ref.shape[0] // nh - 1
        kb0 = jnp.maximum(i - (n_win - 1), 0)
        start = pl.multiple_of(kb0 * BLK, BLK)
        bias = jnp.concatenate([bias_rows(twin_ref, i - kb0 - j, n_win) for j in range(n_win)],
                               axis=0)
        s = _dot_nt(kw_ref[0, pl.ds(start, n_win * BLK), :], q4) + bias
        o_win, _, _ = _softmax_pv(s, _lane_cat([vwt_ref[kb0 + j] for j in range(n_win)]), axis=0)

        gt = jax.nn.sigmoid(gl_ref[0, rows, :] + bg_ref[...]).T

        def gate(br):
            r0 = _GATE_LANE0 + br * nh
            return _lane_cat([gt[r0 + h:r0 + h + 1, :] for h in range(nh)])

        o_t = gate(0) * o_cmp + gate(1) * o_slc + gate(2) * o_win
        for h in range(nh):
            o_ref[0, rows, h * HEAD_DIM:(h + 1) * HEAD_DIM] = (
                o_t[:, h * BLK:(h + 1) * BLK].T.astype(o_ref.dtype))


def _mixer_b(u, u_small, kc, vct, b_nsa_gate, tiles, nblk):
    b, s, _ = u.shape
    nh = B_HEADS
    bg_pad = jnp.zeros((1, LANES), F32).at[0, _GATE_LANE0:_GATE_LANE0 + 3 * nh].set(b_nsa_gate)
    whole = lambda a: pl.BlockSpec(a.shape, lambda bi, i: (0, 0, 0))
    kv = lambda col: pl.BlockSpec((1, s, HEAD_DIM), lambda bi, i: (bi, 0, col))
    n_chunk = kc.shape[1]
    tq = _Q_PER_STEP * BLK
    return pl.pallas_call(
        _mixer_b_kernel,
        grid=(b, nblk // _Q_PER_STEP),
        in_specs=[pl.BlockSpec((1, tq, nh * HEAD_DIM), lambda bi, i: (bi, i, _COL_B_Q // nh)),
                  kv(_COL_B_KS), kv(_COL_B_VS), kv(_COL_B_KW), kv(_COL_B_VW),
                  pl.BlockSpec((1, n_chunk, HEAD_DIM), lambda bi, i: (bi, 0, 0)),
                  pl.BlockSpec((1, HEAD_DIM, n_chunk), lambda bi, i: (bi, 0, 0)),
                  pl.BlockSpec((1, tq, LANES), lambda bi, i: (bi, i, _SMALL_GATE)),
                  pl.BlockSpec((1, LANES), lambda bi, i: (0, 0)),
                  whole(tiles["slc"]), whole(tiles["win"]),
                  pl.BlockSpec((_Q_PER_STEP * nh, BLK, BLK), lambda bi, i: (i, 0, 0))],
        out_specs=pl.BlockSpec((1, tq, nh * HEAD_DIM), lambda bi, i: (bi, i, 0)),
        out_shape=jax.ShapeDtypeStruct((b, s, nh * HEAD_DIM), MXU_DTYPE),
        scratch_shapes=[pltpu.VMEM((HEAD_DIM, s), MXU_DTYPE),
                        pltpu.VMEM((nblk, HEAD_DIM, BLK), MXU_DTYPE)],
        compiler_params=_params("arbitrary", "arbitrary"),
        name="mixer_b",
    )(u, u, u, u, u, kc, vct, u_small, bg_pad, tiles["slc"], tiles["win"], tiles["cmp"])


def _layer_norm(y, g, b):
    mu = jnp.mean(y, axis=-1, keepdims=True)
    yc = y - mu
    var = jnp.mean(yc * yc, axis=-1, keepdims=True)
    return yc * lax.rsqrt(var + LN_EPS) * g + b


def _merge_kernel(x_ref, oa_ref, ob_ref, oc_ref, wga_ref, wgb_ref, wgc_ref, bga_ref, bgb_ref,
                  bgc_ref, wpa_ref, wpb_ref, wpc_ref, wo_ref, g_ref, b_ref, y_ref, xb_ref, *,
                  alpha):
    j = pl.program_id(1)

    @pl.when(j == 0)
    def _():
        xb_ref[...] = x_ref[...].astype(MXU_DTYPE)
        y_ref[...] = jnp.zeros_like(y_ref)

    xb = xb_ref[...]
    mixed = None
    for o_ref, wg_ref, bg_ref, wp_ref in ((oa_ref, wga_ref, bga_ref, wpa_ref),
                                          (ob_ref, wgb_ref, bgb_ref, wpb_ref),
                                          (oc_ref, wgc_ref, bgc_ref, wpc_ref)):
        gate = jax.nn.sigmoid(_dot(xb, wg_ref[...]) + bg_ref[...])
        term = gate * _dot(o_ref[...], wp_ref[...])
        mixed = term if mixed is None else mixed + term
    y_ref[...] += _dot(mixed.astype(MXU_DTYPE), wo_ref[...])

    @pl.when(j == pl.num_programs(1) - 1)
    def _():
        y_ref[...] = _layer_norm(alpha * x_ref[...] + y_ref[...], g_ref[...], b_ref[...])


def _merge_ln(x2, oa, ob, oc, w_gate, b_gate, w_pa, w_pb, w_pc, w_out, ln_g, ln_b, layer, alpha,
              tm, tn):
    t, d = x2.shape
    nj = d // tn
    row = lambda k: pl.BlockSpec((tm, k), lambda i, j: (i, 0))
    colw = lambda k, shift: pl.BlockSpec((None, k, tn), lambda i, j: (layer, 0, shift * nj + j))
    vec = lambda shift: pl.BlockSpec((None, 1, tn), lambda i, j: (layer, 0, shift * nj + j))
    fullvec = pl.BlockSpec((None, 1, d), lambda i, j: (layer, 0, 0))
    return pl.pallas_call(
        functools.partial(_merge_kernel, alpha=alpha),
        grid=(t // tm, nj),
        in_specs=[row(d), row(oa.shape[1]), row(ob.shape[1]), row(oc.shape[1]),
                  colw(d, 0), colw(d, 1), colw(d, 2), vec(0), vec(1), vec(2),
                  colw(w_pa.shape[1], 0), colw(w_pb.shape[1], 0), colw(w_pc.shape[1], 0),
                  pl.BlockSpec((None, tn, d), lambda i, j: (layer, j, 0)), fullvec, fullvec],
        out_specs=pl.BlockSpec((tm, d), lambda i, j: (i, 0)),
        out_shape=jax.ShapeDtypeStruct((t, d), F32),
        scratch_shapes=[pltpu.VMEM((tm, d), MXU_DTYPE)],
        compiler_params=_params("arbitrary", "arbitrary"),
        name="merge_ln",
    )(x2, oa, ob, oc, w_gate, w_gate, w_gate, b_gate, b_gate, b_gate, w_pa, w_pb, w_pc, w_out,
      ln_g, ln_b)


_TAIL = 8


def _ffn_kernel(x_ref, wa_ref, wb_ref, cwa_ref, cwb_ref, cba_ref, cbb_ref, wd_ref, g_ref, b_ref,
                y_ref, xb_ref, ha_ref, hb_ref, tail_ref, *, alpha, tiles_per_seq):
    i = pl.program_id(0)
    j = pl.program_id(1)
    tm = x_ref.shape[0]
    tf = wa_ref.shape[1]

    @pl.when(j == 0)
    def _():
        xb_ref[...] = x_ref[...].astype(MXU_DTYPE)
        y_ref[...] = jnp.zeros_like(y_ref)

    @pl.when(i % tiles_per_seq == 0)
    def _():
        tail_ref[j] = jnp.zeros(tail_ref.shape[1:], F32)

    xb = xb_ref[...]
    for h_ref, w_ref, c0 in ((ha_ref, wa_ref, 0), (hb_ref, wb_ref, tf)):
        h_ref[0:_TAIL, :] = tail_ref[j, :, c0:c0 + tf]
        h_ref[_TAIL:, :] = _dot(xb, w_ref[...])
        tail_ref[j, :, c0:c0 + tf] = h_ref[tm:tm + _TAIL, :]

    def conv(h_ref, cw_ref, cb_ref):
        out = cb_ref[...]
        for tap in range(CONV_W):
            shift = CONV_W - 1 - tap
            out = out + cw_ref[tap:tap + 1, :] * h_ref[pl.ds(_TAIL - shift, tm), :]
        return out

    a = conv(ha_ref, cwa_ref, cba_ref)
    bb = conv(hb_ref, cwb_ref, cbb_ref)
    y_ref[...] += _dot((_gelu(a) * bb).astype(MXU_DTYPE), wd_ref[...])

    @pl.when(j == pl.num_programs(1) - 1)
    def _():
        y_ref[...] = _layer_norm(alpha * x_ref[...] + y_ref[...], g_ref[...], b_ref[...])


def _ffn_ln(x2, w_up, conv_w, conv_b, w_down, ln_g, ln_b, layer, alpha, seq, tm, tf):
    t, d = x2.shape
    f = w_down.shape[1]
    nj = f // tf
    assert seq % tm == 0 and CONV_W - 1 <= _TAIL
    half = lambda k, shift: pl.BlockSpec((None, k, tf), lambda i, j: (layer, 0, shift * nj + j))
    fullvec = pl.BlockSpec((None, 1, d), lambda i, j: (layer, 0, 0))
    return pl.pallas_call(
        functools.partial(_ffn_kernel, alpha=alpha, tiles_per_seq=seq // tm),
        grid=(t // tm, nj),
        in_specs=[pl.BlockSpec((tm, d), lambda i, j: (i, 0)),
                  half(d, 0), half(d, 1), half(CONV_W, 0), half(CONV_W, 1), half(1, 0), half(1, 1),
                  pl.BlockSpec((None, tf, d), lambda i, j: (layer, j, 0)), fullvec, fullvec],
        out_specs=pl.BlockSpec((tm, d), lambda i, j: (i, 0)),
        out_shape=jax.ShapeDtypeStruct((t, d), F32),
        scratch_shapes=[pltpu.VMEM((tm, d), MXU_DTYPE),
                        pltpu.VMEM((tm + _TAIL, tf), F32),
                        pltpu.VMEM((tm + _TAIL, tf), F32),
                        pltpu.VMEM((nj, _TAIL, 2 * tf), F32)],
        compiler_params=_params("arbitrary", "arbitrary"),
        name="ffn_ln",
    )(x2, w_up, w_up, conv_w, conv_w, conv_b, conv_b, w_down, ln_g, ln_b)


def _pack_moves():
    main_dst = {'a_q': _COL_A_Q, 'a_k': _COL_A_K, 'a_v': _COL_A_V,
                'c_q': _COL_C_Q, 'c_k': _COL_C_K, 'c_v': _COL_C_V, 'b_q': _COL_B_Q,
                'b_k_slc': _COL_B_KS, 'b_v_slc': _COL_B_VS, 'b_k_win': _COL_B_KW,
                'b_v_win': _COL_B_VW}
    moves = [(blk * HEAD_DIM, _SPLIT[n][0], _SPLIT[n][1] - _SPLIT[n][0])
             for n, blk in main_dst.items()]
    moves += [(_N_MAIN + _SMALL_KC * LANES, _SPLIT['b_k_cmp'][0], LANES),
              (_N_MAIN + _SMALL_VC * LANES, _SPLIT['b_v_cmp'][0], LANES),
              (_N_MAIN + _SMALL_GATE * LANES, _GATE_COL0, LANES),
              (_N_MAIN + _SMALL_FORGET * LANES, _FORGET_COL0, N_IN - _FORGET_COL0)]
    return moves


def _pack_kernel(w_ref, o_ref):
    tail0 = _N_MAIN + _SMALL_FORGET * LANES
    o_ref[0, :, tail0:] = jnp.zeros((o_ref.shape[1], o_ref.shape[2] - tail0), o_ref.dtype)
    for d0, s0, width in _pack_moves():
        o_ref[0, :, d0:d0 + width] = w_ref[0, :, s0:s0 + width].astype(o_ref.dtype)


def _pack_w_in(w_in, tk):
    depth, d, n_in = w_in.shape
    assert n_in == N_IN
    n_out = _N_MAIN + _N_SMALL
    return pl.pallas_call(
        _pack_kernel,
        grid=(depth, d // tk),
        in_specs=[pl.BlockSpec((1, tk, n_in), lambda l, i: (l, i, 0))],
        out_specs=pl.BlockSpec((1, tk, n_out), lambda l, i: (l, i, 0)),
        out_shape=jax.ShapeDtypeStruct((depth, d, n_out), MXU_DTYPE),
        compiler_params=_params("arbitrary", "arbitrary"),
        name="pack_w_in",
    )(w_in)


_TM_PROJ, _TN_PROJ = 1024, 1536
_TM_MERGE, _TN_MERGE = 512, 512
_TM_FFN, _TF_FFN = 512, 512


def kernel(x, rel_bias, w_in, b_f, b_nsa_gate, cmp_pe, cmp_w1, cmp_b1, cmp_w2, cmp_b2, w_gate, b_gate, w_pa, w_pb, w_pc, w_out, ln1_g, ln1_b, w_up, conv_w, conv_b, w_down, ln2_g, ln2_b):
    bsz, seq, d = x.shape
    depth = w_in.shape[0]
    nblk = seq // BLK
    alpha = (2 * depth) ** 0.25
    t = bsz * seq
    bf = lambda a: a.astype(MXU_DTYPE)
    row = lambda a: a[:, None, :]

    tiles = _bias_tiles(rel_bias, nblk)
    w_in_p = _pack_w_in(w_in, 256)
    cmp_w1_b = bf(cmp_w1).reshape(depth, 2, CMP_LEN, HEAD_DIM, HEAD_DIM)
    cmp_w2_b = bf(cmp_w2)
    w_gate_b, w_pa_b, w_pb_b, w_pc_b, w_out_b = (bf(w_gate), bf(w_pa), bf(w_pb), bf(w_pc),
                                                 bf(w_out))
    w_up_b, w_down_b = bf(w_up), bf(w_down)
    x2 = x.reshape(t, d)
    for l in range(depth):
        u, u_small = _proj(x2, w_in_p, l, _TM_PROJ, _TN_PROJ)
        u = u.reshape(bsz, seq, -1)
        u_small = u_small.reshape(bsz, seq, -1)

        o_a = _mixer_a(u, tiles["a"], nblk)
        kc, vct = _nsa_compress(u_small, cmp_pe, cmp_w1_b, cmp_b1, cmp_w2_b, cmp_b2, l)
        o_b = _mixer_b(u, u_small, kc, vct, b_nsa_gate[l], tiles, nblk)
        cdec = _fox_decay(u_small, b_f[l])
        o_c = _mixer_c(u, cdec)

        x1 = _merge_ln(x2, o_a.reshape(t, -1), o_b.reshape(t, -1), o_c.reshape(t, -1),
                       w_gate_b, row(b_gate), w_pa_b, w_pb_b, w_pc_b, w_out_b,
                       row(ln1_g), row(ln1_b), l, alpha, _TM_MERGE, _TN_MERGE)
        x2 = _ffn_ln(x1, w_up_b, conv_w, row(conv_b), w_down_b, row(ln2_g), row(ln2_b),
                     l, alpha, seq, _TM_FFN, _TF_FFN)
    return x2.reshape(bsz, seq, d)
```

```python
import functools
import math

import numpy as np
import jax
import jax.numpy as jnp
from jax import lax
from jax.experimental import pallas as pl
from jax.experimental.pallas import tpu as pltpu

HEAD_DIM = 128
A_GROUPS = ((128, 1), (512, 4), (2048, 16))
A_HEADS_PER_GROUP = 2
A_HEADS = A_HEADS_PER_GROUP * len(A_GROUPS)
B_HEADS = 4
CMP_LEN = 32
CMP_STRIDE = 16
SLC_LEN = 64
N_SELECT = 16
WIN = 512
C_HEADS = 6
N_BUCKETS = 32
MAX_DISTANCE = 2048
CONV_W = 3
LN_EPS = 1e-5
NEG_INF = -1e30
FORCE_SCORE = 1e9
ATTN_SCALE = HEAD_DIM ** -0.5
_INV_SCALE = np.float32(1.0 / ATTN_SCALE)
_EXP2_SCALE = np.float32(ATTN_SCALE * math.log2(math.e))

LANES = 128
VMEM_LIMIT_BYTES = 56 * 2 ** 20

MXU_DTYPE = jnp.bfloat16
F32 = jnp.float32
BLK = LANES

_COL_A_Q, _COL_A_K, _COL_A_V = 0, 6, 12
_COL_C_Q, _COL_C_K, _COL_C_V = 18, 24, 30
_COL_B_Q = 36
_COL_B_KS, _COL_B_VS, _COL_B_KW, _COL_B_VW = 40, 41, 42, 43
_N_COL_BLOCKS = 44


def _in_split_offsets():
    widths = (('a_q', A_HEADS * HEAD_DIM), ('a_k', A_HEADS * HEAD_DIM), ('a_v', A_HEADS * HEAD_DIM),
              ('b_q', B_HEADS * HEAD_DIM), ('b_k_cmp', HEAD_DIM), ('b_v_cmp', HEAD_DIM),
              ('b_k_slc', HEAD_DIM), ('b_v_slc', HEAD_DIM), ('b_k_win', HEAD_DIM),
              ('b_v_win', HEAD_DIM), ('b_gate', 3 * B_HEADS),
              ('c_q', C_HEADS * HEAD_DIM), ('c_k', C_HEADS * HEAD_DIM), ('c_v', C_HEADS * HEAD_DIM),
              ('c_f', C_HEADS))
    out, o = {}, 0
    for name, w in widths:
        out[name] = (o, o + w)
        o += w
    return out, o


_SPLIT, N_IN = _in_split_offsets()
_SMALL_KC, _SMALL_VC, _SMALL_GATE, _SMALL_FORGET = 0, 1, 2, 3
_N_MAIN = _N_COL_BLOCKS * HEAD_DIM
_N_SMALL = 4 * LANES
_GATE_COL0 = _SPLIT['b_gate'][0] // LANES * LANES
_GATE_LANE0 = _SPLIT['b_gate'][0] - _GATE_COL0
_FORGET_COL0 = _SPLIT['c_f'][0] // LANES * LANES
_FORGET_LANE0 = _SPLIT['c_f'][0] - _FORGET_COL0
assert _SPLIT['b_gate'][1] - _GATE_COL0 <= LANES and N_IN - _FORGET_COL0 <= LANES
_FORGET_ROW0 = _FORGET_LANE0 // 8 * 8
_FORGET_ROWS = -(-(_FORGET_LANE0 + C_HEADS - _FORGET_ROW0) // 8) * 8


def _params(*sem):
    return pltpu.CompilerParams(dimension_semantics=sem, vmem_limit_bytes=VMEM_LIMIT_BYTES)


def _dot(a, b):
    return jnp.dot(a, b, preferred_element_type=F32)


def _dot_nt(a, b):
    return lax.dot_general(a, b, (((1,), (1,)), ((), ())), preferred_element_type=F32)


def _bias_tile_kernel(base_ref, rmul_ref, cmul_ref, dmask_ref, dmax_ref, rmax_ref, head_ref,
                      rel_ref, out_ref):
    row = lax.broadcasted_iota(jnp.int32, (BLK, BLK), 0)
    col = lax.broadcasted_iota(jnp.int32, (BLK, BLK), 1)
    max_exact = N_BUCKETS // 2
    per_step = out_ref.shape[0]
    for k in range(per_step):
        t = pl.program_id(0) * per_step + k
        dist = base_ref[t] + rmul_ref[t] * row + cmul_ref[t] * col
        valid = ((dist >= 0) & (dist <= dmax_ref[t]) & ((dist & dmask_ref[t]) == 0)
                 & (row < rmax_ref[t]))
        d = jnp.maximum(dist, 0)
        log_ratio = (jnp.log(jnp.maximum(d, 1).astype(F32) / max_exact)
                     / math.log(MAX_DISTANCE / max_exact))
        large = jnp.minimum(max_exact + (log_ratio * (N_BUCKETS - max_exact)).astype(jnp.int32),
                            N_BUCKETS - 1)
        bucket = jnp.where(d < max_exact, d, large)
        h = head_ref[t]
        val = jnp.zeros((BLK, BLK), F32)
        for b in range(N_BUCKETS):
            val = jnp.where(bucket == b, rel_ref[b, h], val)
        out_ref[k] = jnp.where(valid, val * _INV_SCALE, NEG_INF)


def _a_tile_counts(nblk):
    return [min(window // BLK + 1, nblk) for window, _ in A_GROUPS]


def _tile_tables(nblk):
    big = 2 ** 30
    fam = {}
    fam["a"] = [(dl * BLK, 1, -1, dil - 1, window if dl < cnt else -1, BLK,
                 g * A_HEADS_PER_GROUP + hh)
                for g, ((window, dil), cnt) in enumerate(zip(A_GROUPS, _a_tile_counts(nblk)))
                for hh in range(A_HEADS_PER_GROUP)
                for dl in range(cnt + 1)]
    fam["slc"] = [(dl * BLK, -1, 1, 0, big if dl < nblk else -1, BLK, A_HEADS + h)
                  for dl in range(nblk + 1) for h in range(B_HEADS)]
    n_win = WIN // BLK + 1
    fam["win"] = [(dl * BLK, -1, 1, 0, WIN - 1 if dl < n_win else -1, BLK, A_HEADS + h)
                  for dl in range(n_win + 1) for h in range(B_HEADS)]
    n_cmp = (nblk * BLK - CMP_LEN) // CMP_STRIDE + 1
    fam["cmp"] = [(i * BLK - (CMP_LEN - 1), -CMP_STRIDE, 1, 0, big, n_cmp, A_HEADS + h)
                  for i in range(nblk) for h in range(B_HEADS)]
    return {k: [np.asarray(c, np.int32) for c in zip(*rows)] for k, rows in fam.items()}


def _a_tile_offsets(nblk):
    cnt = _a_tile_counts(nblk)
    return [A_HEADS_PER_GROUP * sum(c + 1 for c in cnt[:g]) for g in range(len(A_GROUPS))]


_MAX_TILES_PER_STEP = 17


def _bias_tiles(rel_bias, nblk):
    out = {}
    for name, cols in _tile_tables(nblk).items():
        n = cols[0].shape[0]
        per_step = max(k for k in range(1, _MAX_TILES_PER_STEP + 1) if n % k == 0)
        out[name] = pl.pallas_call(
            _bias_tile_kernel,
            grid_spec=pltpu.PrefetchScalarGridSpec(
                num_scalar_prefetch=len(cols),
                grid=(n // per_step,),
                in_specs=[pl.BlockSpec(memory_space=pltpu.SMEM)],
                out_specs=pl.BlockSpec((per_step, BLK, BLK), lambda t, *_: (t, 0, 0)),
            ),
            out_shape=jax.ShapeDtypeStruct((n, BLK, BLK), F32),
            compiler_params=_params("arbitrary"),
            name="bias_tiles_" + name,
        )(*[jnp.asarray(c) for c in cols], rel_bias)
    return out


def _proj_kernel(x_ref, w_ref, o_ref, os_ref, xb_ref):
    j = pl.program_id(1)
    tn = w_ref.shape[1]
    n_side = os_ref.shape[1]

    @pl.when(j == 0)
    def _():
        xb_ref[...] = x_ref[...].astype(MXU_DTYPE)

    acc = _dot(xb_ref[...], w_ref[...])
    o_ref[...] = acc.astype(o_ref.dtype)

    @pl.when(j == pl.num_programs(1) - 1)
    def _():
        os_ref[...] = acc[:, tn - n_side:]


def _proj(x2, w_packed, layer, tm, tn):
    t, d = x2.shape
    n_all = _N_MAIN + _N_SMALL
    assert n_all % tn == 0 and tn >= _N_SMALL
    nj = n_all // tn
    return pl.pallas_call(
        _proj_kernel,
        grid=(t // tm, nj),
        in_specs=[pl.BlockSpec((tm, d), lambda i, j: (i, 0)),
                  pl.BlockSpec((None, d, tn), lambda i, j: (layer, 0, j))],
        out_specs=[pl.BlockSpec((tm, tn), lambda i, j: (i, j)),
                   pl.BlockSpec((tm, _N_SMALL), lambda i, j: (i, 0))],
        out_shape=[jax.ShapeDtypeStruct((t, _N_MAIN), MXU_DTYPE),
                   jax.ShapeDtypeStruct((t, _N_SMALL), F32)],
        scratch_shapes=[pltpu.VMEM((tm, d), MXU_DTYPE)],
        compiler_params=_params("arbitrary", "arbitrary"),
        name="in_proj",
    )(x2, w_packed)


def _lane_cat(xs):
    return jnp.concatenate(xs, axis=1)


def _softmax_pv(z, v, axis):
    m = jnp.max(z, axis=axis, keepdims=True)
    p = jnp.exp2((z - m) * _EXP2_SCALE)
    l = jnp.sum(p, axis=axis, keepdims=True)
    if axis == 1:
        o = _dot(p.astype(MXU_DTYPE), v)
    else:
        o = _dot(v, p.astype(MXU_DTYPE))
    return o / l, m, l


_CAUSAL_STEP = 4
_Q_PER_STEP = 2


def _mixer_a_kernel(q_ref, k_ref, v_ref, t_ref, o_ref, s0_ref, s1_ref, p0_ref, p1_ref, *,
                    tile_off, tile_cnt):
    hpg = A_HEADS_PER_GROUP
    n_groups = len(A_GROUPS)
    nblk = k_ref.shape[1] // BLK
    step = pl.program_id(1)
    s_bufs, p_bufs = (s0_ref, s1_ref), (p0_ref, p1_ref)

    def body(n_full):
        chains = [(sub, g, hh) for sub in range(_Q_PER_STEP) for g in range(n_groups)
                  for hh in range(hpg)]

        def window(sub, g):
            i = step * _Q_PER_STEP + sub
            if tile_cnt[g] == nblk:
                return i, 0, 0, n_full
            kb0 = jnp.maximum(i - (tile_cnt[g] - 1), 0)
            return i, kb0, pl.multiple_of(kb0 * BLK, BLK), tile_cnt[g]

        stats = [None] * len(chains)
        results = {}
        for t in range(len(chains) + 2):
            if 0 <= t - 1 < len(chains):
                c = t - 1
                sub, g, hh = chains[c]
                i, kb0, _, nkb = window(sub, g)
                nd = tile_cnt[g]
                tbase = tile_off[g] + hh * (nd + 1)
                deltas = [i - kb0 - j for j in range(nkb)]
                bias = _lane_cat([t_ref[tbase + jnp.where(dl >= 0, dl, nd)] for dl in deltas])
                z = s_bufs[c % 2][:, :nkb * BLK] + bias
                m = jnp.max(z, axis=1, keepdims=True)
                p = jnp.exp2((z - m) * _EXP2_SCALE)
                stats[c] = (m, jnp.sum(p, axis=1, keepdims=True))
                p_bufs[c % 2][:, :nkb * BLK] = p.astype(MXU_DTYPE)
            if t < len(chains):
                sub, g, hh = chains[t]
                _, _, start, nkb = window(sub, g)
                c0 = (g * hpg + hh) * HEAD_DIM
                q = q_ref[0, sub * BLK:(sub + 1) * BLK, c0:c0 + HEAD_DIM]
                k = k_ref[0, pl.ds(start, nkb * BLK), c0:c0 + HEAD_DIM]
                s_bufs[t % 2][:, :nkb * BLK] = _dot_nt(q, k)
            if t - 2 >= 0:
                c = t - 2
                sub, g, hh = chains[c]
                _, _, start, nkb = window(sub, g)
                c0 = (g * hpg + hh) * HEAD_DIM
                v = v_ref[0, pl.ds(start, nkb * BLK), c0:c0 + HEAD_DIM]
                m, l = stats[c]
                results[chains[c]] = (_dot(p_bufs[c % 2][:, :nkb * BLK], v) / l,
                                      m * ATTN_SCALE + jnp.log(l))
                if (g, hh) == (n_groups - 1, hpg - 1):
                    for h2 in range(hpg):
                        outs = [results[(sub, g2, h2)][0] for g2 in range(n_groups)]
                        lses = [results[(sub, g2, h2)][1] for g2 in range(n_groups)]
                        mx = functools.reduce(jnp.maximum, lses)
                        es = [jnp.exp(x - mx) for x in lses]
                        tot = functools.reduce(lambda a, b: a + b, es)
                        o = functools.reduce(lambda a, b: a + b,
                                             [(e / tot) * og for e, og in zip(es, outs)])
                        o_ref[0, sub * BLK:(sub + 1) * BLK, h2 * HEAD_DIM:(h2 + 1) * HEAD_DIM] = (
                            o.astype(o_ref.dtype))

    if nblk % _CAUSAL_STEP == 0 and _CAUSAL_STEP % _Q_PER_STEP == 0:
        for v in range(nblk // _CAUSAL_STEP):
            pl.when(step * _Q_PER_STEP // _CAUSAL_STEP == v)(
                functools.partial(body, (v + 1) * _CAUSAL_STEP))
    else:
        body(nblk)


def _mixer_a(u, tiles_a, nblk):
    b, s, _ = u.shape
    w = A_HEADS * HEAD_DIM
    kern = functools.partial(_mixer_a_kernel, tile_off=tuple(_a_tile_offsets(nblk)),
                             tile_cnt=tuple(_a_tile_counts(nblk)))
    n_a = tiles_a.shape[0]
    tq = _Q_PER_STEP * BLK
    return pl.pallas_call(
        kern,
        grid=(b, nblk // _Q_PER_STEP),
        in_specs=[pl.BlockSpec((1, tq, w), lambda bi, i: (bi, i, _COL_A_Q // A_HEADS)),
                  pl.BlockSpec((1, s, w), lambda bi, i: (bi, 0, _COL_A_K // A_HEADS)),
                  pl.BlockSpec((1, s, w), lambda bi, i: (bi, 0, _COL_A_V // A_HEADS)),
                  pl.BlockSpec((n_a, BLK, BLK), lambda bi, i: (0, 0, 0))],
        out_specs=pl.BlockSpec((1, tq, A_HEADS_PER_GROUP * HEAD_DIM), lambda bi, i: (bi, i, 0)),
        out_shape=jax.ShapeDtypeStruct((b, s, A_HEADS_PER_GROUP * HEAD_DIM), MXU_DTYPE),
        scratch_shapes=[pltpu.VMEM((BLK, s), F32), pltpu.VMEM((BLK, s), F32),
                        pltpu.VMEM((BLK, s), MXU_DTYPE), pltpu.VMEM((BLK, s), MXU_DTYPE)],
        compiler_params=_params("arbitrary", "arbitrary"),
        name="mixer_a",
    )(u, u, u, tiles_a)


_C_TK = 256


def _split3(x):
    hi = x.astype(MXU_DTYPE)
    r1 = x - hi.astype(F32)
    mid = r1.astype(MXU_DTYPE)
    lo = (r1 - mid.astype(F32)).astype(MXU_DTYPE)
    return hi, mid, lo


def _fox_decay_kernel(f_ref, bf_ref, c_ref):
    x = f_ref[0] + bf_ref[...]
    logf = jnp.minimum(x, 0.0) - jnp.log1p(jnp.exp(-jnp.abs(x)))
    logf_t = logf.T[_FORGET_ROW0:_FORGET_ROW0 + _FORGET_ROWS]
    nk = c_ref.shape[1]
    r = lax.broadcasted_iota(jnp.int32, (_C_TK, _C_TK), 0)
    c = lax.broadcasted_iota(jnp.int32, (_C_TK, _C_TK), 1)
    upper = (r <= c).astype(MXU_DTYPE)
    carry = jnp.zeros((_FORGET_ROWS, 1), F32)
    for j in range(nk):
        hi, mid, lo = _split3(logf_t[:, j * _C_TK:(j + 1) * _C_TK])
        cs = (_dot(hi, upper) + _dot(mid, upper)) + _dot(lo, upper) + carry
        c_ref[0, j] = cs
        carry = cs[:, _C_TK - 1:_C_TK]


def _fox_decay(u_small, b_f):
    b, s, _ = u_small.shape
    nk = s // _C_TK
    bf_pad = jnp.zeros((1, LANES), F32).at[0, _FORGET_LANE0:_FORGET_LANE0 + C_HEADS].set(b_f)
    return pl.pallas_call(
        _fox_decay_kernel,
        grid=(b,),
        in_specs=[pl.BlockSpec((1, s, LANES), lambda bi: (bi, 0, _SMALL_FORGET)),
                  pl.BlockSpec((1, LANES), lambda bi: (0, 0))],
        out_specs=pl.BlockSpec((1, nk, _FORGET_ROWS, _C_TK), lambda bi: (bi, 0, 0, 0)),
        out_shape=jax.ShapeDtypeStruct((b, nk, _FORGET_ROWS, _C_TK), F32),
        compiler_params=_params("arbitrary"),
        name="fox_decay",
    )(u_small, bf_pad)


_C_TQ = 256


def _mixer_c_kernel(q_ref, k_ref, v_ref, c_ref, o_ref, s0_ref, s1_ref, p0_ref, p1_ref):
    row = _FORGET_LANE0 - _FORGET_ROW0 + pl.program_id(1)
    tq = _C_TQ
    nq = q_ref.shape[1] // tq
    r = lax.broadcasted_iota(jnp.int32, (tq, tq), 0)
    c = lax.broadcasted_iota(jnp.int32, (tq, tq), 1)
    s_bufs, p_bufs = (s0_ref, s1_ref), (p0_ref, p1_ref)
    denom = [None] * nq
    for step in range(nq + 2):
        i = step - 1
        if 0 <= i < nq:
            nk = (i + 1) * tq
            decay = _lane_cat([c_ref[0, j, pl.ds(row, 1), :] for j in range(nk // _C_TK)])
            z = s_bufs[i % 2][:, :nk] - decay * _INV_SCALE
            diag = jnp.where(c <= r, z[:, nk - tq:], NEG_INF)
            z = diag if i == 0 else _lane_cat([z[:, :nk - tq], diag])
            m = jnp.max(z, axis=1, keepdims=True)
            p = jnp.exp2((z - m) * _EXP2_SCALE)
            denom[i] = jnp.sum(p, axis=1, keepdims=True)
            p_bufs[i % 2][:, :nk] = p.astype(MXU_DTYPE)
        i = step
        if i < nq:
            nk = (i + 1) * tq
            s_bufs[i % 2][:, :nk] = _dot_nt(q_ref[0, i * tq:(i + 1) * tq, :], k_ref[0, :nk, :])
        i = step - 2
        if i >= 0:
            nk = (i + 1) * tq
            o = _dot(p_bufs[i % 2][:, :nk], v_ref[0, :nk, :]) / denom[i]
            o_ref[0, i * tq:(i + 1) * tq, :] = o.astype(o_ref.dtype)


def _mixer_c(u, cdec):
    b, s, _ = u.shape
    nk = s // _C_TK
    head = lambda col: pl.BlockSpec((1, s, HEAD_DIM), lambda bi, h: (bi, 0, col + h))
    return pl.pallas_call(
        _mixer_c_kernel,
        grid=(b, C_HEADS),
        in_specs=[head(_COL_C_Q), head(_COL_C_K), head(_COL_C_V),
                  pl.BlockSpec((1, nk, _FORGET_ROWS, _C_TK), lambda bi, h: (bi, 0, 0, 0))],
        out_specs=head(0),
        out_shape=jax.ShapeDtypeStruct((b, s, C_HEADS * HEAD_DIM), MXU_DTYPE),
        scratch_shapes=[pltpu.VMEM((_C_TQ, s), F32), pltpu.VMEM((_C_TQ, s), F32),
                        pltpu.VMEM((_C_TQ, s), MXU_DTYPE), pltpu.VMEM((_C_TQ, s), MXU_DTYPE)],
        compiler_params=_params("arbitrary", "arbitrary"),
        name="mixer_c",
    )(u, u, u, cdec)


def _gelu(x):
    return 0.5 * x * (1.0 + lax.erf(x * np.float32(math.sqrt(0.5))))


def _nsa_compress_kernel(k_ref, v_ref, pe_ref, w1_ref, b1_ref, w2_ref, b2_ref, kc_ref, vct_ref):
    n_chunk = k_ref.shape[1] // CMP_STRIDE
    halves = CMP_LEN // CMP_STRIDE
    res = []
    for c, src in enumerate((k_ref, v_ref)):
        parts = []
        for half in range(halves):
            acc = jnp.zeros((n_chunk, HEAD_DIM), F32)
            for p in range(CMP_STRIDE):
                pos = half * CMP_STRIDE + p
                rows = src[0, pl.ds(p, n_chunk, stride=CMP_STRIDE), :] + pe_ref[c, pos:pos + 1, :]
                acc = acc + _dot(rows.astype(MXU_DTYPE), w1_ref[c, pos])
            parts.append(acc)
        hid = parts[0]
        for half in range(1, halves):
            hid = hid + pltpu.roll(parts[half], n_chunk - half, 0)
        hid = _gelu(hid + b1_ref[c:c + 1, :])
        res.append(_dot(hid.astype(MXU_DTYPE), w2_ref[c]) + b2_ref[c:c + 1, :])
    kc_ref[0] = res[0].astype(kc_ref.dtype)
    vct_ref[0] = res[1].T.astype(vct_ref.dtype)


def _nsa_compress(u_small, cmp_pe, cmp_w1, cmp_b1, cmp_w2, cmp_b2, layer):
    b, s, _ = u_small.shape
    n_chunk = s // CMP_STRIDE
    of_layer = lambda a: pl.BlockSpec((None,) + a.shape[1:],
                                      lambda bi: (layer,) + (0,) * (a.ndim - 1))
    return pl.pallas_call(
        _nsa_compress_kernel,
        grid=(b,),
        in_specs=[pl.BlockSpec((1, s, LANES), lambda bi: (bi, 0, _SMALL_KC)),
                  pl.BlockSpec((1, s, LANES), lambda bi: (bi, 0, _SMALL_VC)),
                  of_layer(cmp_pe), of_layer(cmp_w1), of_layer(cmp_b1),
                  of_layer(cmp_w2), of_layer(cmp_b2)],
        out_specs=[pl.BlockSpec((1, n_chunk, HEAD_DIM), lambda bi: (bi, 0, 0)),
                   pl.BlockSpec((1, HEAD_DIM, n_chunk), lambda bi: (bi, 0, 0))],
        out_shape=[jax.ShapeDtypeStruct((b, n_chunk, HEAD_DIM), MXU_DTYPE),
                   jax.ShapeDtypeStruct((b, HEAD_DIM, n_chunk), MXU_DTYPE)],
        compiler_params=_params("arbitrary"),
        name="nsa_compress",
    )(u_small, u_small, cmp_pe, cmp_w1, cmp_b1, cmp_w2, cmp_b2)


def _mixer_b_kernel(q_ref, ks_ref, vs_ref, kw_ref, vw_ref, kc_ref, vct_ref, gl_ref, bg_ref,
                    tslc_ref, twin_ref, tcmp_ref, o_ref, vst_ref, vwt_ref,
                    sc_ref, pc_ref, ss_ref, ps_ref, sw_ref, pw_ref, add_ref, oc_ref):
    step = pl.program_id(1)
    nblk = vwt_ref.shape[0]
    nh = B_HEADS
    n_slc = ks_ref.shape[1] // SLC_LEN
    n_win = twin_ref.shape[0] // nh - 1
    per_blk = BLK // SLC_LEN

    @pl.when(step == 0)
    def _():
        for kb in range(nblk):
            rows = slice(kb * BLK, (kb + 1) * BLK)
            vst_ref[:, rows] = vs_ref[0, rows, :].astype(F32).T.astype(MXU_DTYPE)
            vwt_ref[kb] = vw_ref[0, rows, :].astype(F32).T.astype(MXU_DTYPE)

    def bias_rows(t_ref, dl, n_real):
        idx = jnp.where(dl >= 0, dl, n_real)
        return _lane_cat([t_ref[idx * nh + h] for h in range(nh)])

    def body(n_keys_blk):
        n_keys = n_keys_blk * BLK
        denom = {}

        def block_index(sub):
            i = step * _Q_PER_STEP + sub
            kb0 = jnp.maximum(i - (n_win - 1), 0)
            return i, kb0

        def score_stage(sub):
            _, kb0 = block_index(sub)
            q4 = jnp.concatenate([q_ref[0, sub * BLK:(sub + 1) * BLK, h * HEAD_DIM:(h + 1) * HEAD_DIM]
                                  for h in range(nh)], axis=0)
            sc_ref[sub] = _dot_nt(kc_ref[0], q4)
            ss_ref[sub, :n_keys, :] = _dot_nt(ks_ref[0, :n_keys, :], q4)
            start = pl.multiple_of(kb0 * BLK, BLK)
            sw_ref[sub] = _dot_nt(kw_ref[0, pl.ds(start, n_win * BLK), :], q4)

        def cmp_win_softmax_stage(sub):
            i, kb0 = block_index(sub)
            st = sc_ref[sub] + _lane_cat([tcmp_ref[sub * nh + h] for h in range(nh)])
            m = jnp.max(st, axis=0, keepdims=True)
            e = jnp.exp2((st - m) * _EXP2_SCALE)
            l = jnp.sum(e, axis=0, keepdims=True)
            pc_ref[sub] = jnp.where(m > 0.5 * NEG_INF, e / l, 0.0).astype(MXU_DTYPE)
            bias = jnp.concatenate([bias_rows(twin_ref, i - kb0 - j, n_win) for j in range(n_win)],
                                   axis=0)
            z = sw_ref[sub] + bias
            m = jnp.max(z, axis=0, keepdims=True)
            p = jnp.exp2((z - m) * _EXP2_SCALE)
            denom["win", sub] = jnp.sum(p, axis=0, keepdims=True)
            pw_ref[sub] = p.astype(MXU_DTYPE)

        def select_stage(sub):
            i, _ = block_index(sub)
            pc = pc_ref[sub]
            oc_ref[sub] = _dot(vct_ref[0], pc)
            n_cmp_pad = kc_ref.shape[1]
            jj = lax.broadcasted_iota(jnp.int32, (n_slc, n_cmp_pad), 0)
            nn = lax.broadcasted_iota(jnp.int32, (n_slc, n_cmp_pad), 1)
            overlap = ((nn * CMP_STRIDE <= jj * SLC_LEN + SLC_LEN - 1)
                       & (nn * CMP_STRIDE + CMP_LEN - 1 >= jj * SLC_LEN)).astype(MXU_DTYPE)
            imp4 = _dot(overlap, pc)
            imp = functools.reduce(lambda a, b: a + b,
                                   [imp4[:, h * BLK:(h + 1) * BLK] for h in range(nh)])
            t = i * BLK + lax.broadcasted_iota(jnp.int32, (n_slc, BLK), 1)
            blk = lax.broadcasted_iota(jnp.int32, (n_slc, BLK), 0)
            cur = t // SLC_LEN
            forced = (blk == 0) | (blk == cur) | (blk == cur - 1)
            causal = blk * SLC_LEN <= t
            score = jnp.where(forced, FORCE_SCORE, jnp.where(causal, imp, NEG_INF))
            rank = jnp.zeros((n_slc, BLK), jnp.int32)
            for r in range(n_slc):
                row = score[r:r + 1, :]
                ahead = (row > score) | ((row == score) & (r < blk))
                rank = rank + ahead.astype(jnp.int32)
            add = jnp.where(rank < min(N_SELECT, n_slc), 0.0, NEG_INF).astype(F32)
            add_ref[sub] = _lane_cat([add] * nh)

        def selected_softmax_stage(sub):
            i, _ = block_index(sub)
            mask_rows = []
            for kb in range(n_keys_blk):
                sel = jnp.concatenate(
                    [jnp.broadcast_to(add_ref[sub, kb * per_blk + j:kb * per_blk + j + 1, :],
                                      (SLC_LEN, nh * BLK)) for j in range(per_blk)], axis=0)
                mask_rows.append(bias_rows(tslc_ref, i - kb, nblk) + sel)
            z = ss_ref[sub, :n_keys, :] + jnp.concatenate(mask_rows, axis=0)
            m = jnp.max(z, axis=0, keepdims=True)
            p = jnp.exp2((z - m) * _EXP2_SCALE)
            denom["slc", sub] = jnp.sum(p, axis=0, keepdims=True)
            ps_ref[sub, :n_keys, :] = p.astype(MXU_DTYPE)

        def output_stage(sub):
            _, kb0 = block_index(sub)
            rows = slice(sub * BLK, (sub + 1) * BLK)
            o_slc = _dot(vst_ref[:, :n_keys], ps_ref[sub, :n_keys, :]) / denom["slc", sub]
            o_win = (_dot(_lane_cat([vwt_ref[kb0 + j] for j in range(n_win)]), pw_ref[sub])
                     / denom["win", sub])
            gt = jax.nn.sigmoid(gl_ref[0, rows, :] + bg_ref[...]).T

            def gate(br):
                r0 = _GATE_LANE0 + br * nh
                return _lane_cat([gt[r0 + h:r0 + h + 1, :] for h in range(nh)])

            o_t = gate(0) * oc_ref[sub] + gate(1) * o_slc + gate(2) * o_win
            for h in range(nh):
                o_ref[0, rows, h * HEAD_DIM:(h + 1) * HEAD_DIM] = (
                    o_t[:, h * BLK:(h + 1) * BLK].T.astype(o_ref.dtype))

        stages = (score_stage, cmp_win_softmax_stage, select_stage, selected_softmax_stage,
                  output_stage)
        for t in range(len(stages) + _Q_PER_STEP - 1):
            for sub in range(_Q_PER_STEP):
                if 0 <= t - sub < len(stages):
                    stages[t - sub](sub)

    if nblk % _CAUSAL_STEP == 0 and _CAUSAL_STEP % _Q_PER_STEP == 0:
        for v in range(nblk // _CAUSAL_STEP):
            pl.when(step * _Q_PER_STEP // _CAUSAL_STEP == v)(
                functools.partial(body, (v + 1) * _CAUSAL_STEP))
    else:
        body(nblk)


def _mixer_b(u, u_small, kc, vct, b_nsa_gate, tiles, nblk):
    b, s, _ = u.shape
    nh = B_HEADS
    bg_pad = jnp.zeros((1, LANES), F32).at[0, _GATE_LANE0:_GATE_LANE0 + 3 * nh].set(b_nsa_gate)
    whole = lambda a: pl.BlockSpec(a.shape, lambda bi, i: (0, 0, 0))
    kv = lambda col: pl.BlockSpec((1, s, HEAD_DIM), lambda bi, i: (bi, 0, col))
    n_chunk = kc.shape[1]
    q, tq, wq = _Q_PER_STEP, _Q_PER_STEP * BLK, nh * BLK
    n_win = WIN // BLK + 1
    return pl.pallas_call(
        _mixer_b_kernel,
        grid=(b, nblk // _Q_PER_STEP),
        in_specs=[pl.BlockSpec((1, tq, nh * HEAD_DIM), lambda bi, i: (bi, i, _COL_B_Q // nh)),
                  kv(_COL_B_KS), kv(_COL_B_VS), kv(_COL_B_KW), kv(_COL_B_VW),
                  pl.BlockSpec((1, n_chunk, HEAD_DIM), lambda bi, i: (bi, 0, 0)),
                  pl.BlockSpec((1, HEAD_DIM, n_chunk), lambda bi, i: (bi, 0, 0)),
                  pl.BlockSpec((1, tq, LANES), lambda bi, i: (bi, i, _SMALL_GATE)),
                  pl.BlockSpec((1, LANES), lambda bi, i: (0, 0)),
                  whole(tiles["slc"]), whole(tiles["win"]),
                  pl.BlockSpec((_Q_PER_STEP * nh, BLK, BLK), lambda bi, i: (i, 0, 0))],
        out_specs=pl.BlockSpec((1, tq, nh * HEAD_DIM), lambda bi, i: (bi, i, 0)),
        out_shape=jax.ShapeDtypeStruct((b, s, nh * HEAD_DIM), MXU_DTYPE),
        scratch_shapes=[pltpu.VMEM((HEAD_DIM, s), MXU_DTYPE),
                        pltpu.VMEM((nblk, HEAD_DIM, BLK), MXU_DTYPE),
                        pltpu.VMEM((q, n_chunk, wq), F32), pltpu.VMEM((q, n_chunk, wq), MXU_DTYPE),
                        pltpu.VMEM((q, s, wq), F32), pltpu.VMEM((q, s, wq), MXU_DTYPE),
                        pltpu.VMEM((q, n_win * BLK, wq), F32),
                        pltpu.VMEM((q, n_win * BLK, wq), MXU_DTYPE),
                        pltpu.VMEM((q, s // SLC_LEN, wq), F32),
                        pltpu.VMEM((q, HEAD_DIM, wq), F32)],
        compiler_params=_params("arbitrary", "arbitrary"),
        name="mixer_b",
    )(u, u, u, u, u, kc, vct, u_small, bg_pad, tiles["slc"], tiles["win"], tiles["cmp"])


def _layer_norm(y, g, b):
    mu = jnp.mean(y, axis=-1, keepdims=True)
    yc = y - mu
    var = jnp.mean(yc * yc, axis=-1, keepdims=True)
    return yc * lax.rsqrt(var + LN_EPS) * g + b


def _merge_kernel(x_ref, oa_ref, ob_ref, oc_ref, wga_ref, wgb_ref, wgc_ref, bga_ref, bgb_ref,
                  bgc_ref, wpa_ref, wpb_ref, wpc_ref, wo_ref, g_ref, b_ref, y_ref, xb_ref, *,
                  alpha):
    j = pl.program_id(1)

    @pl.when(j == 0)
    def _():
        xb_ref[...] = x_ref[...].astype(MXU_DTYPE)
        y_ref[...] = jnp.zeros_like(y_ref)

    xb = xb_ref[...]
    mixed = None
    for o_ref, wg_ref, bg_ref, wp_ref in ((oa_ref, wga_ref, bga_ref, wpa_ref),
                                          (ob_ref, wgb_ref, bgb_ref, wpb_ref),
                                          (oc_ref, wgc_ref, bgc_ref, wpc_ref)):
        gate = jax.nn.sigmoid(_dot(xb, wg_ref[...]) + bg_ref[...])
        term = gate * _dot(o_ref[...], wp_ref[...])
        mixed = term if mixed is None else mixed + term
    y_ref[...] += _dot(mixed.astype(MXU_DTYPE), wo_ref[...])

    @pl.when(j == pl.num_programs(1) - 1)
    def _():
        y_ref[...] = _layer_norm(alpha * x_ref[...] + y_ref[...], g_ref[...], b_ref[...])


def _merge_ln(x2, oa, ob, oc, w_gate, b_gate, w_pa, w_pb, w_pc, w_out, ln_g, ln_b, layer, alpha,
              tm, tn):
    t, d = x2.shape
    nj = d // tn
    row = lambda k: pl.BlockSpec((tm, k), lambda i, j: (i, 0))
    colw = lambda k, shift: pl.BlockSpec((None, k, tn), lambda i, j: (layer, 0, shift * nj + j))
    vec = lambda shift: pl.BlockSpec((None, 1, tn), lambda i, j: (layer, 0, shift * nj + j))
    fullvec = pl.BlockSpec((None, 1, d), lambda i, j: (layer, 0, 0))
    return pl.pallas_call(
        functools.partial(_merge_kernel, alpha=alpha),
        grid=(t // tm, nj),
        in_specs=[row(d), row(oa.shape[1]), row(ob.shape[1]), row(oc.shape[1]),
                  colw(d, 0), colw(d, 1), colw(d, 2), vec(0), vec(1), vec(2),
                  colw(w_pa.shape[1], 0), colw(w_pb.shape[1], 0), colw(w_pc.shape[1], 0),
                  pl.BlockSpec((None, tn, d), lambda i, j: (layer, j, 0)), fullvec, fullvec],
        out_specs=pl.BlockSpec((tm, d), lambda i, j: (i, 0)),
        out_shape=jax.ShapeDtypeStruct((t, d), F32),
        scratch_shapes=[pltpu.VMEM((tm, d), MXU_DTYPE)],
        compiler_params=_params("arbitrary", "arbitrary"),
        name="merge_ln",
    )(x2, oa, ob, oc, w_gate, w_gate, w_gate, b_gate, b_gate, b_gate, w_pa, w_pb, w_pc, w_out,
      ln_g, ln_b)


_TAIL = 8


def _ffn_kernel(x_ref, wa_ref, wb_ref, cwa_ref, cwb_ref, cba_ref, cbb_ref, wd_ref, g_ref, b_ref,
                y_ref, xb_ref, ha_ref, hb_ref, tail_ref, *, alpha, tiles_per_seq):
    i = pl.program_id(0)
    j = pl.program_id(1)
    tm = x_ref.shape[0]
    tf = wa_ref.shape[1]

    @pl.when(j == 0)
    def _():
        xb_ref[...] = x_ref[...].astype(MXU_DTYPE)
        y_ref[...] = jnp.zeros_like(y_ref)

    @pl.when(i % tiles_per_seq == 0)
    def _():
        tail_ref[j] = jnp.zeros(tail_ref.shape[1:], F32)

    xb = xb_ref[...]
    for h_ref, w_ref, c0 in ((ha_ref, wa_ref, 0), (hb_ref, wb_ref, tf)):
        h_ref[0:_TAIL, :] = tail_ref[j, :, c0:c0 + tf]
        h_ref[_TAIL:, :] = _dot(xb, w_ref[...])
        tail_ref[j, :, c0:c0 + tf] = h_ref[tm:tm + _TAIL, :]

    def conv(h_ref, cw_ref, cb_ref):
        out = cb_ref[...]
        for tap in range(CONV_W):
            shift = CONV_W - 1 - tap
            out = out + cw_ref[tap:tap + 1, :] * h_ref[pl.ds(_TAIL - shift, tm), :]
        return out

    a = conv(ha_ref, cwa_ref, cba_ref)
    bb = conv(hb_ref, cwb_ref, cbb_ref)
    y_ref[...] += _dot((_gelu(a) * bb).astype(MXU_DTYPE), wd_ref[...])

    @pl.when(j == pl.num_programs(1) - 1)
    def _():
        y_ref[...] = _layer_norm(alpha * x_ref[...] + y_ref[...], g_ref[...], b_ref[...])


def _ffn_ln(x2, w_up, conv_w, conv_b, w_down, ln_g, ln_b, layer, alpha, seq, tm, tf):
    t, d = x2.shape
    f = w_down.shape[1]
    nj = f // tf
    assert seq % tm == 0 and CONV_W - 1 <= _TAIL
    half = lambda k, shift: pl.BlockSpec((None, k, tf), lambda i, j: (layer, 0, shift * nj + j))
    fullvec = pl.BlockSpec((None, 1, d), lambda i, j: (layer, 0, 0))
    return pl.pallas_call(
        functools.partial(_ffn_kernel, alpha=alpha, tiles_per_seq=seq // tm),
        grid=(t // tm, nj),
        in_specs=[pl.BlockSpec((tm, d), lambda i, j: (i, 0)),
                  half(d, 0), half(d, 1), half(CONV_W, 0), half(CONV_W, 1), half(1, 0), half(1, 1),
                  pl.BlockSpec((None, tf, d), lambda i, j: (layer, j, 0)), fullvec, fullvec],
        out_specs=pl.BlockSpec((tm, d), lambda i, j: (i, 0)),
        out_shape=jax.ShapeDtypeStruct((t, d), F32),
        scratch_shapes=[pltpu.VMEM((tm, d), MXU_DTYPE),
                        pltpu.VMEM((tm + _TAIL, tf), F32),
                        pltpu.VMEM((tm + _TAIL, tf), F32),
                        pltpu.VMEM((nj, _TAIL, 2 * tf), F32)],
        compiler_params=_params("arbitrary", "arbitrary"),
        name="ffn_ln",
    )(x2, w_up, w_up, conv_w, conv_w, conv_b, conv_b, w_down, ln_g, ln_b)


def _pack_moves():
    main_dst = {'a_q': _COL_A_Q, 'a_k': _COL_A_K, 'a_v': _COL_A_V,
                'c_q': _COL_C_Q, 'c_k': _COL_C_K, 'c_v': _COL_C_V, 'b_q': _COL_B_Q,
                'b_k_slc': _COL_B_KS, 'b_v_slc': _COL_B_VS, 'b_k_win': _COL_B_KW,
                'b_v_win': _COL_B_VW}
    moves = [(blk * HEAD_DIM, _SPLIT[n][0], _SPLIT[n][1] - _SPLIT[n][0])
             for n, blk in main_dst.items()]
    moves += [(_N_MAIN + _SMALL_KC * LANES, _SPLIT['b_k_cmp'][0], LANES),
              (_N_MAIN + _SMALL_VC * LANES, _SPLIT['b_v_cmp'][0], LANES),
              (_N_MAIN + _SMALL_GATE * LANES, _GATE_COL0, LANES),
              (_N_MAIN + _SMALL_FORGET * LANES, _FORGET_COL0, N_IN - _FORGET_COL0)]
    return moves


def _pack_kernel(w_ref, o_ref):
    tail0 = _N_MAIN + _SMALL_FORGET * LANES
    o_ref[0, :, tail0:] = jnp.zeros((o_ref.shape[1], o_ref.shape[2] - tail0), o_ref.dtype)
    for d0, s0, width in _pack_moves():
        o_ref[0, :, d0:d0 + width] = w_ref[0, :, s0:s0 + width].astype(o_ref.dtype)


def _pack_w_in(w_in, tk):
    depth, d, n_in = w_in.shape
    assert n_in == N_IN
    n_out = _N_MAIN + _N_SMALL
    return pl.pallas_call(
        _pack_kernel,
        grid=(depth, d // tk),
        in_specs=[pl.BlockSpec((1, tk, n_in), lambda l, i: (l, i, 0))],
        out_specs=pl.BlockSpec((1, tk, n_out), lambda l, i: (l, i, 0)),
        out_shape=jax.ShapeDtypeStruct((depth, d, n_out), MXU_DTYPE),
        compiler_params=_params("arbitrary", "arbitrary"),
        name="pack_w_in",
    )(w_in)


_TM_PROJ, _TN_PROJ = 1024, 1536
_TM_MERGE, _TN_MERGE = 512, 512
_TM_FFN, _TF_FFN = 512, 512


def kernel(x, rel_bias, w_in, b_f, b_nsa_gate, cmp_pe, cmp_w1, cmp_b1, cmp_w2, cmp_b2, w_gate, b_gate, w_pa, w_pb, w_pc, w_out, ln1_g, ln1_b, w_up, conv_w, conv_b, w_down, ln2_g, ln2_b):
    bsz, seq, d = x.shape
    depth = w_in.shape[0]
    nblk = seq // BLK
    alpha = (2 * depth) ** 0.25
    t = bsz * seq
    bf = lambda a: a.astype(MXU_DTYPE)
    row = lambda a: a[:, None, :]

    tiles = _bias_tiles(rel_bias, nblk)
    w_in_p = _pack_w_in(w_in, 256)
    cmp_w1_b = bf(cmp_w1).reshape(depth, 2, CMP_LEN, HEAD_DIM, HEAD_DIM)
    cmp_w2_b = bf(cmp_w2)
    w_gate_b, w_pa_b, w_pb_b, w_pc_b, w_out_b = (bf(w_gate), bf(w_pa), bf(w_pb), bf(w_pc),
                                                 bf(w_out))
    w_up_b, w_down_b = bf(w_up), bf(w_down)
    x2 = x.reshape(t, d)
    for l in range(depth):
        u, u_small = _proj(x2, w_in_p, l, _TM_PROJ, _TN_PROJ)
        u = u.reshape(bsz, seq, -1)
        u_small = u_small.reshape(bsz, seq, -1)

        o_a = _mixer_a(u, tiles["a"], nblk)
        kc, vct = _nsa_compress(u_small, cmp_pe, cmp_w1_b, cmp_b1, cmp_w2_b, cmp_b2, l)
        o_b = _mixer_b(u, u_small, kc, vct, b_nsa_gate[l], tiles, nblk)
        cdec = _fox_decay(u_small, b_f[l])
        o_c = _mixer_c(u, cdec)

        x1 = _merge_ln(x2, o_a.reshape(t, -1), o_b.reshape(t, -1), o_c.reshape(t, -1),
                       w_gate_b, row(b_gate), w_pa_b, w_pb_b, w_pc_b, w_out_b,
                       row(ln1_g), row(ln1_b), l, alpha, _TM_MERGE, _TN_MERGE)
        x2 = _ffn_ln(x1, w_up_b, conv_w, row(conv_b), w_down_b, row(ln2_g), row(ln2_b),
                     l, alpha, seq, _TM_FFN, _TF_FFN)
    return x2.reshape(bsz, seq, d)
```

```python
import functools
import math

import numpy as np
import jax
import jax.numpy as jnp
from jax import lax
from jax.experimental import pallas as pl
from jax.experimental.pallas import tpu as pltpu

HEAD_DIM = 128
A_GROUPS = ((128, 1), (512, 4), (2048, 16))
A_HEADS_PER_GROUP = 2
A_HEADS = A_HEADS_PER_GROUP * len(A_GROUPS)
B_HEADS = 4
CMP_LEN = 32
CMP_STRIDE = 16
SLC_LEN = 64
N_SELECT = 16
WIN = 512
C_HEADS = 6
N_BUCKETS = 32
MAX_DISTANCE = 2048
CONV_W = 3
LN_EPS = 1e-5
NEG_INF = -1e30
FORCE_SCORE = 1e9
ATTN_SCALE = HEAD_DIM ** -0.5
_INV_SCALE = np.float32(1.0 / ATTN_SCALE)
_EXP2_SCALE = np.float32(ATTN_SCALE * math.log2(math.e))

LANES = 128
VMEM_LIMIT_BYTES = 56 * 2 ** 20

MXU_DTYPE = jnp.bfloat16
F32 = jnp.float32
BLK = LANES

_COL_A_Q, _COL_A_K, _COL_A_V = 0, 6, 12
_COL_C_Q, _COL_C_K, _COL_C_V = 18, 24, 30
_COL_B_Q = 36
_COL_B_KS, _COL_B_VS, _COL_B_KW, _COL_B_VW = 40, 41, 42, 43
_N_COL_BLOCKS = 44


def _in_split_offsets():
    widths = (('a_q', A_HEADS * HEAD_DIM), ('a_k', A_HEADS * HEAD_DIM), ('a_v', A_HEADS * HEAD_DIM),
              ('b_q', B_HEADS * HEAD_DIM), ('b_k_cmp', HEAD_DIM), ('b_v_cmp', HEAD_DIM),
              ('b_k_slc', HEAD_DIM), ('b_v_slc', HEAD_DIM), ('b_k_win', HEAD_DIM),
              ('b_v_win', HEAD_DIM), ('b_gate', 3 * B_HEADS),
              ('c_q', C_HEADS * HEAD_DIM), ('c_k', C_HEADS * HEAD_DIM), ('c_v', C_HEADS * HEAD_DIM),
              ('c_f', C_HEADS))
    out, o = {}, 0
    for name, w in widths:
        out[name] = (o, o + w)
        o += w
    return out, o


_SPLIT, N_IN = _in_split_offsets()
_SMALL_KC, _SMALL_VC, _SMALL_GATE, _SMALL_FORGET = 0, 1, 2, 3
_N_MAIN = _N_COL_BLOCKS * HEAD_DIM
_N_SMALL = 4 * LANES
_GATE_COL0 = _SPLIT['b_gate'][0] // LANES * LANES
_GATE_LANE0 = _SPLIT['b_gate'][0] - _GATE_COL0
_FORGET_COL0 = _SPLIT['c_f'][0] // LANES * LANES
_FORGET_LANE0 = _SPLIT['c_f'][0] - _FORGET_COL0
assert _SPLIT['b_gate'][1] - _GATE_COL0 <= LANES and N_IN - _FORGET_COL0 <= LANES
_FORGET_ROW0 = _FORGET_LANE0 // 8 * 8
_FORGET_ROWS = -(-(_FORGET_LANE0 + C_HEADS - _FORGET_ROW0) // 8) * 8


def _params(*sem):
    return pltpu.CompilerParams(dimension_semantics=sem, vmem_limit_bytes=VMEM_LIMIT_BYTES)


def _dot(a, b):
    return jnp.dot(a, b, preferred_element_type=F32)


def _dot_nt(a, b):
    return lax.dot_general(a, b, (((1,), (1,)), ((), ())), preferred_element_type=F32)


def _bias_tile_kernel(base_ref, rmul_ref, cmul_ref, dmask_ref, dmax_ref, rmax_ref, head_ref,
                      rel_ref, out_ref):
    row = lax.broadcasted_iota(jnp.int32, (BLK, BLK), 0)
    col = lax.broadcasted_iota(jnp.int32, (BLK, BLK), 1)
    max_exact = N_BUCKETS // 2
    per_step = out_ref.shape[0]
    for k in range(per_step):
        t = pl.program_id(0) * per_step + k
        dist = base_ref[t] + rmul_ref[t] * row + cmul_ref[t] * col
        valid = ((dist >= 0) & (dist <= dmax_ref[t]) & ((dist & dmask_ref[t]) == 0)
                 & (row < rmax_ref[t]))
        d = jnp.maximum(dist, 0)
        log_ratio = (jnp.log(jnp.maximum(d, 1).astype(F32) / max_exact)
                     / math.log(MAX_DISTANCE / max_exact))
        large = jnp.minimum(max_exact + (log_ratio * (N_BUCKETS - max_exact)).astype(jnp.int32),
                            N_BUCKETS - 1)
        bucket = jnp.where(d < max_exact, d, large)
        h = head_ref[t]
        val = jnp.zeros((BLK, BLK), F32)
        for b in range(N_BUCKETS):
            val = jnp.where(bucket == b, rel_ref[b, h], val)
        out_ref[k] = jnp.where(valid, val * _INV_SCALE, NEG_INF)


def _a_tile_counts(nblk):
    return [min(window // BLK + 1, nblk) for window, _ in A_GROUPS]


def _tile_tables(nblk):
    big = 2 ** 30
    fam = {}
    fam["a"] = [(dl * BLK, 1, -1, dil - 1, window if dl < cnt else -1, BLK,
                 g * A_HEADS_PER_GROUP + hh)
                for g, ((window, dil), cnt) in enumerate(zip(A_GROUPS, _a_tile_counts(nblk)))
                for hh in range(A_HEADS_PER_GROUP)
                for dl in range(cnt + 1)]
    fam["slc"] = [(dl * BLK, -1, 1, 0, big if dl < nblk else -1, BLK, A_HEADS + h)
                  for dl in range(nblk + 1) for h in range(B_HEADS)]
    n_win = WIN // BLK + 1
    fam["win"] = [(dl * BLK, -1, 1, 0, WIN - 1 if dl < n_win else -1, BLK, A_HEADS + h)
                  for dl in range(n_win + 1) for h in range(B_HEADS)]
    n_cmp = (nblk * BLK - CMP_LEN) // CMP_STRIDE + 1
    fam["cmp"] = [(i * BLK - (CMP_LEN - 1), -CMP_STRIDE, 1, 0, big, n_cmp, A_HEADS + h)
                  for i in range(nblk) for h in range(B_HEADS)]
    return {k: [np.asarray(c, np.int32) for c in zip(*rows)] for k, rows in fam.items()}


def _a_tile_offsets(nblk):
    cnt = _a_tile_counts(nblk)
    return [A_HEADS_PER_GROUP * sum(c + 1 for c in cnt[:g]) for g in range(len(A_GROUPS))]


_MAX_TILES_PER_STEP = 17


def _bias_tiles(rel_bias, nblk):
    out = {}
    for name, cols in _tile_tables(nblk).items():
        n = cols[0].shape[0]
        per_step = max(k for k in range(1, _MAX_TILES_PER_STEP + 1) if n % k == 0)
        out[name] = pl.pallas_call(
            _bias_tile_kernel,
            grid_spec=pltpu.PrefetchScalarGridSpec(
                num_scalar_prefetch=len(cols),
                grid=(n // per_step,),
                in_specs=[pl.BlockSpec(memory_space=pltpu.SMEM)],
                out_specs=pl.BlockSpec((per_step, BLK, BLK), lambda t, *_: (t, 0, 0)),
            ),
            out_shape=jax.ShapeDtypeStruct((n, BLK, BLK), F32),
            compiler_params=_params("arbitrary"),
            name="bias_tiles_" + name,
        )(*[jnp.asarray(c) for c in cols], rel_bias)
    return out


def _proj_kernel(x_ref, w_ref, o_ref, os_ref, xb_ref):
    j = pl.program_id(1)
    tn = w_ref.shape[1]
    n_side = os_ref.shape[1]

    @pl.when(j == 0)
    def _():
        xb_ref[...] = x_ref[...].astype(MXU_DTYPE)

    acc = _dot(xb_ref[...], w_ref[...])
    o_ref[...] = acc.astype(o_ref.dtype)

    @pl.when(j == pl.num_programs(1) - 1)
    def _():
        os_ref[...] = acc[:, tn - n_side:]


def _proj(x2, w_packed, layer, tm, tn):
    t, d = x2.shape
    n_all = _N_MAIN + _N_SMALL
    assert n_all % tn == 0 and tn >= _N_SMALL
    nj = n_all // tn
    return pl.pallas_call(
        _proj_kernel,
        grid=(t // tm, nj),
        in_specs=[pl.BlockSpec((tm, d), lambda i, j: (i, 0)),
                  pl.BlockSpec((None, d, tn), lambda i, j: (layer, 0, j))],
        out_specs=[pl.BlockSpec((tm, tn), lambda i, j: (i, j)),
                   pl.BlockSpec((tm, _N_SMALL), lambda i, j: (i, 0))],
        out_shape=[jax.ShapeDtypeStruct((t, _N_MAIN), MXU_DTYPE),
                   jax.ShapeDtypeStruct((t, _N_SMALL), F32)],
        scratch_shapes=[pltpu.VMEM((tm, d), MXU_DTYPE)],
        compiler_params=_params("arbitrary", "arbitrary"),
        name="in_proj",
    )(x2, w_packed)


def _lane_cat(xs):
    return jnp.concatenate(xs, axis=1)


def _softmax_pv(z, v, axis):
    m = jnp.max(z, axis=axis, keepdims=True)
    p = jnp.exp2((z - m) * _EXP2_SCALE)
    l = jnp.sum(p, axis=axis, keepdims=True)
    if axis == 1:
        o = _dot(p.astype(MXU_DTYPE), v)
    else:
        o = _dot(v, p.astype(MXU_DTYPE))
    return o / l, m, l


_CAUSAL_STEP = 4
_CAUSAL_STEP_B = 2
_QA_PER_STEP = 4
_Q_PER_STEP = 2


def _mixer_a_kernel(q_ref, k_ref, v_ref, t_ref, o_ref, s0_ref, s1_ref, p0_ref, p1_ref, *,
                    tile_off, tile_cnt):
    hpg = A_HEADS_PER_GROUP
    n_groups = len(A_GROUPS)
    nblk = k_ref.shape[1] // BLK
    step = pl.program_id(1)
    s_bufs, p_bufs = (s0_ref, s1_ref), (p0_ref, p1_ref)

    def body(n_full):
        chains = [(sub, g, hh) for sub in range(_QA_PER_STEP) for g in range(n_groups)
                  for hh in range(hpg)]

        def window(sub, g):
            i = step * _QA_PER_STEP + sub
            if tile_cnt[g] == nblk:
                return i, 0, 0, n_full
            kb0 = jnp.maximum(i - (tile_cnt[g] - 1), 0)
            return i, kb0, pl.multiple_of(kb0 * BLK, BLK), tile_cnt[g]

        stats = [None] * len(chains)
        results = {}
        for t in range(len(chains) + 2):
            if 0 <= t - 1 < len(chains):
                c = t - 1
                sub, g, hh = chains[c]
                i, kb0, _, nkb = window(sub, g)
                nd = tile_cnt[g]
                tbase = tile_off[g] + hh * (nd + 1)
                deltas = [i - kb0 - j for j in range(nkb)]
                bias = _lane_cat([t_ref[tbase + jnp.where(dl >= 0, dl, nd)] for dl in deltas])
                z = s_bufs[c % 2][:, :nkb * BLK] + bias
                m = jnp.max(z, axis=1, keepdims=True)
                p = jnp.exp2((z - m) * _EXP2_SCALE)
                stats[c] = (m, jnp.sum(p, axis=1, keepdims=True))
                p_bufs[c % 2][:, :nkb * BLK] = p.astype(MXU_DTYPE)
            if t < len(chains):
                sub, g, hh = chains[t]
                _, _, start, nkb = window(sub, g)
                c0 = (g * hpg + hh) * HEAD_DIM
                q = q_ref[0, sub * BLK:(sub + 1) * BLK, c0:c0 + HEAD_DIM]
                k = k_ref[0, pl.ds(start, nkb * BLK), c0:c0 + HEAD_DIM]
                s_bufs[t % 2][:, :nkb * BLK] = _dot_nt(q, k)
            if t - 2 >= 0:
                c = t - 2
                sub, g, hh = chains[c]
                _, _, start, nkb = window(sub, g)
                c0 = (g * hpg + hh) * HEAD_DIM
                v = v_ref[0, pl.ds(start, nkb * BLK), c0:c0 + HEAD_DIM]
                m, l = stats[c]
                results[chains[c]] = (_dot(p_bufs[c % 2][:, :nkb * BLK], v) / l,
                                      m * ATTN_SCALE + jnp.log(l))
                if (g, hh) == (n_groups - 1, hpg - 1):
                    for h2 in range(hpg):
                        outs = [results[(sub, g2, h2)][0] for g2 in range(n_groups)]
                        lses = [results[(sub, g2, h2)][1] for g2 in range(n_groups)]
                        mx = functools.reduce(jnp.maximum, lses)
                        es = [jnp.exp(x - mx) for x in lses]
                        tot = functools.reduce(lambda a, b: a + b, es)
                        o = functools.reduce(lambda a, b: a + b,
                                             [(e / tot) * og for e, og in zip(es, outs)])
                        o_ref[0, sub * BLK:(sub + 1) * BLK, h2 * HEAD_DIM:(h2 + 1) * HEAD_DIM] = (
                            o.astype(o_ref.dtype))

    if nblk % _CAUSAL_STEP == 0 and _CAUSAL_STEP % _QA_PER_STEP == 0:
        for v in range(nblk // _CAUSAL_STEP):
            pl.when(step * _QA_PER_STEP // _CAUSAL_STEP == v)(
                functools.partial(body, (v + 1) * _CAUSAL_STEP))
    else:
        body(nblk)


def _mixer_a(u, tiles_a, nblk):
    b, s, _ = u.shape
    w = A_HEADS * HEAD_DIM
    kern = functools.partial(_mixer_a_kernel, tile_off=tuple(_a_tile_offsets(nblk)),
                             tile_cnt=tuple(_a_tile_counts(nblk)))
    n_a = tiles_a.shape[0]
    tq = _QA_PER_STEP * BLK
    return pl.pallas_call(
        kern,
        grid=(b, nblk // _QA_PER_STEP),
        in_specs=[pl.BlockSpec((1, tq, w), lambda bi, i: (bi, i, _COL_A_Q // A_HEADS)),
                  pl.BlockSpec((1, s, w), lambda bi, i: (bi, 0, _COL_A_K // A_HEADS)),
                  pl.BlockSpec((1, s, w), lambda bi, i: (bi, 0, _COL_A_V // A_HEADS)),
                  pl.BlockSpec((n_a, BLK, BLK), lambda bi, i: (0, 0, 0))],
        out_specs=pl.BlockSpec((1, tq, A_HEADS_PER_GROUP * HEAD_DIM), lambda bi, i: (bi, i, 0)),
        out_shape=jax.ShapeDtypeStruct((b, s, A_HEADS_PER_GROUP * HEAD_DIM), MXU_DTYPE),
        scratch_shapes=[pltpu.VMEM((BLK, s), F32), pltpu.VMEM((BLK, s), F32),
                        pltpu.VMEM((BLK, s), MXU_DTYPE), pltpu.VMEM((BLK, s), MXU_DTYPE)],
        compiler_params=_params("arbitrary", "arbitrary"),
        name="mixer_a",
    )(u, u, u, tiles_a)


_C_TK = 256


def _split3(x):
    hi = x.astype(MXU_DTYPE)
    r1 = x - hi.astype(F32)
    mid = r1.astype(MXU_DTYPE)
    lo = (r1 - mid.astype(F32)).astype(MXU_DTYPE)
    return hi, mid, lo


def _fox_decay_kernel(f_ref, bf_ref, c_ref):
    x = f_ref[0] + bf_ref[...]
    logf = jnp.minimum(x, 0.0) - jnp.log1p(jnp.exp(-jnp.abs(x)))
    logf_t = logf.T[_FORGET_ROW0:_FORGET_ROW0 + _FORGET_ROWS]
    nk = c_ref.shape[1]
    r = lax.broadcasted_iota(jnp.int32, (_C_TK, _C_TK), 0)
    c = lax.broadcasted_iota(jnp.int32, (_C_TK, _C_TK), 1)
    upper = (r <= c).astype(MXU_DTYPE)
    carry = jnp.zeros((_FORGET_ROWS, 1), F32)
    for j in range(nk):
        hi, mid, lo = _split3(logf_t[:, j * _C_TK:(j + 1) * _C_TK])
        cs = (_dot(hi, upper) + _dot(mid, upper)) + _dot(lo, upper) + carry
        c_ref[0, j] = cs
        carry = cs[:, _C_TK - 1:_C_TK]


def _fox_decay(u_small, b_f):
    b, s, _ = u_small.shape
    nk = s // _C_TK
    bf_pad = jnp.zeros((1, LANES), F32).at[0, _FORGET_LANE0:_FORGET_LANE0 + C_HEADS].set(b_f)
    return pl.pallas_call(
        _fox_decay_kernel,
        grid=(b,),
        in_specs=[pl.BlockSpec((1, s, LANES), lambda bi: (bi, 0, _SMALL_FORGET)),
                  pl.BlockSpec((1, LANES), lambda bi: (0, 0))],
        out_specs=pl.BlockSpec((1, nk, _FORGET_ROWS, _C_TK), lambda bi: (bi, 0, 0, 0)),
        out_shape=jax.ShapeDtypeStruct((b, nk, _FORGET_ROWS, _C_TK), F32),
        compiler_params=_params("arbitrary"),
        name="fox_decay",
    )(u_small, bf_pad)


_C_TQ = 256
_C_HEADS_PER_STEP = 2


def _mixer_c_kernel(q_ref, k_ref, v_ref, c_ref, o_ref, s0_ref, s1_ref, p0_ref, p1_ref):
    row0 = _FORGET_LANE0 - _FORGET_ROW0 + pl.program_id(1) * _C_HEADS_PER_STEP
    tq = _C_TQ
    nq = q_ref.shape[1] // tq
    r = lax.broadcasted_iota(jnp.int32, (tq, tq), 0)
    c = lax.broadcasted_iota(jnp.int32, (tq, tq), 1)
    s_bufs, p_bufs = (s0_ref, s1_ref), (p0_ref, p1_ref)
    chains = [(hh, i) for hh in range(_C_HEADS_PER_STEP) for i in range(nq)]
    denom = [None] * len(chains)
    for t in range(len(chains) + 2):
        if 0 <= t - 1 < len(chains):
            n = t - 1
            hh, i = chains[n]
            nk = (i + 1) * tq
            decay = _lane_cat([c_ref[0, j, pl.ds(row0 + hh, 1), :] for j in range(nk // _C_TK)])
            z = s_bufs[n % 2][:, :nk] - decay * _INV_SCALE
            diag = jnp.where(c <= r, z[:, nk - tq:], NEG_INF)
            z = diag if i == 0 else _lane_cat([z[:, :nk - tq], diag])
            m = jnp.max(z, axis=1, keepdims=True)
            p = jnp.exp2((z - m) * _EXP2_SCALE)
            denom[n] = jnp.sum(p, axis=1, keepdims=True)
            p_bufs[n % 2][:, :nk] = p.astype(MXU_DTYPE)
        if t < len(chains):
            hh, i = chains[t]
            nk = (i + 1) * tq
            cols = slice(hh * HEAD_DIM, (hh + 1) * HEAD_DIM)
            s_bufs[t % 2][:, :nk] = _dot_nt(q_ref[0, i * tq:(i + 1) * tq, cols],
                                           k_ref[0, :nk, cols])
        if t - 2 >= 0:
            n = t - 2
            hh, i = chains[n]
            nk = (i + 1) * tq
            cols = slice(hh * HEAD_DIM, (hh + 1) * HEAD_DIM)
            o = _dot(p_bufs[n % 2][:, :nk], v_ref[0, :nk, cols]) / denom[n]
            o_ref[0, i * tq:(i + 1) * tq, cols] = o.astype(o_ref.dtype)


def _mixer_c(u, cdec):
    b, s, _ = u.shape
    nk = s // _C_TK
    hps = _C_HEADS_PER_STEP
    assert C_HEADS % hps == 0 and all(c % hps == 0 for c in (_COL_C_Q, _COL_C_K, _COL_C_V))
    head = lambda col: pl.BlockSpec((1, s, hps * HEAD_DIM), lambda bi, h: (bi, 0, col // hps + h))
    return pl.pallas_call(
        _mixer_c_kernel,
        grid=(b, C_HEADS // hps),
        in_specs=[head(_COL_C_Q), head(_COL_C_K), head(_COL_C_V),
                  pl.BlockSpec((1, nk, _FORGET_ROWS, _C_TK), lambda bi, h: (bi, 0, 0, 0))],
        out_specs=head(0),
        out_shape=jax.ShapeDtypeStruct((b, s, C_HEADS * HEAD_DIM), MXU_DTYPE),
        scratch_shapes=[pltpu.VMEM((_C_TQ, s), F32), pltpu.VMEM((_C_TQ, s), F32),
                        pltpu.VMEM((_C_TQ, s), MXU_DTYPE), pltpu.VMEM((_C_TQ, s), MXU_DTYPE)],
        compiler_params=_params("arbitrary", "arbitrary"),
        name="mixer_c",
    )(u, u, u, cdec)


def _gelu(x):
    return 0.5 * x * (1.0 + lax.erf(x * np.float32(math.sqrt(0.5))))


def _nsa_compress_kernel(k_ref, v_ref, pe_ref, w1_ref, b1_ref, w2_ref, b2_ref, kc_ref, vct_ref):
    n_chunk = k_ref.shape[1] // CMP_STRIDE
    halves = CMP_LEN // CMP_STRIDE
    res = []
    for c, src in enumerate((k_ref, v_ref)):
        parts = []
        for half in range(halves):
            acc = jnp.zeros((n_chunk, HEAD_DIM), F32)
            for p in range(CMP_STRIDE):
                pos = half * CMP_STRIDE + p
                rows = src[0, pl.ds(p, n_chunk, stride=CMP_STRIDE), :] + pe_ref[c, pos:pos + 1, :]
                acc = acc + _dot(rows.astype(MXU_DTYPE), w1_ref[c, pos])
            parts.append(acc)
        hid = parts[0]
        for half in range(1, halves):
            hid = hid + pltpu.roll(parts[half], n_chunk - half, 0)
        hid = _gelu(hid + b1_ref[c:c + 1, :])
        res.append(_dot(hid.astype(MXU_DTYPE), w2_ref[c]) + b2_ref[c:c + 1, :])
    kc_ref[0] = res[0].astype(kc_ref.dtype)
    vct_ref[0] = res[1].T.astype(vct_ref.dtype)


def _nsa_compress(u_small, cmp_pe, cmp_w1, cmp_b1, cmp_w2, cmp_b2, layer):
    b, s, _ = u_small.shape
    n_chunk = s // CMP_STRIDE
    of_layer = lambda a: pl.BlockSpec((None,) + a.shape[1:],
                                      lambda bi: (layer,) + (0,) * (a.ndim - 1))
    return pl.pallas_call(
        _nsa_compress_kernel,
        grid=(b,),
        in_specs=[pl.BlockSpec((1, s, LANES), lambda bi: (bi, 0, _SMALL_KC)),
                  pl.BlockSpec((1, s, LANES), lambda bi: (bi, 0, _SMALL_VC)),
                  of_layer(cmp_pe), of_layer(cmp_w1), of_layer(cmp_b1),
                  of_layer(cmp_w2), of_layer(cmp_b2)],
        out_specs=[pl.BlockSpec((1, n_chunk, HEAD_DIM), lambda bi: (bi, 0, 0)),
                   pl.BlockSpec((1, HEAD_DIM, n_chunk), lambda bi: (bi, 0, 0))],
        out_shape=[jax.ShapeDtypeStruct((b, n_chunk, HEAD_DIM), MXU_DTYPE),
                   jax.ShapeDtypeStruct((b, HEAD_DIM, n_chunk), MXU_DTYPE)],
        compiler_params=_params("arbitrary"),
        name="nsa_compress",
    )(u_small, u_small, cmp_pe, cmp_w1, cmp_b1, cmp_w2, cmp_b2)


def _mixer_b_kernel(q_ref, ks_ref, vs_ref, kw_ref, vw_ref, kc_ref, vct_ref, gl_ref, bg_ref,
                    tslc_ref, twin_ref, tcmp_ref, o_ref, vst_ref, vwt_ref,
                    sc_ref, pc_ref, ss_ref, ps_ref, sw_ref, pw_ref, add_ref, oc_ref):
    step = pl.program_id(1)
    nblk = vwt_ref.shape[0]
    nh = B_HEADS
    n_slc = ks_ref.shape[1] // SLC_LEN
    n_win = twin_ref.shape[0] // nh - 1
    per_blk = BLK // SLC_LEN

    @pl.when(step == 0)
    def _():
        for kb in range(nblk):
            rows = slice(kb * BLK, (kb + 1) * BLK)
            vst_ref[:, rows] = vs_ref[0, rows, :].astype(F32).T.astype(MXU_DTYPE)
            vwt_ref[kb] = vw_ref[0, rows, :].astype(F32).T.astype(MXU_DTYPE)

    def bias_rows(t_ref, dl, n_real):
        idx = jnp.where(dl >= 0, dl, n_real)
        return _lane_cat([t_ref[idx * nh + h] for h in range(nh)])

    def body(n_keys_blk):
        n_keys = n_keys_blk * BLK
        denom = {}

        def block_index(sub):
            i = step * _Q_PER_STEP + sub
            kb0 = jnp.maximum(i - (n_win - 1), 0)
            return i, kb0

        def score_stage(sub):
            _, kb0 = block_index(sub)
            q4 = jnp.concatenate([q_ref[0, sub * BLK:(sub + 1) * BLK, h * HEAD_DIM:(h + 1) * HEAD_DIM]
                                  for h in range(nh)], axis=0)
            sc_ref[sub] = _dot_nt(kc_ref[0], q4)
            ss_ref[sub, :n_keys, :] = _dot_nt(ks_ref[0, :n_keys, :], q4)
            start = pl.multiple_of(kb0 * BLK, BLK)
            sw_ref[sub] = _dot_nt(kw_ref[0, pl.ds(start, n_win * BLK), :], q4)

        def cmp_win_softmax_stage(sub):
            i, kb0 = block_index(sub)
            st = sc_ref[sub] + _lane_cat([tcmp_ref[sub * nh + h] for h in range(nh)])
            m = jnp.max(st, axis=0, keepdims=True)
            e = jnp.exp2((st - m) * _EXP2_SCALE)
            l = jnp.sum(e, axis=0, keepdims=True)
            pc_ref[sub] = jnp.where(m > 0.5 * NEG_INF, e / l, 0.0).astype(MXU_DTYPE)
            bias = jnp.concatenate([bias_rows(twin_ref, i - kb0 - j, n_win) for j in range(n_win)],
                                   axis=0)
            z = sw_ref[sub] + bias
            m = jnp.max(z, axis=0, keepdims=True)
            p = jnp.exp2((z - m) * _EXP2_SCALE)
            denom["win", sub] = jnp.sum(p, axis=0, keepdims=True)
            pw_ref[sub] = p.astype(MXU_DTYPE)

        def select_stage(sub):
            i, _ = block_index(sub)
            pc = pc_ref[sub]
            oc_ref[sub] = _dot(vct_ref[0], pc)
            n_cmp_pad = kc_ref.shape[1]
            jj = lax.broadcasted_iota(jnp.int32, (n_slc, n_cmp_pad), 0)
            nn = lax.broadcasted_iota(jnp.int32, (n_slc, n_cmp_pad), 1)
            overlap = ((nn * CMP_STRIDE <= jj * SLC_LEN + SLC_LEN - 1)
                       & (nn * CMP_STRIDE + CMP_LEN - 1 >= jj * SLC_LEN)).astype(MXU_DTYPE)
            imp4 = _dot(overlap, pc)
            imp = functools.reduce(lambda a, b: a + b,
                                   [imp4[:, h * BLK:(h + 1) * BLK] for h in range(nh)])
            t = i * BLK + lax.broadcasted_iota(jnp.int32, (n_slc, BLK), 1)
            blk = lax.broadcasted_iota(jnp.int32, (n_slc, BLK), 0)
            cur = t // SLC_LEN
            forced = (blk == 0) | (blk == cur) | (blk == cur - 1)
            causal = blk * SLC_LEN <= t
            score = jnp.where(forced, FORCE_SCORE, jnp.where(causal, imp, NEG_INF))
            rank = jnp.zeros((n_slc, BLK), jnp.int32)
            for r in range(n_slc):
                row = score[r:r + 1, :]
                ahead = (row > score) | ((row == score) & (r < blk))
                rank = rank + ahead.astype(jnp.int32)
            add = jnp.where(rank < min(N_SELECT, n_slc), 0.0, NEG_INF).astype(F32)
            add_ref[sub] = _lane_cat([add] * nh)

        def selected_softmax_stage(sub):
            i, _ = block_index(sub)
            mask_rows = []
            for kb in range(n_keys_blk):
                sel = jnp.concatenate(
                    [jnp.broadcast_to(add_ref[sub, kb * per_blk + j:kb * per_blk + j + 1, :],
                                      (SLC_LEN, nh * BLK)) for j in range(per_blk)], axis=0)
                mask_rows.append(bias_rows(tslc_ref, i - kb, nblk) + sel)
            z = ss_ref[sub, :n_keys, :] + jnp.concatenate(mask_rows, axis=0)
            m = jnp.max(z, axis=0, keepdims=True)
            p = jnp.exp2((z - m) * _EXP2_SCALE)
            denom["slc", sub] = jnp.sum(p, axis=0, keepdims=True)
            ps_ref[sub, :n_keys, :] = p.astype(MXU_DTYPE)

        def output_stage(sub):
            _, kb0 = block_index(sub)
            rows = slice(sub * BLK, (sub + 1) * BLK)
            o_slc = _dot(vst_ref[:, :n_keys], ps_ref[sub, :n_keys, :]) / denom["slc", sub]
            o_win = (_dot(_lane_cat([vwt_ref[kb0 + j] for j in range(n_win)]), pw_ref[sub])
                     / denom["win", sub])
            gt = jax.nn.sigmoid(gl_ref[0, rows, :] + bg_ref[...]).T

            def gate(br):
                r0 = _GATE_LANE0 + br * nh
                return _lane_cat([gt[r0 + h:r0 + h + 1, :] for h in range(nh)])

            o_t = gate(0) * oc_ref[sub] + gate(1) * o_slc + gate(2) * o_win
            for h in range(nh):
                o_ref[0, rows, h * HEAD_DIM:(h + 1) * HEAD_DIM] = (
                    o_t[:, h * BLK:(h + 1) * BLK].T.astype(o_ref.dtype))

        stages = (score_stage, cmp_win_softmax_stage, select_stage, selected_softmax_stage,
                  output_stage)
        for t in range(len(stages) + _Q_PER_STEP - 1):
            for sub in range(_Q_PER_STEP):
                if 0 <= t - sub < len(stages):
                    stages[t - sub](sub)

    if nblk % _CAUSAL_STEP_B == 0 and _CAUSAL_STEP_B % _Q_PER_STEP == 0:
        for v in range(nblk // _CAUSAL_STEP_B):
            pl.when(step * _Q_PER_STEP // _CAUSAL_STEP_B == v)(
                functools.partial(body, (v + 1) * _CAUSAL_STEP_B))
    else:
        body(nblk)


def _mixer_b(u, u_small, kc, vct, b_nsa_gate, tiles, nblk):
    b, s, _ = u.shape
    nh = B_HEADS
    bg_pad = jnp.zeros((1, LANES), F32).at[0, _GATE_LANE0:_GATE_LANE0 + 3 * nh].set(b_nsa_gate)
    whole = lambda a: pl.BlockSpec(a.shape, lambda bi, i: (0, 0, 0))
    kv = lambda col: pl.BlockSpec((1, s, HEAD_DIM), lambda bi, i: (bi, 0, col))
    n_chunk = kc.shape[1]
    q, tq, wq = _Q_PER_STEP, _Q_PER_STEP * BLK, nh * BLK
    n_win = WIN // BLK + 1
    return pl.pallas_call(
        _mixer_b_kernel,
        grid=(b, nblk // _Q_PER_STEP),
        in_specs=[pl.BlockSpec((1, tq, nh * HEAD_DIM), lambda bi, i: (bi, i, _COL_B_Q // nh)),
                  kv(_COL_B_KS), kv(_COL_B_VS), kv(_COL_B_KW), kv(_COL_B_VW),
                  pl.BlockSpec((1, n_chunk, HEAD_DIM), lambda bi, i: (bi, 0, 0)),
                  pl.BlockSpec((1, HEAD_DIM, n_chunk), lambda bi, i: (bi, 0, 0)),
                  pl.BlockSpec((1, tq, LANES), lambda bi, i: (bi, i, _SMALL_GATE)),
                  pl.BlockSpec((1, LANES), lambda bi, i: (0, 0)),
                  whole(tiles["slc"]), whole(tiles["win"]),
                  pl.BlockSpec((_Q_PER_STEP * nh, BLK, BLK), lambda bi, i: (i, 0, 0))],
        out_specs=pl.BlockSpec((1, tq, nh * HEAD_DIM), lambda bi, i: (bi, i, 0)),
        out_shape=jax.ShapeDtypeStruct((b, s, nh * HEAD_DIM), MXU_DTYPE),
        scratch_shapes=[pltpu.VMEM((HEAD_DIM, s), MXU_DTYPE),
                        pltpu.VMEM((nblk, HEAD_DIM, BLK), MXU_DTYPE),
                        pltpu.VMEM((q, n_chunk, wq), F32), pltpu.VMEM((q, n_chunk, wq), MXU_DTYPE),
                        pltpu.VMEM((q, s, wq), F32), pltpu.VMEM((q, s, wq), MXU_DTYPE),
                        pltpu.VMEM((q, n_win * BLK, wq), F32),
                        pltpu.VMEM((q, n_win * BLK, wq), MXU_DTYPE),
                        pltpu.VMEM((q, s // SLC_LEN, wq), F32),
                        pltpu.VMEM((q, HEAD_DIM, wq), F32)],
        compiler_params=_params("arbitrary", "arbitrary"),
        name="mixer_b",
    )(u, u, u, u, u, kc, vct, u_small, bg_pad, tiles["slc"], tiles["win"], tiles["cmp"])


def _layer_norm(y, g, b):
    mu = jnp.mean(y, axis=-1, keepdims=True)
    yc = y - mu
    var = jnp.mean(yc * yc, axis=-1, keepdims=True)
    return yc * lax.rsqrt(var + LN_EPS) * g + b


def _merge_kernel(x_ref, oa_ref, ob_ref, oc_ref, wga_ref, wgb_ref, wgc_ref, bga_ref, bgb_ref,
                  bgc_ref, wpa_ref, wpb_ref, wpc_ref, wo_ref, g_ref, b_ref, y_ref, xb_ref, *,
                  alpha):
    j = pl.program_id(1)

    @pl.when(j == 0)
    def _():
        xb_ref[...] = x_ref[...].astype(MXU_DTYPE)
        y_ref[...] = jnp.zeros_like(y_ref)

    xb = xb_ref[...]
    mixed = None
    for o_ref, wg_ref, bg_ref, wp_ref in ((oa_ref, wga_ref, bga_ref, wpa_ref),
                                          (ob_ref, wgb_ref, bgb_ref, wpb_ref),
                                          (oc_ref, wgc_ref, bgc_ref, wpc_ref)):
        gate = jax.nn.sigmoid(_dot(xb, wg_ref[...]) + bg_ref[...])
        term = gate * _dot(o_ref[...], wp_ref[...])
        mixed = term if mixed is None else mixed + term
    y_ref[...] += _dot(mixed.astype(MXU_DTYPE), wo_ref[...])

    @pl.when(j == pl.num_programs(1) - 1)
    def _():
        y_ref[...] = _layer_norm(alpha * x_ref[...] + y_ref[...], g_ref[...], b_ref[...])


def _merge_ln(x2, oa, ob, oc, w_gate, b_gate, w_pa, w_pb, w_pc, w_out, ln_g, ln_b, layer, alpha,
              tm, tn):
    t, d = x2.shape
    nj = d // tn
    row = lambda k: pl.BlockSpec((tm, k), lambda i, j: (i, 0))
    colw = lambda k, shift: pl.BlockSpec((None, k, tn), lambda i, j: (layer, 0, shift * nj + j))
    vec = lambda shift: pl.BlockSpec((None, 1, tn), lambda i, j: (layer, 0, shift * nj + j))
    fullvec = pl.BlockSpec((None, 1, d), lambda i, j: (layer, 0, 0))
    return pl.pallas_call(
        functools.partial(_merge_kernel, alpha=alpha),
        grid=(t // tm, nj),
        in_specs=[row(d), row(oa.shape[1]), row(ob.shape[1]), row(oc.shape[1]),
                  colw(d, 0), colw(d, 1), colw(d, 2), vec(0), vec(1), vec(2),
                  colw(w_pa.shape[1], 0), colw(w_pb.shape[1], 0), colw(w_pc.shape[1], 0),
                  pl.BlockSpec((None, tn, d), lambda i, j: (layer, j, 0)), fullvec, fullvec],
        out_specs=pl.BlockSpec((tm, d), lambda i, j: (i, 0)),
        out_shape=jax.ShapeDtypeStruct((t, d), F32),
        scratch_shapes=[pltpu.VMEM((tm, d), MXU_DTYPE)],
        compiler_params=_params("arbitrary", "arbitrary"),
        name="merge_ln",
    )(x2, oa, ob, oc, w_gate, w_gate, w_gate, b_gate, b_gate, b_gate, w_pa, w_pb, w_pc, w_out,
      ln_g, ln_b)


_TAIL = 8


def _ffn_kernel(x_ref, wa_ref, wb_ref, cwa_ref, cwb_ref, cba_ref, cbb_ref, wd_ref, g_ref, b_ref,
                y_ref, xb_ref, ha_ref, hb_ref, tail_ref, *, alpha, tiles_per_seq):
    i = pl.program_id(0)
    j = pl.program_id(1)
    tm = x_ref.shape[0]
    tf = wa_ref.shape[1]

    @pl.when(j == 0)
    def _():
        xb_ref[...] = x_ref[...].astype(MXU_DTYPE)
        y_ref[...] = jnp.zeros_like(y_ref)

    @pl.when(i % tiles_per_seq == 0)
    def _():
        tail_ref[j] = jnp.zeros(tail_ref.shape[1:], F32)

    xb = xb_ref[...]
    for h_ref, w_ref, c0 in ((ha_ref, wa_ref, 0), (hb_ref, wb_ref, tf)):
        h_ref[0:_TAIL, :] = tail_ref[j, :, c0:c0 + tf]
        h_ref[_TAIL:, :] = _dot(xb, w_ref[...])
        tail_ref[j, :, c0:c0 + tf] = h_ref[tm:tm + _TAIL, :]

    def conv(h_ref, cw_ref, cb_ref):
        out = cb_ref[...]
        for tap in range(CONV_W):
            shift = CONV_W - 1 - tap
            out = out + cw_ref[tap:tap + 1, :] * h_ref[pl.ds(_TAIL - shift, tm), :]
        return out

    a = conv(ha_ref, cwa_ref, cba_ref)
    bb = conv(hb_ref, cwb_ref, cbb_ref)
    y_ref[...] += _dot((_gelu(a) * bb).astype(MXU_DTYPE), wd_ref[...])

    @pl.when(j == pl.num_programs(1) - 1)
    def _():
        y_ref[...] = _layer_norm(alpha * x_ref[...] + y_ref[...], g_ref[...], b_ref[...])


def _ffn_ln(x2, w_up, conv_w, conv_b, w_down, ln_g, ln_b, layer, alpha, seq, tm, tf):
    t, d = x2.shape
    f = w_down.shape[1]
    nj = f // tf
    assert seq % tm == 0 and CONV_W - 1 <= _TAIL
    half = lambda k, shift: pl.BlockSpec((None, k, tf), lambda i, j: (layer, 0, shift * nj + j))
    fullvec = pl.BlockSpec((None, 1, d), lambda i, j: (layer, 0, 0))
    row_tile = pl.BlockSpec((tm, d), lambda i, j: (i, 0), pipeline_mode=pl.Buffered(1))
    return pl.pallas_call(
        functools.partial(_ffn_kernel, alpha=alpha, tiles_per_seq=seq // tm),
        grid=(t // tm, nj),
        in_specs=[row_tile,
                  half(d, 0), half(d, 1), half(CONV_W, 0), half(CONV_W, 1), half(1, 0), half(1, 1),
                  pl.BlockSpec((None, tf, d), lambda i, j: (layer, j, 0)), fullvec, fullvec],
        out_specs=row_tile,
        out_shape=jax.ShapeDtypeStruct((t, d), F32),
        scratch_shapes=[pltpu.VMEM((tm, d), MXU_DTYPE),
                        pltpu.VMEM((tm + _TAIL, tf), F32),
                        pltpu.VMEM((tm + _TAIL, tf), F32),
                        pltpu.VMEM((nj, _TAIL, 2 * tf), F32)],
        compiler_params=_params("arbitrary", "arbitrary"),
        name="ffn_ln",
    )(x2, w_up, w_up, conv_w, conv_w, conv_b, conv_b, w_down, ln_g, ln_b)


def _pack_moves():
    main_dst = {'a_q': _COL_A_Q, 'a_k': _COL_A_K, 'a_v': _COL_A_V,
                'c_q': _COL_C_Q, 'c_k': _COL_C_K, 'c_v': _COL_C_V, 'b_q': _COL_B_Q,
                'b_k_slc': _COL_B_KS, 'b_v_slc': _COL_B_VS, 'b_k_win': _COL_B_KW,
                'b_v_win': _COL_B_VW}
    moves = [(blk * HEAD_DIM, _SPLIT[n][0], _SPLIT[n][1] - _SPLIT[n][0])
             for n, blk in main_dst.items()]
    moves += [(_N_MAIN + _SMALL_KC * LANES, _SPLIT['b_k_cmp'][0], LANES),
              (_N_MAIN + _SMALL_VC * LANES, _SPLIT['b_v_cmp'][0], LANES),
              (_N_MAIN + _SMALL_GATE * LANES, _GATE_COL0, LANES),
              (_N_MAIN + _SMALL_FORGET * LANES, _FORGET_COL0, N_IN - _FORGET_COL0)]
    return moves


def _pack_kernel(w_ref, o_ref):
    tail0 = _N_MAIN + _SMALL_FORGET * LANES
    o_ref[0, :, tail0:] = jnp.zeros((o_ref.shape[1], o_ref.shape[2] - tail0), o_ref.dtype)
    for d0, s0, width in _pack_moves():
        o_ref[0, :, d0:d0 + width] = w_ref[0, :, s0:s0 + width].astype(o_ref.dtype)


def _pack_w_in(w_in, tk):
    depth, d, n_in = w_in.shape
    assert n_in == N_IN
    n_out = _N_MAIN + _N_SMALL
    return pl.pallas_call(
        _pack_kernel,
        grid=(depth, d // tk),
        in_specs=[pl.BlockSpec((1, tk, n_in), lambda l, i: (l, i, 0))],
        out_specs=pl.BlockSpec((1, tk, n_out), lambda l, i: (l, i, 0)),
        out_shape=jax.ShapeDtypeStruct((depth, d, n_out), MXU_DTYPE),
        compiler_params=_params("arbitrary", "arbitrary"),
        name="pack_w_in",
    )(w_in)


_TM_PROJ, _TN_PROJ = 1024, 2048
_TM_MERGE, _TN_MERGE = 512, 512
_TM_FFN, _TF_FFN = 1024, 512


def kernel(x, rel_bias, w_in, b_f, b_nsa_gate, cmp_pe, cmp_w1, cmp_b1, cmp_w2, cmp_b2, w_gate, b_gate, w_pa, w_pb, w_pc, w_out, ln1_g, ln1_b, w_up, conv_w, conv_b, w_down, ln2_g, ln2_b):
    bsz, seq, d = x.shape
    depth = w_in.shape[0]
    nblk = seq // BLK
    alpha = (2 * depth) ** 0.25
    t = bsz * seq
    bf = lambda a: a.astype(MXU_DTYPE)
    row = lambda a: a[:, None, :]

    tiles = _bias_tiles(rel_bias, nblk)
    w_in_p = _pack_w_in(w_in, 256)
    cmp_w1_b = bf(cmp_w1).reshape(depth, 2, CMP_LEN, HEAD_DIM, HEAD_DIM)
    cmp_w2_b = bf(cmp_w2)
    w_gate_b, w_pa_b, w_pb_b, w_pc_b, w_out_b = (bf(w_gate), bf(w_pa), bf(w_pb), bf(w_pc),
                                                 bf(w_out))
    w_up_b, w_down_b = bf(w_up), bf(w_down)
    x2 = x.reshape(t, d)
    for l in range(depth):
        u, u_small = _proj(x2, w_in_p, l, _TM_PROJ, _TN_PROJ)
        u = u.reshape(bsz, seq, -1)
        u_small = u_small.reshape(bsz, seq, -1)

        o_a = _mixer_a(u, tiles["a"], nblk)
        kc, vct = _nsa_compress(u_small, cmp_pe, cmp_w1_b, cmp_b1, cmp_w2_b, cmp_b2, l)
        o_b = _mixer_b(u, u_small, kc, vct, b_nsa_gate[l], tiles, nblk)
        cdec = _fox_decay(u_small, b_f[l])
        o_c = _mixer_c(u, cdec)

        x1 = _merge_ln(x2, o_a.reshape(t, -1), o_b.reshape(t, -1), o_c.reshape(t, -1),
                       w_gate_b, row(b_gate), w_pa_b, w_pb_b, w_pc_b, w_out_b,
                       row(ln1_g), row(ln1_b), l, alpha, _TM_MERGE, _TN_MERGE)
        x2 = _ffn_ln(x1, w_up_b, conv_w, row(conv_b), w_down_b, row(ln2_g), row(ln2_b),
                     l, alpha, seq, _TM_FFN, _TF_FFN)
    return x2.reshape(bsz, seq, d)
```

```python
import functools
import math

import numpy as np
import jax
import jax.numpy as jnp
from jax import lax
from jax.experimental import pallas as pl
from jax.experimental.pallas import tpu as pltpu

HEAD_DIM = 128
A_GROUPS = ((128, 1), (512, 4), (2048, 16))
A_HEADS_PER_GROUP = 2
A_HEADS = A_HEADS_PER_GROUP * len(A_GROUPS)
B_HEADS = 4
CMP_LEN = 32
CMP_STRIDE = 16
SLC_LEN = 64
N_SELECT = 16
WIN = 512
C_HEADS = 6
N_BUCKETS = 32
MAX_DISTANCE = 2048
CONV_W = 3
LN_EPS = 1e-5
NEG_INF = -1e30
FORCE_SCORE = 1e9
ATTN_SCALE = HEAD_DIM ** -0.5
_INV_SCALE = np.float32(1.0 / ATTN_SCALE)
_EXP2_SCALE = np.float32(ATTN_SCALE * math.log2(math.e))

LANES = 128
VMEM_LIMIT_BYTES = 56 * 2 ** 20

MXU_DTYPE = jnp.bfloat16
F32 = jnp.float32
BLK = LANES

_COL_A_Q, _COL_A_K, _COL_A_V = 0, 6, 12
_COL_C_Q, _COL_C_K, _COL_C_V = 18, 24, 30
_COL_B_Q = 36
_COL_B_KS, _COL_B_VS, _COL_B_KW, _COL_B_VW = 40, 41, 42, 43
_N_COL_BLOCKS = 44


def _in_split_offsets():
    widths = (('a_q', A_HEADS * HEAD_DIM), ('a_k', A_HEADS * HEAD_DIM), ('a_v', A_HEADS * HEAD_DIM),
              ('b_q', B_HEADS * HEAD_DIM), ('b_k_cmp', HEAD_DIM), ('b_v_cmp', HEAD_DIM),
              ('b_k_slc', HEAD_DIM), ('b_v_slc', HEAD_DIM), ('b_k_win', HEAD_DIM),
              ('b_v_win', HEAD_DIM), ('b_gate', 3 * B_HEADS),
              ('c_q', C_HEADS * HEAD_DIM), ('c_k', C_HEADS * HEAD_DIM), ('c_v', C_HEADS * HEAD_DIM),
              ('c_f', C_HEADS))
    out, o = {}, 0
    for name, w in widths:
        out[name] = (o, o + w)
        o += w
    return out, o


_SPLIT, N_IN = _in_split_offsets()
_SMALL_KC, _SMALL_VC, _SMALL_GATE, _SMALL_FORGET = 0, 1, 2, 3
_N_MAIN = _N_COL_BLOCKS * HEAD_DIM
_N_SMALL = 4 * LANES
_GATE_COL0 = _SPLIT['b_gate'][0] // LANES * LANES
_GATE_LANE0 = _SPLIT['b_gate'][0] - _GATE_COL0
_FORGET_COL0 = _SPLIT['c_f'][0] // LANES * LANES
_FORGET_LANE0 = _SPLIT['c_f'][0] - _FORGET_COL0
assert _SPLIT['b_gate'][1] - _GATE_COL0 <= LANES and N_IN - _FORGET_COL0 <= LANES
_FORGET_ROW0 = _FORGET_LANE0 // 8 * 8
_FORGET_ROWS = -(-(_FORGET_LANE0 + C_HEADS - _FORGET_ROW0) // 8) * 8


def _params(*sem):
    return pltpu.CompilerParams(dimension_semantics=sem, vmem_limit_bytes=VMEM_LIMIT_BYTES)


def _dot(a, b):
    return jnp.dot(a, b, preferred_element_type=F32)


def _dot_nt(a, b):
    return lax.dot_general(a, b, (((1,), (1,)), ((), ())), preferred_element_type=F32)


def _bias_tile_kernel(base_ref, rmul_ref, cmul_ref, dmask_ref, dmax_ref, rmax_ref, head_ref,
                      rel_ref, out_ref):
    row = lax.broadcasted_iota(jnp.int32, (BLK, BLK), 0)
    col = lax.broadcasted_iota(jnp.int32, (BLK, BLK), 1)
    max_exact = N_BUCKETS // 2
    per_step = out_ref.shape[0]
    for k in range(per_step):
        t = pl.program_id(0) * per_step + k
        dist = base_ref[t] + rmul_ref[t] * row + cmul_ref[t] * col
        valid = ((dist >= 0) & (dist <= dmax_ref[t]) & ((dist & dmask_ref[t]) == 0)
                 & (row < rmax_ref[t]))
        d = jnp.maximum(dist, 0)
        log_ratio = (jnp.log(jnp.maximum(d, 1).astype(F32) / max_exact)
                     / math.log(MAX_DISTANCE / max_exact))
        large = jnp.minimum(max_exact + (log_ratio * (N_BUCKETS - max_exact)).astype(jnp.int32),
                            N_BUCKETS - 1)
        bucket = jnp.where(d < max_exact, d, large)
        h = head_ref[t]
        val = jnp.zeros((BLK, BLK), F32)
        for b in range(N_BUCKETS):
            val = jnp.where(bucket == b, rel_ref[b, h], val)
        out_ref[k] = jnp.where(valid, val * _INV_SCALE, NEG_INF)


def _a_tile_counts(nblk):
    return [min(window // BLK + 1, nblk) for window, _ in A_GROUPS]


def _tile_tables(nblk):
    big = 2 ** 30
    fam = {}
    fam["a"] = [(dl * BLK, 1, -1, dil - 1, window if dl < cnt else -1, BLK,
                 g * A_HEADS_PER_GROUP + hh)
                for g, ((window, dil), cnt) in enumerate(zip(A_GROUPS, _a_tile_counts(nblk)))
                for hh in range(A_HEADS_PER_GROUP)
                for dl in range(cnt + 1)]
    fam["slc"] = [(dl * BLK, -1, 1, 0, big if dl < nblk else -1, BLK, A_HEADS + h)
                  for dl in range(nblk + 1) for h in range(B_HEADS)]
    n_win = WIN // BLK + 1
    fam["win"] = [(dl * BLK, -1, 1, 0, WIN - 1 if dl < n_win else -1, BLK, A_HEADS + h)
                  for dl in range(n_win + 1) for h in range(B_HEADS)]
    n_cmp = (nblk * BLK - CMP_LEN) // CMP_STRIDE + 1
    fam["cmp"] = [(i * BLK - (CMP_LEN - 1), -CMP_STRIDE, 1, 0, big, n_cmp, A_HEADS + h)
                  for i in range(nblk) for h in range(B_HEADS)]
    return {k: [np.asarray(c, np.int32) for c in zip(*rows)] for k, rows in fam.items()}


def _a_tile_offsets(nblk):
    cnt = _a_tile_counts(nblk)
    return [A_HEADS_PER_GROUP * sum(c + 1 for c in cnt[:g]) for g in range(len(A_GROUPS))]


_MAX_TILES_PER_STEP = 17


def _bias_tiles(rel_bias, nblk):
    out = {}
    for name, cols in _tile_tables(nblk).items():
        n = cols[0].shape[0]
        per_step = max(k for k in range(1, _MAX_TILES_PER_STEP + 1) if n % k == 0)
        out[name] = pl.pallas_call(
            _bias_tile_kernel,
            grid_spec=pltpu.PrefetchScalarGridSpec(
                num_scalar_prefetch=len(cols),
                grid=(n // per_step,),
                in_specs=[pl.BlockSpec(memory_space=pltpu.SMEM)],
                out_specs=pl.BlockSpec((per_step, BLK, BLK), lambda t, *_: (t, 0, 0)),
            ),
            out_shape=jax.ShapeDtypeStruct((n, BLK, BLK), F32),
            compiler_params=_params("arbitrary"),
            name="bias_tiles_" + name,
        )(*[jnp.asarray(c) for c in cols], rel_bias)
    return out


def _proj_kernel(x_ref, w_ref, o_ref, os_ref, xb_ref):
    j = pl.program_id(1)
    tn = w_ref.shape[1]
    n_side = os_ref.shape[1]

    @pl.when(j == 0)
    def _():
        xb_ref[...] = x_ref[...].astype(MXU_DTYPE)

    acc = _dot(xb_ref[...], w_ref[...])
    o_ref[...] = acc.astype(o_ref.dtype)

    @pl.when(j == pl.num_programs(1) - 1)
    def _():
        os_ref[...] = acc[:, tn - n_side:]


def _proj(x2, w_packed, layer, tm, tn):
    t, d = x2.shape
    n_all = _N_MAIN + _N_SMALL
    assert n_all % tn == 0 and tn >= _N_SMALL
    nj = n_all // tn
    return pl.pallas_call(
        _proj_kernel,
        grid=(t // tm, nj),
        in_specs=[pl.BlockSpec((tm, d), lambda i, j: (i, 0)),
                  pl.BlockSpec((None, d, tn), lambda i, j: (layer, 0, j))],
        out_specs=[pl.BlockSpec((tm, tn), lambda i, j: (i, j)),
                   pl.BlockSpec((tm, _N_SMALL), lambda i, j: (i, 0))],
        out_shape=[jax.ShapeDtypeStruct((t, _N_MAIN), MXU_DTYPE),
                   jax.ShapeDtypeStruct((t, _N_SMALL), F32)],
        scratch_shapes=[pltpu.VMEM((tm, d), MXU_DTYPE)],
        compiler_params=_params("arbitrary", "arbitrary"),
        name="in_proj",
    )(x2, w_packed)


def _lane_cat(xs):
    return jnp.concatenate(xs, axis=1)


def _softmax_pv(z, v, axis):
    m = jnp.max(z, axis=axis, keepdims=True)
    p = jnp.exp2((z - m) * _EXP2_SCALE)
    l = jnp.sum(p, axis=axis, keepdims=True)
    if axis == 1:
        o = _dot(p.astype(MXU_DTYPE), v)
    else:
        o = _dot(v, p.astype(MXU_DTYPE))
    return o / l, m, l


_CAUSAL_STEP = 4
_QA_PER_STEP = 4
_Q_PER_STEP = 2


def _mixer_a_kernel(q_ref, k_ref, v_ref, t_ref, o_ref, s0_ref, s1_ref, p0_ref, p1_ref, *,
                    tile_off, tile_cnt):
    hpg = A_HEADS_PER_GROUP
    n_groups = len(A_GROUPS)
    nblk = k_ref.shape[1] // BLK
    step = pl.program_id(1)
    s_bufs, p_bufs = (s0_ref, s1_ref), (p0_ref, p1_ref)

    def body(n_full):
        chains = [(sub, g, hh) for sub in range(_QA_PER_STEP) for g in range(n_groups)
                  for hh in range(hpg)]

        def window(sub, g):
            i = step * _QA_PER_STEP + sub
            if tile_cnt[g] == nblk:
                return i, 0, 0, n_full
            kb0 = jnp.maximum(i - (tile_cnt[g] - 1), 0)
            return i, kb0, pl.multiple_of(kb0 * BLK, BLK), tile_cnt[g]

        stats = [None] * len(chains)
        results = {}
        for t in range(len(chains) + 2):
            if 0 <= t - 1 < len(chains):
                c = t - 1
                sub, g, hh = chains[c]
                i, kb0, _, nkb = window(sub, g)
                nd = tile_cnt[g]
                tbase = tile_off[g] + hh * (nd + 1)
                deltas = [i - kb0 - j for j in range(nkb)]
                bias = _lane_cat([t_ref[tbase + jnp.where(dl >= 0, dl, nd)] for dl in deltas])
                z = s_bufs[c % 2][:, :nkb * BLK] + bias
                m = jnp.max(z, axis=1, keepdims=True)
                p = jnp.exp2((z - m) * _EXP2_SCALE)
                stats[c] = (m, jnp.sum(p, axis=1, keepdims=True))
                p_bufs[c % 2][:, :nkb * BLK] = p.astype(MXU_DTYPE)
            if t < len(chains):
                sub, g, hh = chains[t]
                _, _, start, nkb = window(sub, g)
                c0 = (g * hpg + hh) * HEAD_DIM
                q = q_ref[0, sub * BLK:(sub + 1) * BLK, c0:c0 + HEAD_DIM]
                k = k_ref[0, pl.ds(start, nkb * BLK), c0:c0 + HEAD_DIM]
                s_bufs[t % 2][:, :nkb * BLK] = _dot_nt(q, k)
            if t - 2 >= 0:
                c = t - 2
                sub, g, hh = chains[c]
                _, _, start, nkb = window(sub, g)
                c0 = (g * hpg + hh) * HEAD_DIM
                v = v_ref[0, pl.ds(start, nkb * BLK), c0:c0 + HEAD_DIM]
                m, l = stats[c]
                results[chains[c]] = (_dot(p_bufs[c % 2][:, :nkb * BLK], v) / l,
                                      m * ATTN_SCALE + jnp.log(l))
                if (g, hh) == (n_groups - 1, hpg - 1):
                    for h2 in range(hpg):
                        outs = [results[(sub, g2, h2)][0] for g2 in range(n_groups)]
                        lses = [results[(sub, g2, h2)][1] for g2 in range(n_groups)]
                        mx = functools.reduce(jnp.maximum, lses)
                        es = [jnp.exp(x - mx) for x in lses]
                        tot = functools.reduce(lambda a, b: a + b, es)
                        o = functools.reduce(lambda a, b: a + b,
                                             [(e / tot) * og for e, og in zip(es, outs)])
                        o_ref[0, sub * BLK:(sub + 1) * BLK, h2 * HEAD_DIM:(h2 + 1) * HEAD_DIM] = (
                            o.astype(o_ref.dtype))

    if nblk % _CAUSAL_STEP == 0 and _CAUSAL_STEP % _QA_PER_STEP == 0:
        for v in range(nblk // _CAUSAL_STEP):
            pl.when(step * _QA_PER_STEP // _CAUSAL_STEP == v)(
                functools.partial(body, (v + 1) * _CAUSAL_STEP))
    else:
        body(nblk)


def _mixer_a(u, tiles_a, nblk):
    b, s, _ = u.shape
    w = A_HEADS * HEAD_DIM
    kern = functools.partial(_mixer_a_kernel, tile_off=tuple(_a_tile_offsets(nblk)),
                             tile_cnt=tuple(_a_tile_counts(nblk)))
    n_a = tiles_a.shape[0]
    tq = _QA_PER_STEP * BLK
    return pl.pallas_call(
        kern,
        grid=(b, nblk // _QA_PER_STEP),
        in_specs=[pl.BlockSpec((1, tq, w), lambda bi, i: (bi, i, _COL_A_Q // A_HEADS)),
                  pl.BlockSpec((1, s, w), lambda bi, i: (bi, 0, _COL_A_K // A_HEADS)),
                  pl.BlockSpec((1, s, w), lambda bi, i: (bi, 0, _COL_A_V // A_HEADS)),
                  pl.BlockSpec((n_a, BLK, BLK), lambda bi, i: (0, 0, 0))],
        out_specs=pl.BlockSpec((1, tq, A_HEADS_PER_GROUP * HEAD_DIM), lambda bi, i: (bi, i, 0)),
        out_shape=jax.ShapeDtypeStruct((b, s, A_HEADS_PER_GROUP * HEAD_DIM), MXU_DTYPE),
        scratch_shapes=[pltpu.VMEM((BLK, s), F32), pltpu.VMEM((BLK, s), F32),
                        pltpu.VMEM((BLK, s), MXU_DTYPE), pltpu.VMEM((BLK, s), MXU_DTYPE)],
        compiler_params=_params("arbitrary", "arbitrary"),
        name="mixer_a",
    )(u, u, u, tiles_a)


_C_TK = 256


def _split3(x):
    hi = x.astype(MXU_DTYPE)
    r1 = x - hi.astype(F32)
    mid = r1.astype(MXU_DTYPE)
    lo = (r1 - mid.astype(F32)).astype(MXU_DTYPE)
    return hi, mid, lo


def _fox_decay_kernel(f_ref, bf_ref, c_ref):
    x = f_ref[0] + bf_ref[...]
    logf = jnp.minimum(x, 0.0) - jnp.log1p(jnp.exp(-jnp.abs(x)))
    logf_t = logf.T[_FORGET_ROW0:_FORGET_ROW0 + _FORGET_ROWS]
    nk = c_ref.shape[1]
    r = lax.broadcasted_iota(jnp.int32, (_C_TK, _C_TK), 0)
    c = lax.broadcasted_iota(jnp.int32, (_C_TK, _C_TK), 1)
    upper = (r <= c).astype(MXU_DTYPE)
    carry = jnp.zeros((_FORGET_ROWS, 1), F32)
    for j in range(nk):
        hi, mid, lo = _split3(logf_t[:, j * _C_TK:(j + 1) * _C_TK])
        cs = (_dot(hi, upper) + _dot(mid, upper)) + _dot(lo, upper) + carry
        c_ref[0, j] = cs
        carry = cs[:, _C_TK - 1:_C_TK]


def _fox_decay(u_small, b_f):
    b, s, _ = u_small.shape
    nk = s // _C_TK
    bf_pad = jnp.zeros((1, LANES), F32).at[0, _FORGET_LANE0:_FORGET_LANE0 + C_HEADS].set(b_f)
    return pl.pallas_call(
        _fox_decay_kernel,
        grid=(b,),
        in_specs=[pl.BlockSpec((1, s, LANES), lambda bi: (bi, 0, _SMALL_FORGET)),
                  pl.BlockSpec((1, LANES), lambda bi: (0, 0))],
        out_specs=pl.BlockSpec((1, nk, _FORGET_ROWS, _C_TK), lambda bi: (bi, 0, 0, 0)),
        out_shape=jax.ShapeDtypeStruct((b, nk, _FORGET_ROWS, _C_TK), F32),
        compiler_params=_params("arbitrary"),
        name="fox_decay",
    )(u_small, bf_pad)


_C_TQ = 256
_C_HEADS_PER_STEP = 2


def _mixer_c_kernel(q_ref, k_ref, v_ref, c_ref, o_ref, s0_ref, s1_ref, p0_ref, p1_ref):
    row0 = _FORGET_LANE0 - _FORGET_ROW0 + pl.program_id(1) * _C_HEADS_PER_STEP
    tq = _C_TQ
    nq = q_ref.shape[1] // tq
    r = lax.broadcasted_iota(jnp.int32, (tq, tq), 0)
    c = lax.broadcasted_iota(jnp.int32, (tq, tq), 1)
    s_bufs, p_bufs = (s0_ref, s1_ref), (p0_ref, p1_ref)
    chains = [(hh, i) for hh in range(_C_HEADS_PER_STEP) for i in range(nq)]
    denom = [None] * len(chains)
    for t in range(len(chains) + 2):
        if 0 <= t - 1 < len(chains):
            n = t - 1
            hh, i = chains[n]
            nk = (i + 1) * tq
            decay = _lane_cat([c_ref[0, j, pl.ds(row0 + hh, 1), :] for j in range(nk // _C_TK)])
            z = s_bufs[n % 2][:, :nk] - decay * _INV_SCALE
            diag = jnp.where(c <= r, z[:, nk - tq:], NEG_INF)
            z = diag if i == 0 else _lane_cat([z[:, :nk - tq], diag])
            m = jnp.max(z, axis=1, keepdims=True)
            p = jnp.exp2((z - m) * _EXP2_SCALE)
            denom[n] = jnp.sum(p, axis=1, keepdims=True)
            p_bufs[n % 2][:, :nk] = p.astype(MXU_DTYPE)
        if t < len(chains):
            hh, i = chains[t]
            nk = (i + 1) * tq
            cols = slice(hh * HEAD_DIM, (hh + 1) * HEAD_DIM)
            s_bufs[t % 2][:, :nk] = _dot_nt(q_ref[0, i * tq:(i + 1) * tq, cols],
                                           k_ref[0, :nk, cols])
        if t - 2 >= 0:
            n = t - 2
            hh, i = chains[n]
            nk = (i + 1) * tq
            cols = slice(hh * HEAD_DIM, (hh + 1) * HEAD_DIM)
            o = _dot(p_bufs[n % 2][:, :nk], v_ref[0, :nk, cols]) / denom[n]
            o_ref[0, i * tq:(i + 1) * tq, cols] = o.astype(o_ref.dtype)


def _mixer_c(u, cdec):
    b, s, _ = u.shape
    nk = s // _C_TK
    hps = _C_HEADS_PER_STEP
    assert C_HEADS % hps == 0 and all(c % hps == 0 for c in (_COL_C_Q, _COL_C_K, _COL_C_V))
    head = lambda col: pl.BlockSpec((1, s, hps * HEAD_DIM), lambda bi, h: (bi, 0, col // hps + h))
    return pl.pallas_call(
        _mixer_c_kernel,
        grid=(b, C_HEADS // hps),
        in_specs=[head(_COL_C_Q), head(_COL_C_K), head(_COL_C_V),
                  pl.BlockSpec((1, nk, _FORGET_ROWS, _C_TK), lambda bi, h: (bi, 0, 0, 0))],
        out_specs=head(0),
        out_shape=jax.ShapeDtypeStruct((b, s, C_HEADS * HEAD_DIM), MXU_DTYPE),
        scratch_shapes=[pltpu.VMEM((_C_TQ, s), F32), pltpu.VMEM((_C_TQ, s), F32),
                        pltpu.VMEM((_C_TQ, s), MXU_DTYPE), pltpu.VMEM((_C_TQ, s), MXU_DTYPE)],
        compiler_params=_params("arbitrary", "arbitrary"),
        name="mixer_c",
    )(u, u, u, cdec)


def _gelu(x):
    return 0.5 * x * (1.0 + lax.erf(x * np.float32(math.sqrt(0.5))))


def _nsa_compress_kernel(k_ref, v_ref, pe_ref, w1_ref, b1_ref, w2_ref, b2_ref, kc_ref, vct_ref):
    n_chunk = k_ref.shape[1] // CMP_STRIDE
    halves = CMP_LEN // CMP_STRIDE
    res = []
    for c, src in enumerate((k_ref, v_ref)):
        parts = []
        for half in range(halves):
            acc = jnp.zeros((n_chunk, HEAD_DIM), F32)
            for p in range(CMP_STRIDE):
                pos = half * CMP_STRIDE + p
                rows = src[0, pl.ds(p, n_chunk, stride=CMP_STRIDE), :] + pe_ref[c, pos:pos + 1, :]
                acc = acc + _dot(rows.astype(MXU_DTYPE), w1_ref[c, pos])
            parts.append(acc)
        hid = parts[0]
        for half in range(1, halves):
            hid = hid + pltpu.roll(parts[half], n_chunk - half, 0)
        hid = _gelu(hid + b1_ref[c:c + 1, :])
        res.append(_dot(hid.astype(MXU_DTYPE), w2_ref[c]) + b2_ref[c:c + 1, :])
    kc_ref[0] = res[0].astype(kc_ref.dtype)
    vct_ref[0] = res[1].T.astype(vct_ref.dtype)


def _nsa_compress(u_small, cmp_pe, cmp_w1, cmp_b1, cmp_w2, cmp_b2, layer):
    b, s, _ = u_small.shape
    n_chunk = s // CMP_STRIDE
    of_layer = lambda a: pl.BlockSpec((None,) + a.shape[1:],
                                      lambda bi: (layer,) + (0,) * (a.ndim - 1))
    return pl.pallas_call(
        _nsa_compress_kernel,
        grid=(b,),
        in_specs=[pl.BlockSpec((1, s, LANES), lambda bi: (bi, 0, _SMALL_KC)),
                  pl.BlockSpec((1, s, LANES), lambda bi: (bi, 0, _SMALL_VC)),
                  of_layer(cmp_pe), of_layer(cmp_w1), of_layer(cmp_b1),
                  of_layer(cmp_w2), of_layer(cmp_b2)],
        out_specs=[pl.BlockSpec((1, n_chunk, HEAD_DIM), lambda bi: (bi, 0, 0)),
                   pl.BlockSpec((1, HEAD_DIM, n_chunk), lambda bi: (bi, 0, 0))],
        out_shape=[jax.ShapeDtypeStruct((b, n_chunk, HEAD_DIM), MXU_DTYPE),
                   jax.ShapeDtypeStruct((b, HEAD_DIM, n_chunk), MXU_DTYPE)],
        compiler_params=_params("arbitrary"),
        name="nsa_compress",
    )(u_small, u_small, cmp_pe, cmp_w1, cmp_b1, cmp_w2, cmp_b2)


def _mixer_b_kernel(q_ref, ks_ref, vs_ref, kw_ref, vw_ref, kc_ref, vct_ref, gl_ref, bg_ref,
                    tslc_ref, twin_ref, tcmp_ref, o_ref, vst_ref, vwt_ref,
                    sc_ref, pc_ref, ss_ref, ps_ref, sw_ref, pw_ref, add_ref, oc_ref):
    step = pl.program_id(1)
    nblk = vwt_ref.shape[0]
    nh = B_HEADS
    n_slc = ks_ref.shape[1] // SLC_LEN
    n_win = twin_ref.shape[0] // nh - 1
    per_blk = BLK // SLC_LEN

    @pl.when(step == 0)
    def _():
        for kb in range(nblk):
            rows = slice(kb * BLK, (kb + 1) * BLK)
            vst_ref[:, rows] = vs_ref[0, rows, :].astype(F32).T.astype(MXU_DTYPE)
            vwt_ref[kb] = vw_ref[0, rows, :].astype(F32).T.astype(MXU_DTYPE)

    def bias_rows(t_ref, dl, n_real):
        idx = jnp.where(dl >= 0, dl, n_real)
        return _lane_cat([t_ref[idx * nh + h] for h in range(nh)])

    def body(n_keys_blk):
        n_keys = n_keys_blk * BLK
        denom = {}

        def block_index(sub):
            i = step * _Q_PER_STEP + sub
            kb0 = jnp.maximum(i - (n_win - 1), 0)
            return i, kb0

        def score_stage(sub):
            _, kb0 = block_index(sub)
            q4 = jnp.concatenate([q_ref[0, sub * BLK:(sub + 1) * BLK, h * HEAD_DIM:(h + 1) * HEAD_DIM]
                                  for h in range(nh)], axis=0)
            sc_ref[sub] = _dot_nt(kc_ref[0], q4)
            ss_ref[sub, :n_keys, :] = _dot_nt(ks_ref[0, :n_keys, :], q4)
            start = pl.multiple_of(kb0 * BLK, BLK)
            sw_ref[sub] = _dot_nt(kw_ref[0, pl.ds(start, n_win * BLK), :], q4)

        def cmp_win_softmax_stage(sub):
            i, kb0 = block_index(sub)
            st = sc_ref[sub] + _lane_cat([tcmp_ref[sub * nh + h] for h in range(nh)])
            m = jnp.max(st, axis=0, keepdims=True)
            e = jnp.exp2((st - m) * _EXP2_SCALE)
            l = jnp.sum(e, axis=0, keepdims=True)
            pc_ref[sub] = jnp.where(m > 0.5 * NEG_INF, e / l, 0.0).astype(MXU_DTYPE)
            bias = jnp.concatenate([bias_rows(twin_ref, i - kb0 - j, n_win) for j in range(n_win)],
                                   axis=0)
            z = sw_ref[sub] + bias
            m = jnp.max(z, axis=0, keepdims=True)
            p = jnp.exp2((z - m) * _EXP2_SCALE)
            denom["win", sub] = jnp.sum(p, axis=0, keepdims=True)
            pw_ref[sub] = p.astype(MXU_DTYPE)

        def select_stage(sub):
            i, _ = block_index(sub)
            pc = pc_ref[sub]
            oc_ref[sub] = _dot(vct_ref[0], pc)
            n_cmp_pad = kc_ref.shape[1]
            jj = lax.broadcasted_iota(jnp.int32, (n_slc, n_cmp_pad), 0)
            nn = lax.broadcasted_iota(jnp.int32, (n_slc, n_cmp_pad), 1)
            overlap = ((nn * CMP_STRIDE <= jj * SLC_LEN + SLC_LEN - 1)
                       & (nn * CMP_STRIDE + CMP_LEN - 1 >= jj * SLC_LEN)).astype(MXU_DTYPE)
            imp4 = _dot(overlap, pc)
            imp = functools.reduce(lambda a, b: a + b,
                                   [imp4[:, h * BLK:(h + 1) * BLK] for h in range(nh)])
            t = i * BLK + lax.broadcasted_iota(jnp.int32, (n_slc, BLK), 1)
            blk = lax.broadcasted_iota(jnp.int32, (n_slc, BLK), 0)
            cur = t // SLC_LEN
            forced = (blk == 0) | (blk == cur) | (blk == cur - 1)
            causal = blk * SLC_LEN <= t
            score = jnp.where(forced, FORCE_SCORE, jnp.where(causal, imp, NEG_INF))
            rank = jnp.zeros((n_slc, BLK), jnp.int32)
            for r in range(n_slc):
                row = score[r:r + 1, :]
                ahead = (row > score) | ((row == score) & (r < blk))
                rank = rank + ahead.astype(jnp.int32)
            add = jnp.where(rank < min(N_SELECT, n_slc), 0.0, NEG_INF).astype(F32)
            add_ref[sub] = _lane_cat([add] * nh)

        def selected_softmax_stage(sub):
            i, _ = block_index(sub)
            mask_rows = []
            for kb in range(n_keys_blk):
                sel = jnp.concatenate(
                    [jnp.broadcast_to(add_ref[sub, kb * per_blk + j:kb * per_blk + j + 1, :],
                                      (SLC_LEN, nh * BLK)) for j in range(per_blk)], axis=0)
                mask_rows.append(bias_rows(tslc_ref, i - kb, nblk) + sel)
            z = ss_ref[sub, :n_keys, :] + jnp.concatenate(mask_rows, axis=0)
            m = jnp.max(z, axis=0, keepdims=True)
            p = jnp.exp2((z - m) * _EXP2_SCALE)
            denom["slc", sub] = jnp.sum(p, axis=0, keepdims=True)
            ps_ref[sub, :n_keys, :] = p.astype(MXU_DTYPE)

        def output_stage(sub):
            _, kb0 = block_index(sub)
            rows = slice(sub * BLK, (sub + 1) * BLK)
            o_slc = _dot(vst_ref[:, :n_keys], ps_ref[sub, :n_keys, :]) / denom["slc", sub]
            o_win = (_dot(_lane_cat([vwt_ref[kb0 + j] for j in range(n_win)]), pw_ref[sub])
                     / denom["win", sub])
            gt = jax.nn.sigmoid(gl_ref[0, rows, :] + bg_ref[...]).T

            def gate(br):
                r0 = _GATE_LANE0 + br * nh
                return _lane_cat([gt[r0 + h:r0 + h + 1, :] for h in range(nh)])

            o_t = gate(0) * oc_ref[sub] + gate(1) * o_slc + gate(2) * o_win
            for h in range(nh):
                o_ref[0, rows, h * HEAD_DIM:(h + 1) * HEAD_DIM] = (
                    o_t[:, h * BLK:(h + 1) * BLK].T.astype(o_ref.dtype))

        stages = (score_stage, cmp_win_softmax_stage, select_stage, selected_softmax_stage,
                  output_stage)
        for t in range(len(stages) + _Q_PER_STEP - 1):
            for sub in range(_Q_PER_STEP):
                if 0 <= t - sub < len(stages):
                    stages[t - sub](sub)

    if nblk % _CAUSAL_STEP == 0 and _CAUSAL_STEP % _Q_PER_STEP == 0:
        for v in range(nblk // _CAUSAL_STEP):
            pl.when(step * _Q_PER_STEP // _CAUSAL_STEP == v)(
                functools.partial(body, (v + 1) * _CAUSAL_STEP))
    else:
        body(nblk)


def _mixer_b(u, u_small, kc, vct, b_nsa_gate, tiles, nblk):
    b, s, _ = u.shape
    nh = B_HEADS
    bg_pad = jnp.zeros((1, LANES), F32).at[0, _GATE_LANE0:_GATE_LANE0 + 3 * nh].set(b_nsa_gate)
    whole = lambda a: pl.BlockSpec(a.shape, lambda bi, i: (0, 0, 0))
    kv = lambda col: pl.BlockSpec((1, s, HEAD_DIM), lambda bi, i: (bi, 0, col))
    n_chunk = kc.shape[1]
    q, tq, wq = _Q_PER_STEP, _Q_PER_STEP * BLK, nh * BLK
    n_win = WIN // BLK + 1
    return pl.pallas_call(
        _mixer_b_kernel,
        grid=(b, nblk // _Q_PER_STEP),
        in_specs=[pl.BlockSpec((1, tq, nh * HEAD_DIM), lambda bi, i: (bi, i, _COL_B_Q // nh)),
                  kv(_COL_B_KS), kv(_COL_B_VS), kv(_COL_B_KW), kv(_COL_B_VW),
                  pl.BlockSpec((1, n_chunk, HEAD_DIM), lambda bi, i: (bi, 0, 0)),
                  pl.BlockSpec((1, HEAD_DIM, n_chunk), lambda bi, i: (bi, 0, 0)),
                  pl.BlockSpec((1, tq, LANES), lambda bi, i: (bi, i, _SMALL_GATE)),
                  pl.BlockSpec((1, LANES), lambda bi, i: (0, 0)),
                  whole(tiles["slc"]), whole(tiles["win"]),
                  pl.BlockSpec((_Q_PER_STEP * nh, BLK, BLK), lambda bi, i: (i, 0, 0))],
        out_specs=pl.BlockSpec((1, tq, nh * HEAD_DIM), lambda bi, i: (bi, i, 0)),
        out_shape=jax.ShapeDtypeStruct((b, s, nh * HEAD_DIM), MXU_DTYPE),
        scratch_shapes=[pltpu.VMEM((HEAD_DIM, s), MXU_DTYPE),
                        pltpu.VMEM((nblk, HEAD_DIM, BLK), MXU_DTYPE),
                        pltpu.VMEM((q, n_chunk, wq), F32), pltpu.VMEM((q, n_chunk, wq), MXU_DTYPE),
                        pltpu.VMEM((q, s, wq), F32), pltpu.VMEM((q, s, wq), MXU_DTYPE),
                        pltpu.VMEM((q, n_win * BLK, wq), F32),
                        pltpu.VMEM((q, n_win * BLK, wq), MXU_DTYPE),
                        pltpu.VMEM((q, s // SLC_LEN, wq), F32),
                        pltpu.VMEM((q, HEAD_DIM, wq), F32)],
        compiler_params=_params("arbitrary", "arbitrary"),
        name="mixer_b",
    )(u, u, u, u, u, kc, vct, u_small, bg_pad, tiles["slc"], tiles["win"], tiles["cmp"])


def _layer_norm(y, g, b):
    mu = jnp.mean(y, axis=-1, keepdims=True)
    yc = y - mu
    var = jnp.mean(yc * yc, axis=-1, keepdims=True)
    return yc * lax.rsqrt(var + LN_EPS) * g + b


def _merge_kernel(x_ref, oa_ref, ob_ref, oc_ref, wga_ref, wgb_ref, wgc_ref, bga_ref, bgb_ref,
                  bgc_ref, wpa_ref, wpb_ref, wpc_ref, wo_ref, g_ref, b_ref, y_ref, xb_ref, *,
                  alpha):
    j = pl.program_id(1)

    @pl.when(j == 0)
    def _():
        xb_ref[...] = x_ref[...].astype(MXU_DTYPE)
        y_ref[...] = jnp.zeros_like(y_ref)

    xb = xb_ref[...]
    mixed = None
    for o_ref, wg_ref, bg_ref, wp_ref in ((oa_ref, wga_ref, bga_ref, wpa_ref),
                                          (ob_ref, wgb_ref, bgb_ref, wpb_ref),
                                          (oc_ref, wgc_ref, bgc_ref, wpc_ref)):
        gate = jax.nn.sigmoid(_dot(xb, wg_ref[...]) + bg_ref[...])
        term = gate * _dot(o_ref[...], wp_ref[...])
        mixed = term if mixed is None else mixed + term
    y_ref[...] += _dot(mixed.astype(MXU_DTYPE), wo_ref[...])

    @pl.when(j == pl.num_programs(1) - 1)
    def _():
        y_ref[...] = _layer_norm(alpha * x_ref[...] + y_ref[...], g_ref[...], b_ref[...])


def _merge_ln(x2, oa, ob, oc, w_gate, b_gate, w_pa, w_pb, w_pc, w_out, ln_g, ln_b, layer, alpha,
              tm, tn):
    t, d = x2.shape
    nj = d // tn
    row = lambda k: pl.BlockSpec((tm, k), lambda i, j: (i, 0))
    colw = lambda k, shift: pl.BlockSpec((None, k, tn), lambda i, j: (layer, 0, shift * nj + j))
    vec = lambda shift: pl.BlockSpec((None, 1, tn), lambda i, j: (layer, 0, shift * nj + j))
    fullvec = pl.BlockSpec((None, 1, d), lambda i, j: (layer, 0, 0))
    return pl.pallas_call(
        functools.partial(_merge_kernel, alpha=alpha),
        grid=(t // tm, nj),
        in_specs=[row(d), row(oa.shape[1]), row(ob.shape[1]), row(oc.shape[1]),
                  colw(d, 0), colw(d, 1), colw(d, 2), vec(0), vec(1), vec(2),
                  colw(w_pa.shape[1], 0), colw(w_pb.shape[1], 0), colw(w_pc.shape[1], 0),
                  pl.BlockSpec((None, tn, d), lambda i, j: (layer, j, 0)), fullvec, fullvec],
        out_specs=pl.BlockSpec((tm, d), lambda i, j: (i, 0)),
        out_shape=jax.ShapeDtypeStruct((t, d), F32),
        scratch_shapes=[pltpu.VMEM((tm, d), MXU_DTYPE)],
        compiler_params=_params("arbitrary", "arbitrary"),
        name="merge_ln",
    )(x2, oa, ob, oc, w_gate, w_gate, w_gate, b_gate, b_gate, b_gate, w_pa, w_pb, w_pc, w_out,
      ln_g, ln_b)


_TAIL = 8
_GATE_ROWS = 32


def _ffn_kernel(x_ref, wa_ref, wb_ref, cwa_ref, cwb_ref, cba_ref, cbb_ref, wd_ref, g_ref, b_ref,
                y_ref, xb_ref, acc_ref, h0a_ref, h0b_ref, h1a_ref, h1b_ref, g0_ref, g1_ref,
                tail_ref, *, alpha, tiles_per_seq, nj):
    s = pl.program_id(0)
    n = pl.num_programs(0) - 2
    su = jnp.minimum(s, n - 1)
    iu, ju = su // nj, su % nj
    sd = s - 2
    jd = jnp.maximum(sd, 0) % nj
    tm = x_ref.shape[0]
    tf = wa_ref.shape[1]

    @pl.when(s == 0)
    def _():
        for ref in (acc_ref, h1a_ref, h1b_ref, g0_ref, g1_ref):
            ref[...] = jnp.zeros_like(ref)

    @pl.when(ju == 0)
    def _():
        xb_ref[...] = x_ref[...].astype(MXU_DTYPE)

    @pl.when(iu % tiles_per_seq == 0)
    def _():
        tail_ref[ju] = jnp.zeros(tail_ref.shape[1:], F32)

    @pl.when((jd == 0) & (sd >= 0))
    def _():
        acc_ref[...] = alpha * x_ref[...]

    def conv(h_ref, cw_ref, cb_ref, r0):
        out = cb_ref[...]
        for tap in range(CONV_W):
            shift = CONV_W - 1 - tap
            out = out + cw_ref[tap:tap + 1, :] * h_ref[pl.ds(_TAIL - shift + r0, _GATE_ROWS), :]
        return out

    def stages(h_up, h_gate, g_gate, g_down):
        for r0 in range(0, tm, _GATE_ROWS):
            a = conv(h_gate[0], cwa_ref, cba_ref, r0)
            bb = conv(h_gate[1], cwb_ref, cbb_ref, r0)
            g_gate[r0:r0 + _GATE_ROWS, :] = (_gelu(a) * bb).astype(MXU_DTYPE)
        xb = xb_ref[...]
        for h_ref, w_ref, c0 in ((h_up[0], wa_ref, 0), (h_up[1], wb_ref, tf)):
            h_ref[0:_TAIL, :] = tail_ref[ju, :, c0:c0 + tf]
            h_ref[_TAIL:, :] = _dot(xb, w_ref[...])
            tail_ref[ju, :, c0:c0 + tf] = h_ref[tm:tm + _TAIL, :]
        acc_ref[...] += _dot(g_down[...], wd_ref[...])

    @pl.when(s % 2 == 0)
    def _():
        stages((h0a_ref, h0b_ref), (h1a_ref, h1b_ref), g1_ref, g0_ref)

    @pl.when(s % 2 == 1)
    def _():
        stages((h1a_ref, h1b_ref), (h0a_ref, h0b_ref), g0_ref, g1_ref)

    @pl.when((jd == nj - 1) & (sd >= 0))
    def _():
        y_ref[...] = _layer_norm(acc_ref[...], g_ref[...], b_ref[...])


def _ffn_ln(x2, w_up, conv_w, conv_b, w_down, ln_g, ln_b, layer, alpha, seq, tm, tf):
    t, d = x2.shape
    f = w_down.shape[1]
    nj = f // tf
    n = (t // tm) * nj
    assert seq % tm == 0 and CONV_W - 1 <= _TAIL and nj > 2
    up = lambda s: jnp.minimum(s, n - 1)
    gate = lambda s: jnp.clip(s - 1, 0, n - 1)
    down = lambda s: jnp.clip(s - 2, 0, n - 1)
    half = lambda k, shift, stage: pl.BlockSpec(
        (None, k, tf), lambda s: (layer, 0, shift * nj + stage(s) % nj))
    fullvec = pl.BlockSpec((None, 1, d), lambda s: (layer, 0, 0))
    h_scratch = pltpu.VMEM((tm + _TAIL, tf), F32)
    g_scratch = pltpu.VMEM((tm, tf), MXU_DTYPE)
    return pl.pallas_call(
        functools.partial(_ffn_kernel, alpha=alpha, tiles_per_seq=seq // tm, nj=nj),
        grid=(n + 2,),
        in_specs=[pl.BlockSpec((tm, d), lambda s: (up(s) // nj, 0)),
                  half(d, 0, up), half(d, 1, up),
                  half(CONV_W, 0, gate), half(CONV_W, 1, gate), half(1, 0, gate), half(1, 1, gate),
                  pl.BlockSpec((None, tf, d), lambda s: (layer, down(s) % nj, 0)),
                  fullvec, fullvec],
        out_specs=pl.BlockSpec((tm, d), lambda s: (down(s) // nj, 0)),
        out_shape=jax.ShapeDtypeStruct((t, d), F32),
        scratch_shapes=[pltpu.VMEM((tm, d), MXU_DTYPE), pltpu.VMEM((tm, d), F32),
                        h_scratch, h_scratch, h_scratch, h_scratch, g_scratch, g_scratch,
                        pltpu.VMEM((nj, _TAIL, 2 * tf), F32)],
        compiler_params=_params("arbitrary"),
        name="ffn_ln",
    )(x2, w_up, w_up, conv_w, conv_w, conv_b, conv_b, w_down, ln_g, ln_b)


def _pack_moves():
    main_dst = {'a_q': _COL_A_Q, 'a_k': _COL_A_K, 'a_v': _COL_A_V,
                'c_q': _COL_C_Q, 'c_k': _COL_C_K, 'c_v': _COL_C_V, 'b_q': _COL_B_Q,
                'b_k_slc': _COL_B_KS, 'b_v_slc': _COL_B_VS, 'b_k_win': _COL_B_KW,
                'b_v_win': _COL_B_VW}
    moves = [(blk * HEAD_DIM, _SPLIT[n][0], _SPLIT[n][1] - _SPLIT[n][0])
             for n, blk in main_dst.items()]
    moves += [(_N_MAIN + _SMALL_KC * LANES, _SPLIT['b_k_cmp'][0], LANES),
              (_N_MAIN + _SMALL_VC * LANES, _SPLIT['b_v_cmp'][0], LANES),
              (_N_MAIN + _SMALL_GATE * LANES, _GATE_COL0, LANES),
              (_N_MAIN + _SMALL_FORGET * LANES, _FORGET_COL0, N_IN - _FORGET_COL0)]
    return moves


def _pack_kernel(w_ref, o_ref):
    tail0 = _N_MAIN + _SMALL_FORGET * LANES
    o_ref[0, :, tail0:] = jnp.zeros((o_ref.shape[1], o_ref.shape[2] - tail0), o_ref.dtype)
    for d0, s0, width in _pack_moves():
        o_ref[0, :, d0:d0 + width] = w_ref[0, :, s0:s0 + width].astype(o_ref.dtype)


def _pack_w_in(w_in, tk):
    depth, d, n_in = w_in.shape
    assert n_in == N_IN
    n_out = _N_MAIN + _N_SMALL
    return pl.pallas_call(
        _pack_kernel,
        grid=(depth, d // tk),
        in_specs=[pl.BlockSpec((1, tk, n_in), lambda l, i: (l, i, 0))],
        out_specs=pl.BlockSpec((1, tk, n_out), lambda l, i: (l, i, 0)),
        out_shape=jax.ShapeDtypeStruct((depth, d, n_out), MXU_DTYPE),
        compiler_params=_params("arbitrary", "arbitrary"),
        name="pack_w_in",
    )(w_in)


_TM_PROJ, _TN_PROJ = 1024, 2048
_TM_MERGE, _TN_MERGE = 512, 512
_TM_FFN, _TF_FFN = 512, 512


def kernel(x, rel_bias, w_in, b_f, b_nsa_gate, cmp_pe, cmp_w1, cmp_b1, cmp_w2, cmp_b2, w_gate, b_gate, w_pa, w_pb, w_pc, w_out, ln1_g, ln1_b, w_up, conv_w, conv_b, w_down, ln2_g, ln2_b):
    bsz, seq, d = x.shape
    depth = w_in.shape[0]
    nblk = seq // BLK
    alpha = (2 * depth) ** 0.25
    t = bsz * seq
    bf = lambda a: a.astype(MXU_DTYPE)
    row = lambda a: a[:, None, :]

    tiles = _bias_tiles(rel_bias, nblk)
    w_in_p = _pack_w_in(w_in, 256)
    cmp_w1_b = bf(cmp_w1).reshape(depth, 2, CMP_LEN, HEAD_DIM, HEAD_DIM)
    cmp_w2_b = bf(cmp_w2)
    w_gate_b, w_pa_b, w_pb_b, w_pc_b, w_out_b = (bf(w_gate), bf(w_pa), bf(w_pb), bf(w_pc),
                                                 bf(w_out))
    w_up_b, w_down_b = bf(w_up), bf(w_down)
    x2 = x.reshape(t, d)
    for l in range(depth):
        u, u_small = _proj(x2, w_in_p, l, _TM_PROJ, _TN_PROJ)
        u = u.reshape(bsz, seq, -1)
        u_small = u_small.reshape(bsz, seq, -1)

        o_a = _mixer_a(u, tiles["a"], nblk)
        kc, vct = _nsa_compress(u_small, cmp_pe, cmp_w1_b, cmp_b1, cmp_w2_b, cmp_b2, l)
        o_b = _mixer_b(u, u_small, kc, vct, b_nsa_gate[l], tiles, nblk)
        cdec = _fox_decay(u_small, b_f[l])
        o_c = _mixer_c(u, cdec)

        x1 = _merge_ln(x2, o_a.reshape(t, -1), o_b.reshape(t, -1), o_c.reshape(t, -1),
                       w_gate_b, row(b_gate), w_pa_b, w_pb_b, w_pc_b, w_out_b,
                       row(ln1_g), row(ln1_b), l, alpha, _TM_MERGE, _TN_MERGE)
        x2 = _ffn_ln(x1, w_up_b, conv_w, row(conv_b), w_down_b, row(ln2_g), row(ln2_b),
                     l, alpha, seq, _TM_FFN, _TF_FFN)
    return x2.reshape(bsz, seq, d)
```

```python
import functools
import math

import numpy as np
import jax
import jax.numpy as jnp
from jax import lax
from jax.experimental import pallas as pl
from jax.experimental.pallas import tpu as pltpu

HEAD_DIM = 128
A_GROUPS = ((128, 1), (512, 4), (2048, 16))
A_HEADS_PER_GROUP = 2
A_HEADS = A_HEADS_PER_GROUP * len(A_GROUPS)
B_HEADS = 4
CMP_LEN = 32
CMP_STRIDE = 16
SLC_LEN = 64
N_SELECT = 16
WIN = 512
C_HEADS = 6
N_BUCKETS = 32
MAX_DISTANCE = 2048
CONV_W = 3
LN_EPS = 1e-5
NEG_INF = -1e30
FORCE_SCORE = 1e9
ATTN_SCALE = HEAD_DIM ** -0.5
_INV_SCALE = np.float32(1.0 / ATTN_SCALE)
_EXP2_SCALE = np.float32(ATTN_SCALE * math.log2(math.e))

LANES = 128
VMEM_LIMIT_BYTES = 56 * 2 ** 20

MXU_DTYPE = jnp.bfloat16
F32 = jnp.float32
BLK = LANES

_COL_A_Q, _COL_A_K, _COL_A_V = 0, 6, 12
_COL_C_Q, _COL_C_K, _COL_C_V = 18, 24, 30
_COL_B_Q = 36
_COL_B_KS, _COL_B_VS, _COL_B_KW, _COL_B_VW = 40, 41, 42, 43
_N_COL_BLOCKS = 44


def _in_split_offsets():
    widths = (('a_q', A_HEADS * HEAD_DIM), ('a_k', A_HEADS * HEAD_DIM), ('a_v', A_HEADS * HEAD_DIM),
              ('b_q', B_HEADS * HEAD_DIM), ('b_k_cmp', HEAD_DIM), ('b_v_cmp', HEAD_DIM),
              ('b_k_slc', HEAD_DIM), ('b_v_slc', HEAD_DIM), ('b_k_win', HEAD_DIM),
              ('b_v_win', HEAD_DIM), ('b_gate', 3 * B_HEADS),
              ('c_q', C_HEADS * HEAD_DIM), ('c_k', C_HEADS * HEAD_DIM), ('c_v', C_HEADS * HEAD_DIM),
              ('c_f', C_HEADS))
    out, o = {}, 0
    for name, w in widths:
        out[name] = (o, o + w)
        o += w
    return out, o


_SPLIT, N_IN = _in_split_offsets()
_SMALL_KC, _SMALL_VC, _SMALL_GATE, _SMALL_FORGET = 0, 1, 2, 3
_N_MAIN = _N_COL_BLOCKS * HEAD_DIM
_N_SMALL = 4 * LANES
_GATE_COL0 = _SPLIT['b_gate'][0] // LANES * LANES
_GATE_LANE0 = _SPLIT['b_gate'][0] - _GATE_COL0
_FORGET_COL0 = _SPLIT['c_f'][0] // LANES * LANES
_FORGET_LANE0 = _SPLIT['c_f'][0] - _FORGET_COL0
assert _SPLIT['b_gate'][1] - _GATE_COL0 <= LANES and N_IN - _FORGET_COL0 <= LANES
_FORGET_ROW0 = _FORGET_LANE0 // 8 * 8
_FORGET_ROWS = -(-(_FORGET_LANE0 + C_HEADS - _FORGET_ROW0) // 8) * 8


def _params(*sem):
    return pltpu.CompilerParams(dimension_semantics=sem, vmem_limit_bytes=VMEM_LIMIT_BYTES)


def _dot(a, b):
    return jnp.dot(a, b, preferred_element_type=F32)


def _dot_nt(a, b):
    return lax.dot_general(a, b, (((1,), (1,)), ((), ())), preferred_element_type=F32)


def _bias_tile_kernel(base_ref, rmul_ref, cmul_ref, dmask_ref, dmax_ref, rmax_ref, head_ref,
                      rel_ref, out_ref):
    row = lax.broadcasted_iota(jnp.int32, (BLK, BLK), 0)
    col = lax.broadcasted_iota(jnp.int32, (BLK, BLK), 1)
    max_exact = N_BUCKETS // 2
    per_step = out_ref.shape[0]
    for k in range(per_step):
        t = pl.program_id(0) * per_step + k
        dist = base_ref[t] + rmul_ref[t] * row + cmul_ref[t] * col
        valid = ((dist >= 0) & (dist <= dmax_ref[t]) & ((dist & dmask_ref[t]) == 0)
                 & (row < rmax_ref[t]))
        d = jnp.maximum(dist, 0)
        log_ratio = (jnp.log(jnp.maximum(d, 1).astype(F32) / max_exact)
                     / math.log(MAX_DISTANCE / max_exact))
        large = jnp.minimum(max_exact + (log_ratio * (N_BUCKETS - max_exact)).astype(jnp.int32),
                            N_BUCKETS - 1)
        bucket = jnp.where(d < max_exact, d, large)
        h = head_ref[t]
        val = jnp.zeros((BLK, BLK), F32)
        for b in range(N_BUCKETS):
            val = jnp.where(bucket == b, rel_ref[b, h], val)
        out_ref[k] = jnp.where(valid, val * _INV_SCALE, NEG_INF)


def _a_tile_counts(nblk):
    return [min(window // BLK + 1, nblk) for window, _ in A_GROUPS]


def _tile_tables(nblk):
    big = 2 ** 30
    fam = {}
    fam["a"] = [(dl * BLK, 1, -1, dil - 1, window if dl < cnt else -1, BLK,
                 g * A_HEADS_PER_GROUP + hh)
                for g, ((window, dil), cnt) in enumerate(zip(A_GROUPS, _a_tile_counts(nblk)))
                for hh in range(A_HEADS_PER_GROUP)
                for dl in range(cnt + 1)]
    fam["slc"] = [(dl * BLK, -1, 1, 0, big if dl < nblk else -1, BLK, A_HEADS + h)
                  for dl in range(nblk + 1) for h in range(B_HEADS)]
    n_win = WIN // BLK + 1
    fam["win"] = [(dl * BLK, -1, 1, 0, WIN - 1 if dl < n_win else -1, BLK, A_HEADS + h)
                  for dl in range(n_win + 1) for h in range(B_HEADS)]
    n_cmp = (nblk * BLK - CMP_LEN) // CMP_STRIDE + 1
    fam["cmp"] = [(i * BLK - (CMP_LEN - 1), -CMP_STRIDE, 1, 0, big, n_cmp, A_HEADS + h)
                  for i in range(nblk) for h in range(B_HEADS)]
    return {k: [np.asarray(c, np.int32) for c in zip(*rows)] for k, rows in fam.items()}


def _a_tile_offsets(nblk):
    cnt = _a_tile_counts(nblk)
    return [A_HEADS_PER_GROUP * sum(c + 1 for c in cnt[:g]) for g in range(len(A_GROUPS))]


_MAX_TILES_PER_STEP = 17


def _bias_tiles(rel_bias, nblk):
    out = {}
    for name, cols in _tile_tables(nblk).items():
        n = cols[0].shape[0]
        per_step = max(k for k in range(1, _MAX_TILES_PER_STEP + 1) if n % k == 0)
        out[name] = pl.pallas_call(
            _bias_tile_kernel,
            grid_spec=pltpu.PrefetchScalarGridSpec(
                num_scalar_prefetch=len(cols),
                grid=(n // per_step,),
                in_specs=[pl.BlockSpec(memory_space=pltpu.SMEM)],
                out_specs=pl.BlockSpec((per_step, BLK, BLK), lambda t, *_: (t, 0, 0)),
            ),
            out_shape=jax.ShapeDtypeStruct((n, BLK, BLK), F32),
            compiler_params=_params("arbitrary"),
            name="bias_tiles_" + name,
        )(*[jnp.asarray(c) for c in cols], rel_bias)
    return out


def _proj_kernel(x_ref, w_ref, o_ref, os_ref, xb_ref):
    j = pl.program_id(1)
    tn = w_ref.shape[1]
    n_side = os_ref.shape[1]

    @pl.when(j == 0)
    def _():
        xb_ref[...] = x_ref[...].astype(MXU_DTYPE)

    acc = _dot(xb_ref[...], w_ref[...])
    o_ref[...] = acc.astype(o_ref.dtype)

    @pl.when(j == pl.num_programs(1) - 1)
    def _():
        os_ref[...] = acc[:, tn - n_side:]


def _proj(x2, w_packed, layer, tm, tn):
    t, d = x2.shape
    n_all = _N_MAIN + _N_SMALL
    assert n_all % tn == 0 and tn >= _N_SMALL
    nj = n_all // tn
    return pl.pallas_call(
        _proj_kernel,
        grid=(t // tm, nj),
        in_specs=[pl.BlockSpec((tm, d), lambda i, j: (i, 0)),
                  pl.BlockSpec((None, d, tn), lambda i, j: (layer, 0, j))],
        out_specs=[pl.BlockSpec((tm, tn), lambda i, j: (i, j)),
                   pl.BlockSpec((tm, _N_SMALL), lambda i, j: (i, 0))],
        out_shape=[jax.ShapeDtypeStruct((t, _N_MAIN), MXU_DTYPE),
                   jax.ShapeDtypeStruct((t, _N_SMALL), F32)],
        scratch_shapes=[pltpu.VMEM((tm, d), MXU_DTYPE)],
        compiler_params=_params("arbitrary", "arbitrary"),
        name="in_proj",
    )(x2, w_packed)


def _lane_cat(xs):
    return jnp.concatenate(xs, axis=1)


def _softmax_pv(z, v, axis):
    m = jnp.max(z, axis=axis, keepdims=True)
    p = jnp.exp2((z - m) * _EXP2_SCALE)
    l = jnp.sum(p, axis=axis, keepdims=True)
    if axis == 1:
        o = _dot(p.astype(MXU_DTYPE), v)
    else:
        o = _dot(v, p.astype(MXU_DTYPE))
    return o / l, m, l


_CAUSAL_STEP = 4
_QA_PER_STEP = 4
_Q_PER_STEP = 2


def _mixer_a_kernel(q_ref, k_ref, v_ref, t_ref, o_ref, s0_ref, s1_ref, p0_ref, p1_ref, *,
                    tile_off, tile_cnt):
    hpg = A_HEADS_PER_GROUP
    n_groups = len(A_GROUPS)
    nblk = k_ref.shape[1] // BLK
    step = pl.program_id(1)
    s_bufs, p_bufs = (s0_ref, s1_ref), (p0_ref, p1_ref)

    def body(n_full):
        chains = [(sub, g, hh) for sub in range(_QA_PER_STEP) for g in range(n_groups)
                  for hh in range(hpg)]

        def window(sub, g):
            i = step * _QA_PER_STEP + sub
            if tile_cnt[g] == nblk:
                return i, 0, 0, n_full
            kb0 = jnp.maximum(i - (tile_cnt[g] - 1), 0)
            return i, kb0, pl.multiple_of(kb0 * BLK, BLK), tile_cnt[g]

        stats = [None] * len(chains)
        results = {}
        for t in range(len(chains) + 2):
            if 0 <= t - 1 < len(chains):
                c = t - 1
                sub, g, hh = chains[c]
                i, kb0, _, nkb = window(sub, g)
                nd = tile_cnt[g]
                tbase = tile_off[g] + hh * (nd + 1)
                deltas = [i - kb0 - j for j in range(nkb)]
                bias = _lane_cat([t_ref[tbase + jnp.where(dl >= 0, dl, nd)] for dl in deltas])
                z = s_bufs[c % 2][:, :nkb * BLK] + bias
                m = jnp.max(z, axis=1, keepdims=True)
                p = jnp.exp2((z - m) * _EXP2_SCALE)
                stats[c] = (m, jnp.sum(p, axis=1, keepdims=True))
                p_bufs[c % 2][:, :nkb * BLK] = p.astype(MXU_DTYPE)
            if t < len(chains):
                sub, g, hh = chains[t]
                _, _, start, nkb = window(sub, g)
                c0 = (g * hpg + hh) * HEAD_DIM
                q = q_ref[0, sub * BLK:(sub + 1) * BLK, c0:c0 + HEAD_DIM]
                k = k_ref[0, pl.ds(start, nkb * BLK), c0:c0 + HEAD_DIM]
                s_bufs[t % 2][:, :nkb * BLK] = _dot_nt(q, k)
            if t - 2 >= 0:
                c = t - 2
                sub, g, hh = chains[c]
                _, _, start, nkb = window(sub, g)
                c0 = (g * hpg + hh) * HEAD_DIM
                v = v_ref[0, pl.ds(start, nkb * BLK), c0:c0 + HEAD_DIM]
                m, l = stats[c]
                results[chains[c]] = (_dot(p_bufs[c % 2][:, :nkb * BLK], v) / l,
                                      m * ATTN_SCALE + jnp.log(l))
                if (g, hh) == (n_groups - 1, hpg - 1):
                    for h2 in range(hpg):
                        outs = [results[(sub, g2, h2)][0] for g2 in range(n_groups)]
                        lses = [results[(sub, g2, h2)][1] for g2 in range(n_groups)]
                        mx = functools.reduce(jnp.maximum, lses)
                        es = [jnp.exp(x - mx) for x in lses]
                        tot = functools.reduce(lambda a, b: a + b, es)
                        o = functools.reduce(lambda a, b: a + b,
                                             [(e / tot) * og for e, og in zip(es, outs)])
                        o_ref[0, sub * BLK:(sub + 1) * BLK, h2 * HEAD_DIM:(h2 + 1) * HEAD_DIM] = (
                            o.astype(o_ref.dtype))

    if nblk % _CAUSAL_STEP == 0 and _CAUSAL_STEP % _QA_PER_STEP == 0:
        for v in range(nblk // _CAUSAL_STEP):
            pl.when(step * _QA_PER_STEP // _CAUSAL_STEP == v)(
                functools.partial(body, (v + 1) * _CAUSAL_STEP))
    else:
        body(nblk)


def _mixer_a(u, tiles_a, nblk):
    b, s, _ = u.shape
    w = A_HEADS * HEAD_DIM
    kern = functools.partial(_mixer_a_kernel, tile_off=tuple(_a_tile_offsets(nblk)),
                             tile_cnt=tuple(_a_tile_counts(nblk)))
    n_a = tiles_a.shape[0]
    tq = _QA_PER_STEP * BLK
    return pl.pallas_call(
        kern,
        grid=(b, nblk // _QA_PER_STEP),
        in_specs=[pl.BlockSpec((1, tq, w), lambda bi, i: (bi, i, _COL_A_Q // A_HEADS)),
                  pl.BlockSpec((1, s, w), lambda bi, i: (bi, 0, _COL_A_K // A_HEADS)),
                  pl.BlockSpec((1, s, w), lambda bi, i: (bi, 0, _COL_A_V // A_HEADS)),
                  pl.BlockSpec((n_a, BLK, BLK), lambda bi, i: (0, 0, 0))],
        out_specs=pl.BlockSpec((1, tq, A_HEADS_PER_GROUP * HEAD_DIM), lambda bi, i: (bi, i, 0)),
        out_shape=jax.ShapeDtypeStruct((b, s, A_HEADS_PER_GROUP * HEAD_DIM), MXU_DTYPE),
        scratch_shapes=[pltpu.VMEM((BLK, s), F32), pltpu.VMEM((BLK, s), F32),
                        pltpu.VMEM((BLK, s), MXU_DTYPE), pltpu.VMEM((BLK, s), MXU_DTYPE)],
        compiler_params=_params("arbitrary", "arbitrary"),
        name="mixer_a",
    )(u, u, u, tiles_a)


_C_TK = 256


def _split3(x):
    hi = x.astype(MXU_DTYPE)
    r1 = x - hi.astype(F32)
    mid = r1.astype(MXU_DTYPE)
    lo = (r1 - mid.astype(F32)).astype(MXU_DTYPE)
    return hi, mid, lo


def _fox_decay_kernel(f_ref, bf_ref, c_ref):
    x = f_ref[0] + bf_ref[...]
    logf = jnp.minimum(x, 0.0) - jnp.log1p(jnp.exp(-jnp.abs(x)))
    logf_t = logf.T[_FORGET_ROW0:_FORGET_ROW0 + _FORGET_ROWS]
    nk = c_ref.shape[1]
    r = lax.broadcasted_iota(jnp.int32, (_C_TK, _C_TK), 0)
    c = lax.broadcasted_iota(jnp.int32, (_C_TK, _C_TK), 1)
    upper = (r <= c).astype(MXU_DTYPE)
    carry = jnp.zeros((_FORGET_ROWS, 1), F32)
    for j in range(nk):
        hi, mid, lo = _split3(logf_t[:, j * _C_TK:(j + 1) * _C_TK])
        cs = (_dot(hi, upper) + _dot(mid, upper)) + _dot(lo, upper) + carry
        c_ref[0, j] = cs
        carry = cs[:, _C_TK - 1:_C_TK]


def _fox_decay(u_small, b_f):
    b, s, _ = u_small.shape
    nk = s // _C_TK
    bf_pad = jnp.zeros((1, LANES), F32).at[0, _FORGET_LANE0:_FORGET_LANE0 + C_HEADS].set(b_f)
    return pl.pallas_call(
        _fox_decay_kernel,
        grid=(b,),
        in_specs=[pl.BlockSpec((1, s, LANES), lambda bi: (bi, 0, _SMALL_FORGET)),
                  pl.BlockSpec((1, LANES), lambda bi: (0, 0))],
        out_specs=pl.BlockSpec((1, nk, _FORGET_ROWS, _C_TK), lambda bi: (bi, 0, 0, 0)),
        out_shape=jax.ShapeDtypeStruct((b, nk, _FORGET_ROWS, _C_TK), F32),
        compiler_params=_params("arbitrary"),
        name="fox_decay",
    )(u_small, bf_pad)


_C_TQ = 256
_C_HEADS_PER_STEP = 2


def _mixer_c_kernel(q_ref, k_ref, v_ref, c_ref, o_ref, s0_ref, s1_ref, p0_ref, p1_ref):
    row0 = _FORGET_LANE0 - _FORGET_ROW0 + pl.program_id(1) * _C_HEADS_PER_STEP
    tq = _C_TQ
    nq = q_ref.shape[1] // tq
    r = lax.broadcasted_iota(jnp.int32, (tq, tq), 0)
    c = lax.broadcasted_iota(jnp.int32, (tq, tq), 1)
    s_bufs, p_bufs = (s0_ref, s1_ref), (p0_ref, p1_ref)
    chains = [(hh, i) for hh in range(_C_HEADS_PER_STEP) for i in range(nq)]
    denom = [None] * len(chains)
    for t in range(len(chains) + 2):
        if 0 <= t - 1 < len(chains):
            n = t - 1
            hh, i = chains[n]
            nk = (i + 1) * tq
            decay = _lane_cat([c_ref[0, j, pl.ds(row0 + hh, 1), :] for j in range(nk // _C_TK)])
            z = s_bufs[n % 2][:, :nk] - decay * _INV_SCALE
            diag = jnp.where(c <= r, z[:, nk - tq:], NEG_INF)
            z = diag if i == 0 else _lane_cat([z[:, :nk - tq], diag])
            m = jnp.max(z, axis=1, keepdims=True)
            p = jnp.exp2((z - m) * _EXP2_SCALE)
            denom[n] = jnp.sum(p, axis=1, keepdims=True)
            p_bufs[n % 2][:, :nk] = p.astype(MXU_DTYPE)
        if t < len(chains):
            hh, i = chains[t]
            nk = (i + 1) * tq
            cols = slice(hh * HEAD_DIM, (hh + 1) * HEAD_DIM)
            s_bufs[t % 2][:, :nk] = _dot_nt(q_ref[0, i * tq:(i + 1) * tq, cols],
                                           k_ref[0, :nk, cols])
        if t - 2 >= 0:
            n = t - 2
            hh, i = chains[n]
            nk = (i + 1) * tq
            cols = slice(hh * HEAD_DIM, (hh + 1) * HEAD_DIM)
            o = _dot(p_bufs[n % 2][:, :nk], v_ref[0, :nk, cols]) / denom[n]
            o_ref[0, i * tq:(i + 1) * tq, cols] = o.astype(o_ref.dtype)


def _mixer_c(u, cdec):
    b, s, _ = u.shape
    nk = s // _C_TK
    hps = _C_HEADS_PER_STEP
    assert C_HEADS % hps == 0 and all(c % hps == 0 for c in (_COL_C_Q, _COL_C_K, _COL_C_V))
    head = lambda col: pl.BlockSpec((1, s, hps * HEAD_DIM), lambda bi, h: (bi, 0, col // hps + h))
    return pl.pallas_call(
        _mixer_c_kernel,
        grid=(b, C_HEADS // hps),
        in_specs=[head(_COL_C_Q), head(_COL_C_K), head(_COL_C_V),
                  pl.BlockSpec((1, nk, _FORGET_ROWS, _C_TK), lambda bi, h: (bi, 0, 0, 0))],
        out_specs=head(0),
        out_shape=jax.ShapeDtypeStruct((b, s, C_HEADS * HEAD_DIM), MXU_DTYPE),
        scratch_shapes=[pltpu.VMEM((_C_TQ, s), F32), pltpu.VMEM((_C_TQ, s), F32),
                        pltpu.VMEM((_C_TQ, s), MXU_DTYPE), pltpu.VMEM((_C_TQ, s), MXU_DTYPE)],
        compiler_params=_params("arbitrary", "arbitrary"),
        name="mixer_c",
    )(u, u, u, cdec)


def _gelu(x):
    return 0.5 * x * (1.0 + lax.erf(x * np.float32(math.sqrt(0.5))))


def _nsa_compress_kernel(k_ref, v_ref, pe_ref, w1_ref, b1_ref, w2_ref, b2_ref, kc_ref, vct_ref):
    n_chunk = k_ref.shape[1] // CMP_STRIDE
    halves = CMP_LEN // CMP_STRIDE
    res = []
    for c, src in enumerate((k_ref, v_ref)):
        parts = []
        for half in range(halves):
            acc = jnp.zeros((n_chunk, HEAD_DIM), F32)
            for p in range(CMP_STRIDE):
                pos = half * CMP_STRIDE + p
                rows = src[0, pl.ds(p, n_chunk, stride=CMP_STRIDE), :] + pe_ref[c, pos:pos + 1, :]
                acc = acc + _dot(rows.astype(MXU_DTYPE), w1_ref[c, pos])
            parts.append(acc)
        hid = parts[0]
        for half in range(1, halves):
            hid = hid + pltpu.roll(parts[half], n_chunk - half, 0)
        hid = _gelu(hid + b1_ref[c:c + 1, :])
        res.append(_dot(hid.astype(MXU_DTYPE), w2_ref[c]) + b2_ref[c:c + 1, :])
    kc_ref[0] = res[0].astype(kc_ref.dtype)
    vct_ref[0] = res[1].T.astype(vct_ref.dtype)


def _nsa_compress(u_small, cmp_pe, cmp_w1, cmp_b1, cmp_w2, cmp_b2, layer):
    b, s, _ = u_small.shape
    n_chunk = s // CMP_STRIDE
    of_layer = lambda a: pl.BlockSpec((None,) + a.shape[1:],
                                      lambda bi: (layer,) + (0,) * (a.ndim - 1))
    return pl.pallas_call(
        _nsa_compress_kernel,
        grid=(b,),
        in_specs=[pl.BlockSpec((1, s, LANES), lambda bi: (bi, 0, _SMALL_KC)),
                  pl.BlockSpec((1, s, LANES), lambda bi: (bi, 0, _SMALL_VC)),
                  of_layer(cmp_pe), of_layer(cmp_w1), of_layer(cmp_b1),
                  of_layer(cmp_w2), of_layer(cmp_b2)],
        out_specs=[pl.BlockSpec((1, n_chunk, HEAD_DIM), lambda bi: (bi, 0, 0)),
                   pl.BlockSpec((1, HEAD_DIM, n_chunk), lambda bi: (bi, 0, 0))],
        out_shape=[jax.ShapeDtypeStruct((b, n_chunk, HEAD_DIM), MXU_DTYPE),
                   jax.ShapeDtypeStruct((b, HEAD_DIM, n_chunk), MXU_DTYPE)],
        compiler_params=_params("arbitrary"),
        name="nsa_compress",
    )(u_small, u_small, cmp_pe, cmp_w1, cmp_b1, cmp_w2, cmp_b2)


def _mixer_b_kernel(q_ref, ks_ref, vs_ref, kw_ref, vw_ref, kc_ref, vct_ref, gl_ref, bg_ref,
                    tslc_ref, twin_ref, tcmp_ref, o_ref, vst_ref, vwt_ref,
                    sc_ref, pc_ref, ss_ref, ps_ref, sw_ref, pw_ref, add_ref, oc_ref):
    step = pl.program_id(1)
    nblk = vwt_ref.shape[0]
    nh = B_HEADS
    n_slc = ks_ref.shape[1] // SLC_LEN
    n_win = twin_ref.shape[0] // nh - 1
    per_blk = BLK // SLC_LEN

    @pl.when(step == 0)
    def _():
        for kb in range(nblk):
            rows = slice(kb * BLK, (kb + 1) * BLK)
            vst_ref[:, rows] = vs_ref[0, rows, :].astype(F32).T.astype(MXU_DTYPE)
            vwt_ref[kb] = vw_ref[0, rows, :].astype(F32).T.astype(MXU_DTYPE)

    def bias_rows(t_ref, dl, n_real):
        idx = jnp.where(dl >= 0, dl, n_real)
        return _lane_cat([t_ref[idx * nh + h] for h in range(nh)])

    def body(n_keys_blk):
        n_keys = n_keys_blk * BLK
        denom = {}

        def block_index(sub):
            i = step * _Q_PER_STEP + sub
            kb0 = jnp.maximum(i - (n_win - 1), 0)
            return i, kb0

        def score_stage(sub):
            _, kb0 = block_index(sub)
            q4 = jnp.concatenate([q_ref[0, sub * BLK:(sub + 1) * BLK, h * HEAD_DIM:(h + 1) * HEAD_DIM]
                                  for h in range(nh)], axis=0)
            sc_ref[sub] = _dot_nt(kc_ref[0], q4)
            ss_ref[sub, :n_keys, :] = _dot_nt(ks_ref[0, :n_keys, :], q4)
            start = pl.multiple_of(kb0 * BLK, BLK)
            sw_ref[sub] = _dot_nt(kw_ref[0, pl.ds(start, n_win * BLK), :], q4)

        def cmp_win_softmax_stage(sub):
            i, kb0 = block_index(sub)
            st = sc_ref[sub] + _lane_cat([tcmp_ref[sub * nh + h] for h in range(nh)])
            m = jnp.max(st, axis=0, keepdims=True)
            e = jnp.exp2((st - m) * _EXP2_SCALE)
            l = jnp.sum(e, axis=0, keepdims=True)
            pc_ref[sub] = jnp.where(m > 0.5 * NEG_INF, e / l, 0.0).astype(MXU_DTYPE)
            bias = jnp.concatenate([bias_rows(twin_ref, i - kb0 - j, n_win) for j in range(n_win)],
                                   axis=0)
            z = sw_ref[sub] + bias
            m = jnp.max(z, axis=0, keepdims=True)
            p = jnp.exp2((z - m) * _EXP2_SCALE)
            denom["win", sub] = jnp.sum(p, axis=0, keepdims=True)
            pw_ref[sub] = p.astype(MXU_DTYPE)

        def select_stage(sub):
            i, _ = block_index(sub)
            pc = pc_ref[sub]
            oc_ref[sub] = _dot(vct_ref[0], pc)
            n_cmp_pad = kc_ref.shape[1]
            jj = lax.broadcasted_iota(jnp.int32, (n_slc, n_cmp_pad), 0)
            nn = lax.broadcasted_iota(jnp.int32, (n_slc, n_cmp_pad), 1)
            overlap = ((nn * CMP_STRIDE <= jj * SLC_LEN + SLC_LEN - 1)
                       & (nn * CMP_STRIDE + CMP_LEN - 1 >= jj * SLC_LEN)).astype(MXU_DTYPE)
            imp4 = _dot(overlap, pc)
            imp = functools.reduce(lambda a, b: a + b,
                                   [imp4[:, h * BLK:(h + 1) * BLK] for h in range(nh)])
            t = i * BLK + lax.broadcasted_iota(jnp.int32, (n_slc, BLK), 1)
            blk = lax.broadcasted_iota(jnp.int32, (n_slc, BLK), 0)
            cur = t // SLC_LEN
            forced = (blk == 0) | (blk == cur) | (blk == cur - 1)
            causal = blk * SLC_LEN <= t
            score = jnp.where(forced, FORCE_SCORE, jnp.where(causal, imp, NEG_INF))
            rank = jnp.zeros((n_slc, BLK), jnp.int32)
            for r in range(n_slc):
                row = score[r:r + 1, :]
                ahead = (row > score) | ((row == score) & (r < blk))
                rank = rank + ahead.astype(jnp.int32)
            add = jnp.where(rank < min(N_SELECT, n_slc), 0.0, NEG_INF).astype(F32)
            add_ref[sub] = _lane_cat([add] * nh)

        def selected_softmax_stage(sub):
            i, _ = block_index(sub)
            mask_rows = []
            for kb in range(n_keys_blk):
                sel = jnp.concatenate(
                    [jnp.broadcast_to(add_ref[sub, kb * per_blk + j:kb * per_blk + j + 1, :],
                                      (SLC_LEN, nh * BLK)) for j in range(per_blk)], axis=0)
                mask_rows.append(bias_rows(tslc_ref, i - kb, nblk) + sel)
            z = ss_ref[sub, :n_keys, :] + jnp.concatenate(mask_rows, axis=0)
            m = jnp.max(z, axis=0, keepdims=True)
            p = jnp.exp2((z - m) * _EXP2_SCALE)
            denom["slc", sub] = jnp.sum(p, axis=0, keepdims=True)
            ps_ref[sub, :n_keys, :] = p.astype(MXU_DTYPE)

        def output_stage(sub):
            _, kb0 = block_index(sub)
            rows = slice(sub * BLK, (sub + 1) * BLK)
            o_slc = _dot(vst_ref[:, :n_keys], ps_ref[sub, :n_keys, :]) / denom["slc", sub]
            o_win = (_dot(_lane_cat([vwt_ref[kb0 + j] for j in range(n_win)]), pw_ref[sub])
                     / denom["win", sub])
            gt = jax.nn.sigmoid(gl_ref[0, rows, :] + bg_ref[...]).T

            def gate(br):
                r0 = _GATE_LANE0 + br * nh
                return _lane_cat([gt[r0 + h:r0 + h + 1, :] for h in range(nh)])

            o_t = gate(0) * oc_ref[sub] + gate(1) * o_slc + gate(2) * o_win
            for h in range(nh):
                o_ref[0, rows, h * HEAD_DIM:(h + 1) * HEAD_DIM] = (
                    o_t[:, h * BLK:(h + 1) * BLK].T.astype(o_ref.dtype))

        stages = (score_stage, cmp_win_softmax_stage, select_stage, selected_softmax_stage,
                  output_stage)
        for t in range(len(stages) + _Q_PER_STEP - 1):
            for sub in range(_Q_PER_STEP):
                if 0 <= t - sub < len(stages):
                    stages[t - sub](sub)

    if nblk % _CAUSAL_STEP == 0 and _CAUSAL_STEP % _Q_PER_STEP == 0:
        for v in range(nblk // _CAUSAL_STEP):
            pl.when(step * _Q_PER_STEP // _CAUSAL_STEP == v)(
                functools.partial(body, (v + 1) * _CAUSAL_STEP))
    else:
        body(nblk)


def _mixer_b(u, u_small, kc, vct, b_nsa_gate, tiles, nblk):
    b, s, _ = u.shape
    nh = B_HEADS
    bg_pad = jnp.zeros((1, LANES), F32).at[0, _GATE_LANE0:_GATE_LANE0 + 3 * nh].set(b_nsa_gate)
    whole = lambda a: pl.BlockSpec(a.shape, lambda bi, i: (0, 0, 0))
    kv = lambda col: pl.BlockSpec((1, s, HEAD_DIM), lambda bi, i: (bi, 0, col))
    n_chunk = kc.shape[1]
    q, tq, wq = _Q_PER_STEP, _Q_PER_STEP * BLK, nh * BLK
    n_win = WIN // BLK + 1
    return pl.pallas_call(
        _mixer_b_kernel,
        grid=(b, nblk // _Q_PER_STEP),
        in_specs=[pl.BlockSpec((1, tq, nh * HEAD_DIM), lambda bi, i: (bi, i, _COL_B_Q // nh)),
                  kv(_COL_B_KS), kv(_COL_B_VS), kv(_COL_B_KW), kv(_COL_B_VW),
                  pl.BlockSpec((1, n_chunk, HEAD_DIM), lambda bi, i: (bi, 0, 0)),
                  pl.BlockSpec((1, HEAD_DIM, n_chunk), lambda bi, i: (bi, 0, 0)),
                  pl.BlockSpec((1, tq, LANES), lambda bi, i: (bi, i, _SMALL_GATE)),
                  pl.BlockSpec((1, LANES), lambda bi, i: (0, 0)),
                  whole(tiles["slc"]), whole(tiles["win"]),
                  pl.BlockSpec((_Q_PER_STEP * nh, BLK, BLK), lambda bi, i: (i, 0, 0))],
        out_specs=pl.BlockSpec((1, tq, nh * HEAD_DIM), lambda bi, i: (bi, i, 0)),
        out_shape=jax.ShapeDtypeStruct((b, s, nh * HEAD_DIM), MXU_DTYPE),
        scratch_shapes=[pltpu.VMEM((HEAD_DIM, s), MXU_DTYPE),
                        pltpu.VMEM((nblk, HEAD_DIM, BLK), MXU_DTYPE),
                        pltpu.VMEM((q, n_chunk, wq), F32), pltpu.VMEM((q, n_chunk, wq), MXU_DTYPE),
                        pltpu.VMEM((q, s, wq), F32), pltpu.VMEM((q, s, wq), MXU_DTYPE),
                        pltpu.VMEM((q, n_win * BLK, wq), F32),
                        pltpu.VMEM((q, n_win * BLK, wq), MXU_DTYPE),
                        pltpu.VMEM((q, s // SLC_LEN, wq), F32),
                        pltpu.VMEM((q, HEAD_DIM, wq), F32)],
        compiler_params=_params("arbitrary", "arbitrary"),
        name="mixer_b",
    )(u, u, u, u, u, kc, vct, u_small, bg_pad, tiles["slc"], tiles["win"], tiles["cmp"])


def _layer_norm(y, g, b):
    mu = jnp.mean(y, axis=-1, keepdims=True)
    yc = y - mu
    var = jnp.mean(yc * yc, axis=-1, keepdims=True)
    return yc * lax.rsqrt(var + LN_EPS) * g + b


def _merge_kernel(x_ref, oa_ref, ob_ref, oc_ref, wga_ref, wgb_ref, wgc_ref, bga_ref, bgb_ref,
                  bgc_ref, wpa_ref, wpb_ref, wpc_ref, wo_ref, g_ref, b_ref, y_ref, xb_ref, *,
                  alpha):
    j = pl.program_id(1)

    @pl.when(j == 0)
    def _():
        x = x_ref[...]
        xb_ref[...] = x.astype(MXU_DTYPE)
        y_ref[...] = alpha * x

    xb = xb_ref[...]
    mixed = None
    for o_ref, wg_ref, bg_ref, wp_ref in ((oa_ref, wga_ref, bga_ref, wpa_ref),
                                          (ob_ref, wgb_ref, bgb_ref, wpb_ref),
                                          (oc_ref, wgc_ref, bgc_ref, wpc_ref)):
        gate = jax.nn.sigmoid(_dot(xb, wg_ref[...]) + bg_ref[...])
        term = gate * _dot(o_ref[...], wp_ref[...])
        mixed = term if mixed is None else mixed + term
    y_ref[...] += _dot(mixed.astype(MXU_DTYPE), wo_ref[...])

    @pl.when(j == pl.num_programs(1) - 1)
    def _():
        y_ref[...] = _layer_norm(y_ref[...], g_ref[...], b_ref[...])


def _merge_ln(x2, oa, ob, oc, w_gate, b_gate, w_pa, w_pb, w_pc, w_out, ln_g, ln_b, layer, alpha,
              tm, tn):
    t, d = x2.shape
    nj = d // tn
    row = lambda k: pl.BlockSpec((tm, k), lambda i, j: (i, 0))
    colw = lambda k, shift: pl.BlockSpec((None, k, tn), lambda i, j: (layer, 0, shift * nj + j))
    vec = lambda shift: pl.BlockSpec((None, 1, tn), lambda i, j: (layer, 0, shift * nj + j))
    fullvec = pl.BlockSpec((None, 1, d), lambda i, j: (layer, 0, 0))
    return pl.pallas_call(
        functools.partial(_merge_kernel, alpha=alpha),
        grid=(t // tm, nj),
        in_specs=[row(d), row(oa.shape[1]), row(ob.shape[1]), row(oc.shape[1]),
                  colw(d, 0), colw(d, 1), colw(d, 2), vec(0), vec(1), vec(2),
                  colw(w_pa.shape[1], 0), colw(w_pb.shape[1], 0), colw(w_pc.shape[1], 0),
                  pl.BlockSpec((None, tn, d), lambda i, j: (layer, j, 0)), fullvec, fullvec],
        out_specs=pl.BlockSpec((tm, d), lambda i, j: (i, 0)),
        out_shape=jax.ShapeDtypeStruct((t, d), F32),
        scratch_shapes=[pltpu.VMEM((tm, d), MXU_DTYPE)],
        compiler_params=_params("arbitrary", "arbitrary"),
        name="merge_ln",
    )(x2, oa, ob, oc, w_gate, w_gate, w_gate, b_gate, b_gate, b_gate, w_pa, w_pb, w_pc, w_out,
      ln_g, ln_b)


_TAIL = 8


def _ffn_kernel(x_ref, wa_ref, wb_ref, cwa_ref, cwb_ref, cba_ref, cbb_ref, wd_ref, g_ref, b_ref,
                y_ref, xb_ref, ha_ref, hb_ref, tail_ref, *, alpha, tiles_per_seq):
    i = pl.program_id(0)
    j = pl.program_id(1)
    tm = x_ref.shape[0]
    tf = wa_ref.shape[1]

    @pl.when(j == 0)
    def _():
        x = x_ref[...]
        xb_ref[...] = x.astype(MXU_DTYPE)
        y_ref[...] = alpha * x

    @pl.when(i % tiles_per_seq == 0)
    def _():
        tail_ref[j] = jnp.zeros(tail_ref.shape[1:], F32)

    xb = xb_ref[...]
    for h_ref, w_ref, c0 in ((ha_ref, wa_ref, 0), (hb_ref, wb_ref, tf)):
        h_ref[0:_TAIL, :] = tail_ref[j, :, c0:c0 + tf]
        h_ref[_TAIL:, :] = _dot(xb, w_ref[...])
        tail_ref[j, :, c0:c0 + tf] = h_ref[tm:tm + _TAIL, :]

    def conv(h_ref, cw_ref, cb_ref):
        out = cb_ref[...]
        for tap in range(CONV_W):
            shift = CONV_W - 1 - tap
            out = out + cw_ref[tap:tap + 1, :] * h_ref[pl.ds(_TAIL - shift, tm), :]
        return out

    a = conv(ha_ref, cwa_ref, cba_ref)
    bb = conv(hb_ref, cwb_ref, cbb_ref)
    y_ref[...] += _dot((_gelu(a) * bb).astype(MXU_DTYPE), wd_ref[...])

    @pl.when(j == pl.num_programs(1) - 1)
    def _():
        y_ref[...] = _layer_norm(y_ref[...], g_ref[...], b_ref[...])


def _ffn_ln(x2, w_up, conv_w, conv_b, w_down, ln_g, ln_b, layer, alpha, seq, tm, tf):
    t, d = x2.shape
    f = w_down.shape[1]
    nj = f // tf
    assert seq % tm == 0 and CONV_W - 1 <= _TAIL
    half = lambda k, shift: pl.BlockSpec((None, k, tf), lambda i, j: (layer, 0, shift * nj + j))
    fullvec = pl.BlockSpec((None, 1, d), lambda i, j: (layer, 0, 0))
    return pl.pallas_call(
        functools.partial(_ffn_kernel, alpha=alpha, tiles_per_seq=seq // tm),
        grid=(t // tm, nj),
        in_specs=[pl.BlockSpec((tm, d), lambda i, j: (i, 0)),
                  half(d, 0), half(d, 1), half(CONV_W, 0), half(CONV_W, 1), half(1, 0), half(1, 1),
                  pl.BlockSpec((None, tf, d), lambda i, j: (layer, j, 0)), fullvec, fullvec],
        out_specs=pl.BlockSpec((tm, d), lambda i, j: (i, 0)),
        out_shape=jax.ShapeDtypeStruct((t, d), F32),
        scratch_shapes=[pltpu.VMEM((tm, d), MXU_DTYPE),
                        pltpu.VMEM((tm + _TAIL, tf), F32),
                        pltpu.VMEM((tm + _TAIL, tf), F32),
                        pltpu.VMEM((nj, _TAIL, 2 * tf), F32)],
        compiler_params=_params("arbitrary", "arbitrary"),
        name="ffn_ln",
    )(x2, w_up, w_up, conv_w, conv_w, conv_b, conv_b, w_down, ln_g, ln_b)


def _pack_moves():
    main_dst = {'a_q': _COL_A_Q, 'a_k': _COL_A_K, 'a_v': _COL_A_V,
                'c_q': _COL_C_Q, 'c_k': _COL_C_K, 'c_v': _COL_C_V, 'b_q': _COL_B_Q,
                'b_k_slc': _COL_B_KS, 'b_v_slc': _COL_B_VS, 'b_k_win': _COL_B_KW,
                'b_v_win': _COL_B_VW}
    moves = [(blk * HEAD_DIM, _SPLIT[n][0], _SPLIT[n][1] - _SPLIT[n][0])
             for n, blk in main_dst.items()]
    moves += [(_N_MAIN + _SMALL_KC * LANES, _SPLIT['b_k_cmp'][0], LANES),
              (_N_MAIN + _SMALL_VC * LANES, _SPLIT['b_v_cmp'][0], LANES),
              (_N_MAIN + _SMALL_GATE * LANES, _GATE_COL0, LANES),
              (_N_MAIN + _SMALL_FORGET * LANES, _FORGET_COL0, N_IN - _FORGET_COL0)]
    return moves


def _pack_kernel(w_ref, o_ref):
    tail0 = _N_MAIN + _SMALL_FORGET * LANES
    o_ref[0, :, tail0:] = jnp.zeros((o_ref.shape[1], o_ref.shape[2] - tail0), o_ref.dtype)
    for d0, s0, width in _pack_moves():
        o_ref[0, :, d0:d0 + width] = w_ref[0, :, s0:s0 + width].astype(o_ref.dtype)


def _pack_w_in(w_in, tk):
    depth, d, n_in = w_in.shape
    assert n_in == N_IN
    n_out = _N_MAIN + _N_SMALL
    return pl.pallas_call(
        _pack_kernel,
        grid=(depth, d // tk),
        in_specs=[pl.BlockSpec((1, tk, n_in), lambda l, i: (l, i, 0))],
        out_specs=pl.BlockSpec((1, tk, n_out), lambda l, i: (l, i, 0)),
        out_shape=jax.ShapeDtypeStruct((depth, d, n_out), MXU_DTYPE),
        compiler_params=_params("arbitrary", "arbitrary"),
        name="pack_w_in",
    )(w_in)


_TM_PROJ, _TN_PROJ = 1024, 2048
_TM_MERGE, _TN_MERGE = 512, 512
_TM_FFN, _TF_FFN = 512, 512


def kernel(x, rel_bias, w_in, b_f, b_nsa_gate, cmp_pe, cmp_w1, cmp_b1, cmp_w2, cmp_b2, w_gate, b_gate, w_pa, w_pb, w_pc, w_out, ln1_g, ln1_b, w_up, conv_w, conv_b, w_down, ln2_g, ln2_b):
    bsz, seq, d = x.shape
    depth = w_in.shape[0]
    nblk = seq // BLK
    alpha = (2 * depth) ** 0.25
    t = bsz * seq
    bf = lambda a: a.astype(MXU_DTYPE)
    row = lambda a: a[:, None, :]

    tiles = _bias_tiles(rel_bias, nblk)
    w_in_p = _pack_w_in(w_in, 256)
    cmp_w1_b = bf(cmp_w1).reshape(depth, 2, CMP_LEN, HEAD_DIM, HEAD_DIM)
    cmp_w2_b = bf(cmp_w2)
    w_gate_b, w_pa_b, w_pb_b, w_pc_b, w_out_b = (bf(w_gate), bf(w_pa), bf(w_pb), bf(w_pc),
                                                 bf(w_out))
    w_up_b, w_down_b = bf(w_up), bf(w_down)
    x2 = x.reshape(t, d)
    for l in range(depth):
        u, u_small = _proj(x2, w_in_p, l, _TM_PROJ, _TN_PROJ)
        u = u.reshape(bsz, seq, -1)
        u_small = u_small.reshape(bsz, seq, -1)

        o_a = _mixer_a(u, tiles["a"], nblk)
        kc, vct = _nsa_compress(u_small, cmp_pe, cmp_w1_b, cmp_b1, cmp_w2_b, cmp_b2, l)
        o_b = _mixer_b(u, u_small, kc, vct, b_nsa_gate[l], tiles, nblk)
        cdec = _fox_decay(u_small, b_f[l])
        o_c = _mixer_c(u, cdec)

        x1 = _merge_ln(x2, o_a.reshape(t, -1), o_b.reshape(t, -1), o_c.reshape(t, -1),
                       w_gate_b, row(b_gate), w_pa_b, w_pb_b, w_pc_b, w_out_b,
                       row(ln1_g), row(ln1_b), l, alpha, _TM_MERGE, _TN_MERGE)
        x2 = _ffn_ln(x1, w_up_b, conv_w, row(conv_b), w_down_b, row(ln2_g), row(ln2_b),
                     l, alpha, seq, _TM_FFN, _TF_FFN)
    return x2.reshape(bsz, seq, d)
```
